```python
import math
import jax, jax.numpy as jnp
from jax import lax
import numpy as np

D_MODEL = 1024
BATCH = 8
SEQ = 4096
DEPTH = 2

CTX_LEN = 256
GRID_W = 64
EPS = 1e-6
GN_EPS = 64e-5
ROPE_BASE = 10000.0
NEG_INF = -1e30

MIX_W = 256
N_BRANCH = 4
HEAD_DIM = 64

RWKV_HEADS = MIX_W // HEAD_DIM
RWKV_DECAY_RANK = 32
RWKV_ICLR_RANK = 32
RWKV_GATE_RANK = 64

GLA_HEADS = 4
GLA_DK = 32
GLA_DV = 64
GLA_RANK = 16
GLA_TAU = 16.0
GLA_CHUNK = 64

ATTN_HEADS = 4
ATTN_KV_HEADS = 2
WINDOW = 128
ATTN_BLOCK = 128

S5_GROUPS = 16
S5_GROUP_CH = MIX_W // S5_GROUPS
S5_STATE = 64

N_GROUPS = 4
EXPERTS_PER_GROUP = 8
N_EXPERTS = N_GROUPS * EXPERTS_PER_GROUP
EXPERT_HIDDEN = 512
TOP_K = 2
MOE_BLOCK = 128

RWKV_IN = 4 * MIX_W
GLA_IN = 2 * GLA_HEADS * GLA_DK + 2 * GLA_HEADS * GLA_DV + 2 * GLA_RANK
ATTN_IN = (ATTN_HEADS + 2 * ATTN_KV_HEADS) * HEAD_DIM
S5_IN = MIX_W
GATE_IN = N_BRANCH * D_MODEL
P_IN = RWKV_IN + GLA_IN + ATTN_IN + S5_IN + GATE_IN

kernel_name = "hybrid_rwkv_gla_swa_s5_hmoe_dit"


def rmsnorm(x, g):
    xf = x.astype(jnp.float32)
    y = xf * lax.rsqrt(jnp.mean(xf * xf, axis=-1, keepdims=True) + EPS)
    return (y * g.astype(jnp.float32)).astype(x.dtype)


def modulate(h, shift, scale):
    return h * (1.0 + scale) + shift


def to_heads(z, h):
    return z.reshape(z.shape[:-1] + (h, z.shape[-1] // h))


def centered_shift(z):
    zp = jnp.pad(z, ((0, 0), (1, 1), (0, 0)))
    return 0.5 * (zp[:, :-2] + zp[:, 2:])


def split_sections(u):
    o1 = RWKV_IN
    o2 = o1 + GLA_IN
    o3 = o2 + ATTN_IN
    o4 = o3 + S5_IN
    return jnp.split(u, [o1, o2, o3, o4], axis=-1)


def axial_rope_tables(rows):
    row = jnp.repeat(jnp.arange(rows, dtype=jnp.float32), GRID_W)
    col = jnp.tile(jnp.arange(GRID_W, dtype=jnp.float32), rows)
    n_freq = HEAD_DIM // 4
    inv_freq = ROPE_BASE ** (-jnp.arange(n_freq, dtype=jnp.float32) / n_freq)
    ang = jnp.concatenate([row[:, None] * inv_freq, col[:, None] * inv_freq], axis=-1)
    return jnp.cos(ang), jnp.sin(ang)


def apply_rope(z, cos, sin):
    half = HEAD_DIM // 2
    z1, z2 = z[..., :half], z[..., half:]
    cos, sin = cos[:, None, :], sin[:, None, :]
    return jnp.concatenate([z1 * cos - z2 * sin, z1 * sin + z2 * cos], axis=-1)


def rwkv7_scan(r, w, k, v, a, b, s0, reverse):
    def step(s, inp):
        r_t, w_t, k_t, v_t, a_t, b_t = inp
        sa = jnp.einsum('bhvk,bhk->bhv', s, a_t)
        s = s * w_t[:, :, None, :] + sa[..., :, None] * b_t[..., None, :] + v_t[..., :, None] * k_t[..., None, :]
        return s, jnp.einsum('bhvk,bhk->bhv', s, r_t)
    xs = tuple(jnp.moveaxis(z, 1, 0) for z in (r, w, k, v, a, b))
    s_last, ys = lax.scan(step, s0, xs, reverse=reverse)
    return jnp.moveaxis(ys, 0, 1), s_last


def rwkv7_mixer(u_lat, u_ctx, p, ctx_out):
    H = RWKV_HEADS

    def prep(u):
        u = u.astype(jnp.float32)
        mixed = u + (centered_shift(u) - u) * p['rwkv_mu'].reshape(-1)
        r, k, v, xa = jnp.split(mixed, 4, axis=-1)
        kk = to_heads(k * p['rwkv_kk'], H)
        kk = kk * lax.rsqrt(jnp.sum(kk * kk, axis=-1, keepdims=True) + EPS)
        return r, k, v, xa, kk

    def direction(k, xa, kk, d):
        log_w = -jnp.exp(-jax.nn.softplus(-(p['rwkv_w0'][d] + jnp.tanh(xa @ p['rwkv_w1'][d]) @ p['rwkv_w2'][d])) - 0.5)
        iclr = jax.nn.sigmoid(p['rwkv_a0'][d] + (xa @ p['rwkv_a1'][d]) @ p['rwkv_a2'][d])
        k_eff = k * (1.0 + (iclr - 1.0) * p['rwkv_ka'])
        return jnp.exp(to_heads(log_w, H)), to_heads(k_eff, H), -kk, kk * to_heads(iclr, H)

    rl, kl, vl, xal, kkl = prep(u_lat)
    rc, kc, vc, xac, kkc = prep(u_ctx)
    rlh, vlh, rch, vch = to_heads(rl, H), to_heads(vl, H), to_heads(rc, H), to_heads(vc, H)
    s0 = jnp.zeros((u_lat.shape[0], H, HEAD_DIM, HEAD_DIM), jnp.float32)
    y_lat, y_ctx = 0.0, 0.0
    for d in range(2):
        rev = d == 1
        wc, kec, ac, bc = direction(kc, xac, kkc, d)
        yc_d, s_ctx = rwkv7_scan(rch, wc, kec, vch, ac, bc, s0, rev)
        wl, kel, al, bl = direction(kl, xal, kkl, d)
        yl_d, _ = rwkv7_scan(rlh, wl, kel, vlh, al, bl, s_ctx, rev)
        y_lat = y_lat + yl_d
        y_ctx = y_ctx + yc_d

    def finish(y, r, k, v, xa):
        mu = jnp.mean(y, axis=-1, keepdims=True)
        var = jnp.mean(jnp.square(y - mu), axis=-1, keepdims=True)
        y = ((y - mu) * lax.rsqrt(var + GN_EPS)).reshape(r.shape) * p['rwkv_ln_g']
        rh, kh, vh = to_heads(r, H), to_heads(k, H), to_heads(v, H)
        bonus = (jnp.sum(rh * kh * p['rwkv_rk'], axis=-1, keepdims=True) * vh).reshape(r.shape)
        g = jax.nn.sigmoid(xa @ p['rwkv_g1']) @ p['rwkv_g2']
        return (y + bonus) * g

    out_lat = finish(y_lat, rl, kl, vl, xal)
    out_ctx = finish(y_ctx, rc, kc, vc, xac) if ctx_out else None
    return out_lat, out_ctx


def gla_chunked(q, k, v, log_a, s0):
    b, h, t, _ = q.shape
    dv = v.shape[-1]
    n = t // GLA_CHUNK

    def chunks(z):
        return z.reshape(b, h, n, GLA_CHUNK, z.shape[-1])
    q, k, v, log_a = chunks(q), chunks(k), chunks(v), chunks(log_a)
    cum = jnp.cumsum(log_a, axis=3)
    total = cum[:, :, :, -1:, :]
    q_dec = q * jnp.exp(cum)
    k_inv = k * jnp.exp(-cum)
    k_end = k * jnp.exp(total - cum)
    prefix = jnp.tril(jnp.ones((GLA_CHUNK, GLA_CHUNK), jnp.float32))
    scores = jnp.einsum('bhnik,bhnjk->bhnij', q_dec, k_inv) * prefix
    o_intra = jnp.einsum('bhnij,bhnjv->bhniv', scores, v)
    delta = jnp.einsum('bhnjk,bhnjv->bhnkv', k_end, v)
    chunk_decay = jnp.exp(total[:, :, :, 0, :])

    def step(s, inp):
        dcy, ds = inp
        return s * dcy[..., None] + ds, s
    s_last, s_enter = lax.scan(step, s0, (jnp.moveaxis(chunk_decay, 2, 0), jnp.moveaxis(delta, 2, 0)))
    o_inter = jnp.einsum('bhnik,bhnkv->bhniv', q_dec, jnp.moveaxis(s_enter, 0, 2))
    return (o_intra + o_inter).reshape(b, h, t, dv), s_last


def gla_mixer(u_lat, u_ctx, p, ctx_out):
    qk, vw = GLA_HEADS * GLA_DK, GLA_HEADS * GLA_DV

    def bhtd(z):
        return jnp.moveaxis(to_heads(z, GLA_HEADS), 2, 1)

    def prep(u):
        u = u.astype(jnp.float32)
        q, k, v, r, alow = jnp.split(u, [qk, 2 * qk, 2 * qk + vw, 2 * qk + 2 * vw], axis=-1)
        log_a = [bhtd(jax.nn.log_sigmoid(alow[..., d * GLA_RANK:(d + 1) * GLA_RANK] @ p['gla_a2'][d] + p['gla_ab'][d]) / GLA_TAU)
                 for d in range(2)]
        return bhtd(q) * GLA_DK ** -0.5, bhtd(k), bhtd(v), r, log_a

    ql, kl, vl, rl, al = prep(u_lat)
    qc, kc, vc, rc, ac = prep(u_ctx)
    s0 = jnp.zeros((u_lat.shape[0], GLA_HEADS, GLA_DK, GLA_DV), jnp.float32)
    o_lat, o_ctx = 0.0, 0.0
    for d in range(2):
        if d == 1:
            flip = lambda z: jnp.flip(z, axis=2)
        else:
            flip = lambda z: z
        oc_d, s_ctx = gla_chunked(flip(qc), flip(kc), flip(vc), flip(ac[d]), s0)
        ol_d, _ = gla_chunked(flip(ql), flip(kl), flip(vl), flip(al[d]), s_ctx)
        o_lat = o_lat + flip(ol_d)
        o_ctx = o_ctx + flip(oc_d)

    def finish(o, r):
        o = jnp.moveaxis(o, 1, 2)
        o = o * lax.rsqrt(jnp.mean(o * o, axis=-1, keepdims=True) + EPS)
        return o.reshape(r.shape) * p['gla_ln_g'] * jax.nn.silu(r)

    return finish(o_lat, rl), (finish(o_ctx, rc) if ctx_out else None)


def window_attention(u_lat, u_ctx, p, cos, sin, ctx_out):
    H, KVH, hd, BLK = ATTN_HEADS, ATTN_KV_HEADS, HEAD_DIM, ATTN_BLOCK
    G = H // KVH
    scale = hd ** -0.5

    def split_qkv(u):
        q, k, v = jnp.split(u.astype(jnp.float32), [H * hd, (H + KVH) * hd], axis=-1)
        return to_heads(q, H), to_heads(k, KVH), to_heads(v, KVH)

    q, k, v = split_qkv(u_lat)
    qc, kc, vc = split_qkv(u_ctx)
    q = apply_rope(q, cos, sin) * scale
    k = apply_rope(k, cos, sin)
    b, t = q.shape[0], q.shape[1]
    nb = t // BLK
    lc = kc.shape[1]
    sink = p['attn_sink'].astype(jnp.float32).reshape(KVH, G)

    qb = q.reshape(b, nb, BLK, KVH, G, hd)

    def band(z):
        zp = jnp.pad(z, ((0, 0), (BLK, BLK), (0, 0), (0, 0))).reshape(b, nb + 2, BLK, KVH, hd)
        return jnp.concatenate([zp[:, :-2], zp[:, 1:-1], zp[:, 2:]], axis=2)
    kw, vw = band(k), band(v)
    s_w = jnp.einsum('bnqhgd,bnkhd->bnhgqk', qb, kw)
    s_c = jnp.einsum('bnqhgd,bkhd->bnhgqk', qb, kc)
    qpos = jnp.arange(nb)[:, None, None] * BLK + jnp.arange(BLK)[None, :, None]
    kpos = jnp.arange(nb)[:, None, None] * BLK - BLK + jnp.arange(3 * BLK)[None, None, :]
    valid = (jnp.abs(qpos - kpos) <= WINDOW) & (kpos >= 0) & (kpos < t)
    s_w = jnp.where(valid[None, :, None, None], s_w, NEG_INF)
    s_sink = jnp.broadcast_to(sink[None, None, :, :, None, None], s_w.shape[:-1] + (1,))
    probs = jax.nn.softmax(jnp.concatenate([s_w, s_c, s_sink], axis=-1), axis=-1)
    o = (jnp.einsum('bnhgqk,bnkhd->bnqhgd', probs[..., :3 * BLK], vw)
         + jnp.einsum('bnhgqk,bkhd->bnqhgd', probs[..., 3 * BLK:3 * BLK + lc], vc))
    out_lat = o.reshape(b, t, H * hd)

    out_ctx = None
    if ctx_out:
        qcs = qc.reshape(b, lc, KVH, G, hd) * scale
        sc = jnp.einsum('bqhgd,bkhd->bhgqk', qcs, kc)
        sc_sink = jnp.broadcast_to(sink[None, :, :, None, None], sc.shape[:-1] + (1,))
        pc = jax.nn.softmax(jnp.concatenate([sc, sc_sink], axis=-1), axis=-1)
        out_ctx = jnp.einsum('bhgqk,bkhd->bqhgd', pc[..., :lc], vc).reshape(b, lc, H * hd)
    return out_lat, out_ctx


def lti_scan(a_bar, bu, x0):
    bu = bu.at[:, 0].add(a_bar * x0)
    a = jnp.broadcast_to(a_bar, bu.shape)

    def combine(e1, e2):
        a1, b1 = e1
        a2, b2 = e2
        return a1 * a2, a2 * b1 + b2
    return lax.associative_scan(combine, (a, bu), axis=1)[1]


def s5_mixer(u_lat, u_ctx, p, ctx_out):
    f32 = jnp.float32

    def groups(u):
        return u.astype(f32).reshape(u.shape[0], u.shape[1], S5_GROUPS, S5_GROUP_CH)
    ul, uc = groups(u_lat), groups(u_ctx)
    b_mat = lax.complex(p['s5_b_re'].astype(f32), p['s5_b_im'].astype(f32))
    d_skip = p['s5_d'].astype(f32)
    y_lat, y_ctx = d_skip * ul, d_skip * uc
    x0 = jnp.zeros((u_lat.shape[0], S5_GROUPS, S5_STATE), jnp.complex64)
    for d in range(2):
        if d == 1:
            flip = lambda z: jnp.flip(z, axis=1)
        else:
            flip = lambda z: z
        lam = lax.complex(p['s5_lam_re'][d].astype(f32), p['s5_lam_im'][d].astype(f32))
        dt = jnp.exp(p['s5_log_dt'][d].astype(f32))[:, None]
        a_bar = jnp.exp(lam * dt)
        b_bar = ((a_bar - 1.0) / lam)[..., None] * b_mat
        c_mat = lax.complex(p['s5_c_re'][d].astype(f32), p['s5_c_im'][d].astype(f32))
        bu_c = jnp.einsum('gpc,btgc->btgp', b_bar, flip(uc).astype(jnp.complex64))
        bu_l = jnp.einsum('gpc,btgc->btgp', b_bar, flip(ul).astype(jnp.complex64))
        xs_c = lti_scan(a_bar, bu_c, x0)
        xs_l = lti_scan(a_bar, bu_l, xs_c[:, -1])
        y_lat = y_lat + jnp.real(jnp.einsum('gcp,btgp->btgc', c_mat, flip(xs_l)))
        if ctx_out:
            y_ctx = y_ctx + jnp.real(jnp.einsum('gcp,btgp->btgc', c_mat, flip(xs_c)))

    def finish(y):
        y = jax.nn.gelu(y.reshape(y.shape[0], y.shape[1], MIX_W))
        return y * jax.nn.sigmoid(y @ p['s5_w_glu'] + p['s5_b_glu'])
    return finish(y_lat), (finish(y_ctx) if ctx_out else None)


def merge_branches(ys, gate_logits, p):
    gates = jnp.split(gate_logits.astype(jnp.float32), N_BRANCH, axis=-1)
    merged = 0.0
    for i in range(N_BRANCH):
        merged = merged + jax.nn.sigmoid(gates[i]) * (ys[i] @ p['w_branch'][i])
    return merged @ p['w_out']


def hier_moe(h, p):
    n, d = h.shape
    hf = h.astype(jnp.float32)
    rows = jnp.arange(n)
    g_logits = hf @ p['w_router_g'] + p['b_router_g']
    g_sel = jnp.argmax(g_logits, axis=-1).astype(jnp.int32)
    g_w = jax.nn.softmax(g_logits, axis=-1)[rows, g_sel]
    e_logits = (hf @ p['w_router_e'] + p['b_router_e']).reshape(n, N_GROUPS, EXPERTS_PER_GROUP)[rows, g_sel]
    top_v, top_i = lax.top_k(e_logits, TOP_K)
    weights = jax.nn.softmax(top_v, axis=-1) * g_w[:, None]
    expert = g_sel[:, None] * EXPERTS_PER_GROUP + top_i.astype(jnp.int32)

    flat_e = expert.reshape(-1)
    flat_w = weights.reshape(-1)
    flat_tok = jnp.arange(n * TOP_K, dtype=jnp.int32) // TOP_K
    order = jnp.argsort(flat_e)
    se, stok, sw = flat_e[order], flat_tok[order], flat_w[order]
    counts = jnp.zeros((N_EXPERTS,), jnp.int32).at[flat_e].add(1)
    padded = (counts + MOE_BLOCK - 1) // MOE_BLOCK * MOE_BLOCK
    pend = jnp.cumsum(padded)
    pstart = pend - padded
    start = jnp.cumsum(counts) - counts
    dest = pstart[se] + (jnp.arange(n * TOP_K, dtype=jnp.int32) - start[se])
    cap = -(-(n * TOP_K) // MOE_BLOCK) * MOE_BLOCK + N_EXPERTS * MOE_BLOCK
    n_blk = cap // MOE_BLOCK
    buf_tok = jnp.full((cap,), n, jnp.int32).at[dest].set(stok)
    buf_w = jnp.zeros((cap,), jnp.float32).at[dest].set(sw)
    blk_e = jnp.minimum(jnp.searchsorted(pend, jnp.arange(n_blk, dtype=jnp.int32) * MOE_BLOCK, side='right'),
                        N_EXPERTS - 1)
    h_pad = jnp.concatenate([h, jnp.zeros((1, d), h.dtype)], axis=0)
    xb = h_pad[buf_tok].reshape(n_blk, MOE_BLOCK, d)

    def expert_block(args):
        x_blk, e = args
        hid = jax.nn.silu(x_blk @ p['w_exp_gate'][e]) * (x_blk @ p['w_exp_up'][e])
        return hid @ p['w_exp_down'][e]
    y_buf = lax.map(expert_block, (xb, blk_e)).reshape(cap, d) * buf_w[:, None]
    return jax.ops.segment_sum(y_buf, buf_tok, num_segments=n + 1)[:n].astype(h.dtype)


def adaln(cvec, p):
    return jnp.split(jax.nn.silu(cvec) @ p['w_ada'] + p['b_ada'], 6, axis=-1)


def hybrid_layer(x, xc, c, c_ctx, p, cos, sin, ctx_out):
    mods = [m[:, None, :] for m in adaln(c, p)]
    cmods = adaln(c_ctx, p)
    h = modulate(rmsnorm(x, p['norm1_g']), mods[0], mods[1])
    hc = modulate(rmsnorm(xc, p['norm1_g']), cmods[0], cmods[1])
    ua, ub, uatt, us5, ug = split_sections(h @ p['w_in'])
    ca, cb, catt, cs5, cg = split_sections(hc @ p['w_in'])
    ya, yac = rwkv7_mixer(ua, ca, p, ctx_out)
    yb, ybc = gla_mixer(ub, cb, p, ctx_out)
    yc, ycc = window_attention(uatt, catt, p, cos, sin, ctx_out)
    yd, ydc = s5_mixer(us5, cs5, p, ctx_out)
    x = x + (mods[2] * merge_branches((ya, yb, yc, yd), ug, p)).astype(x.dtype)
    if ctx_out:
        xc = xc + (cmods[2] * merge_branches((yac, ybc, ycc, ydc), cg, p)).astype(xc.dtype)

    b, t, d = x.shape
    tokens = modulate(rmsnorm(x, p['norm2_g']), mods[3], mods[4]).reshape(-1, d)
    if ctx_out:
        h2c = modulate(rmsnorm(xc, p['norm2_g']), cmods[3], cmods[4])
        tokens = jnp.concatenate([tokens, h2c.reshape(-1, d)], axis=0)
    y = hier_moe(tokens, p)
    x = x + (mods[5] * y[:b * t].reshape(b, t, d)).astype(x.dtype)
    if ctx_out:
        xc = xc + (cmods[5] * y[b * t:].reshape(xc.shape)).astype(xc.dtype)
    return x, xc


def setup_inputs(seed: int = 0) -> dict:
    key = jax.random.key(seed)
    ks = iter(jax.random.split(key, 64))
    f32 = jnp.float32

    def nrm(shape, scale):
        return scale * jax.random.normal(next(ks), shape, f32)

    def uni(shape, lo, hi):
        return jax.random.uniform(next(ks), shape, f32, lo, hi)

    L, D = DEPTH, D_MODEL
    G, P, CH = S5_GROUPS, S5_STATE, S5_GROUP_CH
    return {
        'x': nrm((BATCH, SEQ, D), 1.0),
        'c': nrm((BATCH, D), 1.0),
        'ctx': nrm((BATCH, CTX_LEN, D), 1.0),
        'c_ctx': nrm((D,), 1.0),
        'norm1_g': 1.0 + nrm((L, D), 0.02),
        'norm2_g': 1.0 + nrm((L, D), 0.02),
        'final_norm_g': 1.0 + nrm((D,), 0.02),
        'w_ada': nrm((L, D, 6 * D), 0.5 * D ** -0.5),
        'b_ada': nrm((L, 6 * D), 0.02),
        'w_in': nrm((L, D, P_IN), D ** -0.5),
        'rwkv_mu': uni((L, 4, MIX_W), 0.0, 1.0),
        'rwkv_w0': uni((L, 2, MIX_W), -5.0, 0.5),
        'rwkv_w1': nrm((L, 2, MIX_W, RWKV_DECAY_RANK), MIX_W ** -0.5),
        'rwkv_w2': nrm((L, 2, RWKV_DECAY_RANK, MIX_W), 0.5 * RWKV_DECAY_RANK ** -0.5),
        'rwkv_a0': nrm((L, 2, MIX_W), 0.1),
        'rwkv_a1': nrm((L, 2, MIX_W, RWKV_ICLR_RANK), MIX_W ** -0.5),
        'rwkv_a2': nrm((L, 2, RWKV_ICLR_RANK, MIX_W), 0.5 * RWKV_ICLR_RANK ** -0.5),
        'rwkv_kk': 0.85 + nrm((L, MIX_W), 0.02),
        'rwkv_ka': 1.0 + nrm((L, MIX_W), 0.02),
        'rwkv_rk': nrm((L, RWKV_HEADS, HEAD_DIM), 0.1),
        'rwkv_g1': nrm((L, MIX_W, RWKV_GATE_RANK), MIX_W ** -0.5),
        'rwkv_g2': nrm((L, RWKV_GATE_RANK, MIX_W), RWKV_GATE_RANK ** -0.5),
        'rwkv_ln_g': 1.0 + nrm((L, MIX_W), 0.02),
        'gla_a2': nrm((L, 2, GLA_RANK, GLA_HEADS * GLA_DK), GLA_RANK ** -0.5),
        'gla_ab': nrm((L, 2, GLA_HEADS * GLA_DK), 0.1),
        'gla_ln_g': 1.0 + nrm((L, MIX_W), 0.02),
        'attn_sink': nrm((L, ATTN_HEADS), 0.5),
        's5_lam_re': -0.5 + nrm((L, 2, G, P), 0.01),
        's5_lam_im': jnp.pi * jnp.arange(P, dtype=f32) + nrm((L, 2, G, P), 0.01),
        's5_log_dt': uni((L, 2, G), math.log(1e-3), math.log(1e-1)),
        's5_b_re': nrm((L, G, P, CH), (2 * CH) ** -0.5),
        's5_b_im': nrm((L, G, P, CH), (2 * CH) ** -0.5),
        's5_c_re': nrm((L, 2, G, CH, P), P ** -0.5),
        's5_c_im': nrm((L, 2, G, CH, P), P ** -0.5),
        's5_d': nrm((L, G, CH), 1.0),
        's5_w_glu': nrm((L, MIX_W, MIX_W), MIX_W ** -0.5),
        's5_b_glu': nrm((L, MIX_W), 0.02),
        'w_branch': nrm((L, N_BRANCH, MIX_W, D), MIX_W ** -0.5),
        'w_out': nrm((L, D, D), D ** -0.5),
        'w_router_g': nrm((L, D, N_GROUPS), D ** -0.5),
        'b_router_g': nrm((L, N_GROUPS), 0.01),
        'w_router_e': nrm((L, D, N_EXPERTS), D ** -0.5),
        'b_router_e': nrm((L, N_EXPERTS), 0.01),
        'w_exp_gate': nrm((L, N_EXPERTS, D, EXPERT_HIDDEN), D ** -0.5),
        'w_exp_up': nrm((L, N_EXPERTS, D, EXPERT_HIDDEN), D ** -0.5),
        'w_exp_down': nrm((L, N_EXPERTS, EXPERT_HIDDEN, D), EXPERT_HIDDEN ** -0.5),
    }


def reference(x, c, ctx, c_ctx, norm1_g, norm2_g, final_norm_g, w_ada, b_ada, w_in,
              rwkv_mu, rwkv_w0, rwkv_w1, rwkv_w2, rwkv_a0, rwkv_a1, rwkv_a2, rwkv_kk, rwkv_ka,
              rwkv_rk, rwkv_g1, rwkv_g2, rwkv_ln_g, gla_a2, gla_ab, gla_ln_g, attn_sink,
              s5_lam_re, s5_lam_im, s5_log_dt, s5_b_re, s5_b_im, s5_c_re, s5_c_im, s5_d,
              s5_w_glu, s5_b_glu, w_branch, w_out, w_router_g, b_router_g, w_router_e, b_router_e,
              w_exp_gate, w_exp_up, w_exp_down):
    rows = x.shape[1] // GRID_W
    cos, sin = axial_rope_tables(rows)
    xc = ctx
    for l in range(DEPTH):
        p = {
            'norm1_g': norm1_g[l], 'norm2_g': norm2_g[l], 'w_ada': w_ada[l], 'b_ada': b_ada[l],
            'w_in': w_in[l],
            'rwkv_mu': rwkv_mu[l], 'rwkv_w0': rwkv_w0[l], 'rwkv_w1': rwkv_w1[l], 'rwkv_w2': rwkv_w2[l],
            'rwkv_a0': rwkv_a0[l], 'rwkv_a1': rwkv_a1[l], 'rwkv_a2': rwkv_a2[l], 'rwkv_kk': rwkv_kk[l],
            'rwkv_ka': rwkv_ka[l], 'rwkv_rk': rwkv_rk[l], 'rwkv_g1': rwkv_g1[l], 'rwkv_g2': rwkv_g2[l],
            'rwkv_ln_g': rwkv_ln_g[l],
            'gla_a2': gla_a2[l], 'gla_ab': gla_ab[l], 'gla_ln_g': gla_ln_g[l],
            'attn_sink': attn_sink[l],
            's5_lam_re': s5_lam_re[l], 's5_lam_im': s5_lam_im[l], 's5_log_dt': s5_log_dt[l],
            's5_b_re': s5_b_re[l], 's5_b_im': s5_b_im[l], 's5_c_re': s5_c_re[l], 's5_c_im': s5_c_im[l],
            's5_d': s5_d[l], 's5_w_glu': s5_w_glu[l], 's5_b_glu': s5_b_glu[l],
            'w_branch': w_branch[l], 'w_out': w_out[l],
            'w_router_g': w_router_g[l], 'b_router_g': b_router_g[l],
            'w_router_e': w_router_e[l], 'b_router_e': b_router_e[l],
            'w_exp_gate': w_exp_gate[l], 'w_exp_up': w_exp_up[l], 'w_exp_down': w_exp_down[l],
        }
        x, xc = hybrid_layer(x, xc, c, c_ctx, p, cos, sin, l < DEPTH - 1)
    return rmsnorm(x, final_norm_g)
```

```python
import functools
import math

import jax
import jax.numpy as jnp
from jax import lax
from jax.experimental import pallas as pl
from jax.experimental.pallas import tpu as pltpu

F32 = jnp.float32
BF16 = jnp.bfloat16
HI = lax.Precision.HIGHEST

EPS = 1e-6
GN_EPS = 64e-5
ROPE_BASE = 10000.0
NEG_INF = -1e30
GRID_W = 64

MIX_W = 256
N_BRANCH = 4
HEAD_DIM = 64
RWKV_HEADS = MIX_W // HEAD_DIM
GLA_HEADS = 4
GLA_DK = 32
GLA_DV = 64
GLA_RANK = 16
GLA_TAU = 16.0
ATTN_HEADS = 4
ATTN_KV_HEADS = 2
WINDOW = 128
ATTN_BLOCK = 128
S5_GROUPS = 16
S5_GROUP_CH = MIX_W // S5_GROUPS
S5_STATE = 64
N_GROUPS = 4
EXPERTS_PER_GROUP = 8
N_EXPERTS = N_GROUPS * EXPERTS_PER_GROUP
TOP_K = 2
MOE_BLOCK = 128

RWKV_IN = 4 * MIX_W
GLA_IN = 2 * GLA_HEADS * GLA_DK + 2 * GLA_HEADS * GLA_DV + 2 * GLA_RANK
GLA_IN_PAD = 896
ATTN_IN = (ATTN_HEADS + 2 * ATTN_KV_HEADS) * HEAD_DIM
S5_IN = MIX_W
MIX_IN_PAD = RWKV_IN + GLA_IN_PAD + ATTN_IN + S5_IN

CHUNK = 64
TOKEN_TILE = 256
VMEM_LIMIT = 56 * 1024 * 1024


def _params(sem):
    return pltpu.CompilerParams(dimension_semantics=sem, vmem_limit_bytes=VMEM_LIMIT)


def _sigmoid(z):
    return 1.0 / (1.0 + jnp.exp(-z))


def _dot(a, b, prec=None):
    return jnp.dot(a, b, precision=prec, preferred_element_type=F32)


def _dot_nt(a, b, prec=None):
    return lax.dot_general(a, b, (((1,), (1,)), ((), ())), precision=prec, preferred_element_type=F32)


def _dot_tn(a, b, prec=None):
    return lax.dot_general(a, b, (((0,), (0,)), ((), ())), precision=prec, preferred_element_type=F32)


def _seg_ones(width, seg):
    r = lax.broadcasted_iota(jnp.int32, (width, width), 0) // seg
    c = lax.broadcasted_iota(jnp.int32, (width, width), 1) // seg
    return (r == c).astype(F32)


def _order_mask(n, reverse, inclusive):
    t = lax.broadcasted_iota(jnp.int32, (n, n), 0)
    i = lax.broadcasted_iota(jnp.int32, (n, n), 1)
    if reverse:
        return (i >= t) if inclusive else (i > t)
    return (i <= t) if inclusive else (i < t)


def _norm_mod(x, g, shift, scale):
    ms = jnp.mean(x * x, axis=-1, keepdims=True)
    return (x * lax.rsqrt(ms + EPS) * g) * (1.0 + scale) + shift


def _ada_kernel(c_ref, w_ref, b_ref, o_ref):
    cv = c_ref[...]
    o_ref[...] = _dot(cv * _sigmoid(cv), w_ref[...], HI) + b_ref[...]


def adaln_mods(cvec, w_ada, b_ada):
    rows, d = cvec.shape
    n = w_ada.shape[1]
    tn = 1024
    return pl.pallas_call(
        _ada_kernel,
        out_shape=jax.ShapeDtypeStruct((rows, n), F32),
        grid=(n // tn,),
        in_specs=[pl.BlockSpec((rows, d), lambda j: (0, 0)),
                  pl.BlockSpec((d, tn), lambda j: (0, j)),
                  pl.BlockSpec((1, tn), lambda j: (0, j))],
        out_specs=pl.BlockSpec((rows, tn), lambda j: (0, j)),
        compiler_params=_params(("parallel",)),
        name="adaln_mods",
    )(cvec, w_ada, b_ada.reshape(1, n))


def _in_proj_kernel(x_ref, g_ref, mod_ref, w_ref, oa_ref, ob_ref, oc_ref, od_ref):
    h = _norm_mod(x_ref[0], g_ref[...], mod_ref[0, 0, 0:1, :], mod_ref[0, 0, 1:2, :])
    u = _dot(h.astype(BF16), w_ref[...])
    o1 = RWKV_IN
    o2 = o1 + GLA_IN_PAD
    o3 = o2 + ATTN_IN
    oa_ref[0] = u[:, :o1]
    ob_ref[0] = u[:, o1:o2]
    oc_ref[0] = u[:, o2:o3]
    od_ref[0] = u[:, o3:]


def in_proj(xs, g, mods, w_mix, n_ctx):
    b, s, d = xs.shape
    tm = TOKEN_TILE
    ctx_tiles = n_ctx // tm
    widths = (RWKV_IN, GLA_IN_PAD, ATTN_IN, S5_IN)
    return pl.pallas_call(
        _in_proj_kernel,
        out_shape=[jax.ShapeDtypeStruct((b, s, w), F32) for w in widths],
        grid=(b, s // tm),
        in_specs=[pl.BlockSpec((1, tm, d), lambda i, j: (i, j, 0)),
                  pl.BlockSpec((1, d), lambda i, j: (0, 0)),
                  pl.BlockSpec((1, 1, 6, d), lambda i, j: (i, (j >= ctx_tiles).astype(jnp.int32), 0, 0)),
                  pl.BlockSpec((d, MIX_IN_PAD), lambda i, j: (0, 0))],
        out_specs=[pl.BlockSpec((1, tm, w), lambda i, j: (i, j, 0)) for w in widths],
        compiler_params=_params(("parallel", "parallel")),
        name="in_proj",
    )(xs, g.reshape(1, d), mods, w_mix)


def _rope(z, cos, sin):
    lane = lax.broadcasted_iota(jnp.int32, z.shape, 1)
    swapped = jnp.where((lane % HEAD_DIM) < HEAD_DIM // 2,
                        pltpu.roll(z, 128 - HEAD_DIM // 2, axis=1),
                        pltpu.roll(z, HEAD_DIM // 2, axis=1))
    return z * cos + swapped * sin


def _attn_kernel(own_ref, prev_ref, next_ref, ctx_ref, cos_o, sin_o, cos_p, sin_p, cos_n, sin_n,
                 sink_ref, o_ref, *, ctx_blocks, n_lat):
    blk = ATTN_BLOCK
    qw = ATTN_HEADS * HEAD_DIM
    kw = ATTN_KV_HEADS * HEAD_DIM
    n = pl.program_id(1) - ctx_blocks
    own = own_ref[0]
    scale = HEAD_DIM ** -0.5
    q = jnp.concatenate([_rope(own[:, 0:128], cos_o[...], sin_o[...]),
                         _rope(own[:, 128:256], cos_o[...], sin_o[...])], axis=1) * scale
    k_band = jnp.concatenate([_rope(prev_ref[0][:, qw:qw + kw], cos_p[...], sin_p[...]),
                              _rope(own[:, qw:qw + kw], cos_o[...], sin_o[...]),
                              _rope(next_ref[0][:, qw:qw + kw], cos_n[...], sin_n[...])], axis=0)
    v_band = jnp.concatenate([prev_ref[0][:, qw + kw:], own[:, qw + kw:], next_ref[0][:, qw + kw:]], axis=0)
    k_ctx = ctx_ref[0][:, qw:qw + kw]
    v_ctx = ctx_ref[0][:, qw + kw:]
    qpos = n * blk + lax.broadcasted_iota(jnp.int32, (blk, 3 * blk), 0)
    kpos = (n - 1) * blk + lax.broadcasted_iota(jnp.int32, (blk, 3 * blk), 1)
    valid = (jnp.abs(qpos - kpos) <= WINDOW) & (kpos >= 0) & (kpos < n_lat) & (n >= 0)
    group = ATTN_HEADS // ATTN_KV_HEADS
    outs = []
    for h in range(ATTN_HEADS):
        kv = h // group
        qh = q[:, h * HEAD_DIM:(h + 1) * HEAD_DIM]
        kh = k_band[:, kv * HEAD_DIM:(kv + 1) * HEAD_DIM]
        vh = v_band[:, kv * HEAD_DIM:(kv + 1) * HEAD_DIM]
        kch = k_ctx[:, kv * HEAD_DIM:(kv + 1) * HEAD_DIM]
        vch = v_ctx[:, kv * HEAD_DIM:(kv + 1) * HEAD_DIM]
        s_w = jnp.where(valid, _dot_nt(qh.astype(BF16), kh.astype(BF16)), NEG_INF)
        s_c = _dot_nt(qh.astype(BF16), kch.astype(BF16))
        sink = sink_ref[0:1, h:h + 1]
        m = jnp.maximum(jnp.maximum(jnp.max(s_w, axis=-1, keepdims=True),
                                    jnp.max(s_c, axis=-1, keepdims=True)), sink)
        p_w = jnp.exp(s_w - m)
        p_c = jnp.exp(s_c - m)
        den = (jnp.sum(p_w, axis=-1, keepdims=True) + jnp.sum(p_c, axis=-1, keepdims=True)
               + jnp.exp(sink - m))
        acc = _dot(p_w.astype(BF16), vh.astype(BF16)) + _dot(p_c.astype(BF16), vch.astype(BF16))
        outs.append(acc / den)
    o_ref[0] = jnp.concatenate(outs, axis=1)


def window_attention(u_att, cos_t, sin_t, sink, n_ctx):
    b, s, w = u_att.shape
    blk = ATTN_BLOCK
    ctx_blocks = n_ctx // blk
    nblk = s // blk

    def own_map(i, j):
        return (i, j, 0)

    def prev_map(i, j):
        return (i, jnp.maximum(j - 1, 0), 0)

    def next_map(i, j):
        return (i, jnp.minimum(j + 1, nblk - 1), 0)

    def tab(m):
        return lambda i, j: m(i, j)[1:]

    kern = functools.partial(_attn_kernel, ctx_blocks=ctx_blocks, n_lat=s - n_ctx)
    return pl.pallas_call(
        kern,
        out_shape=jax.ShapeDtypeStruct((b, s, ATTN_HEADS * HEAD_DIM), F32),
        grid=(b, nblk),
        in_specs=[pl.BlockSpec((1, blk, w), own_map),
                  pl.BlockSpec((1, blk, w), prev_map),
                  pl.BlockSpec((1, blk, w), next_map),
                  pl.BlockSpec((1, n_ctx, w), lambda i, j: (i, 0, 0)),
                  pl.BlockSpec((blk, 128), tab(own_map)), pl.BlockSpec((blk, 128), tab(own_map)),
                  pl.BlockSpec((blk, 128), tab(prev_map)), pl.BlockSpec((blk, 128), tab(prev_map)),
                  pl.BlockSpec((blk, 128), tab(next_map)), pl.BlockSpec((blk, 128), tab(next_map)),
                  pl.BlockSpec((1, 128), lambda i, j: (0, 0))],
        out_specs=pl.BlockSpec((1, blk, ATTN_HEADS * HEAD_DIM), own_map),
        compiler_params=_params(("parallel", "parallel")),
        name="window_attention",
    )(u_att, u_att, u_att, u_att, cos_t, sin_t, cos_t, sin_t, cos_t, sin_t, sink)


def rope_tables(n_ctx, n_lat):
    rows = n_lat // GRID_W
    row = jnp.repeat(jnp.arange(rows, dtype=F32), GRID_W)
    col = jnp.tile(jnp.arange(GRID_W, dtype=F32), rows)
    n_freq = HEAD_DIM // 4
    inv_freq = ROPE_BASE ** (-jnp.arange(n_freq, dtype=F32) / n_freq)
    ang = jnp.concatenate([row[:, None] * inv_freq, col[:, None] * inv_freq], axis=-1)
    cos, sin = jnp.cos(ang), jnp.sin(ang)
    cos_l = jnp.concatenate([cos, cos, cos, cos], axis=-1)
    sin_l = jnp.concatenate([-sin, sin, -sin, sin], axis=-1)
    cos_t = jnp.concatenate([jnp.ones((n_ctx, 128), F32), cos_l], axis=0)
    sin_t = jnp.concatenate([jnp.zeros((n_ctx, 128), F32), sin_l], axis=0)
    return cos_t, sin_t


def _scan_order(j, ctx_steps, n_steps, reverse):
    if not reverse:
        return j
    return jnp.where(j < ctx_steps, ctx_steps - 1 - j, n_steps - 1 - (j - ctx_steps))


def _s5_kernel(u_ref, bblk_ref, cblk_ref, apr_ref, api_ref, pr_ref, pi_ref, y_ref, car_ref, cai_ref,
               *, reverse, n_doubling):
    tm = u_ref.shape[1]
    ns = pr_ref.shape[1]

    @pl.when(pl.program_id(1) == 0)
    def _():
        car_ref[...] = jnp.zeros_like(car_ref)
        cai_ref[...] = jnp.zeros_like(cai_ref)

    bu = _dot(u_ref[0].astype(BF16), bblk_ref[...])
    xr, xi = bu[:, :ns], bu[:, ns:]
    row = lax.broadcasted_iota(jnp.int32, (tm, ns), 0)
    for k in range(n_doubling):
        sh = 1 << k
        if reverse:
            keep = row < tm - sh
            sxr = jnp.where(keep, pltpu.roll(xr, tm - sh, axis=0), 0.0)
            sxi = jnp.where(keep, pltpu.roll(xi, tm - sh, axis=0), 0.0)
        else:
            keep = row >= sh
            sxr = jnp.where(keep, pltpu.roll(xr, sh, axis=0), 0.0)
            sxi = jnp.where(keep, pltpu.roll(xi, sh, axis=0), 0.0)
        ar, ai = apr_ref[k:k + 1, :], api_ref[k:k + 1, :]
        xr, xi = xr + ar * sxr - ai * sxi, xi + ar * sxi + ai * sxr
    cr, ci = car_ref[0:1, :], cai_ref[0:1, :]
    pr, pi = pr_ref[...], pi_ref[...]
    xr, xi = xr + pr * cr - pi * ci, xi + pr * ci + pi * cr
    last = 0 if reverse else tm - 1
    car_ref[0:1, :] = xr[last:last + 1, :]
    cai_ref[0:1, :] = xi[last:last + 1, :]
    y_ref[0] = _dot(jnp.concatenate([xr, xi], axis=1).astype(BF16), cblk_ref[...])


def s5_scan(u, prm, n_ctx, reverse):
    b, s, w = u.shape
    tm = TOKEN_TILE
    ns = S5_GROUPS * S5_STATE
    ctx_tiles, n_tiles = n_ctx // tm, s // tm
    n_doubling = int(math.log2(tm))
    bblk, cblk, apr, api, pr, pi = prm
    kern = functools.partial(_s5_kernel, reverse=reverse, n_doubling=n_doubling)

    def tile_map(i, j):
        return (i, _scan_order(j, ctx_tiles, n_tiles, reverse), 0)

    def const(i, j):
        return (0, 0)

    return pl.pallas_call(
        kern,
        out_shape=jax.ShapeDtypeStruct((b, s, w), F32),
        grid=(b, n_tiles),
        in_specs=[pl.BlockSpec((1, tm, w), tile_map),
                  pl.BlockSpec((w, 2 * ns), const), pl.BlockSpec((2 * ns, w), const),
                  pl.BlockSpec(apr.shape, const), pl.BlockSpec(api.shape, const),
                  pl.BlockSpec((tm, ns), const), pl.BlockSpec((tm, ns), const)],
        out_specs=pl.BlockSpec((1, tm, w), tile_map),
        scratch_shapes=[pltpu.VMEM((8, ns), F32), pltpu.VMEM((8, ns), F32)],
        compiler_params=_params(("parallel", "arbitrary")),
        name="s5_scan_rev" if reverse else "s5_scan_fwd",
    )(u, bblk, cblk, apr, api, pr, pi)


def s5_direction_params(p, d, reverse):
    g, ns, ch = S5_GROUPS, S5_STATE, S5_GROUP_CH
    tm = TOKEN_TILE
    lr, li = p['s5_lam_re'][d], p['s5_lam_im'][d]
    dt = jnp.exp(p['s5_log_dt'][d])[:, None]

    def power(n):
        mag = jnp.exp(n * lr * dt)
        return mag * jnp.cos(n * li * dt), mag * jnp.sin(n * li * dt)

    ar, ai = power(1.0)
    den = lr * lr + li * li
    cr = ((ar - 1.0) * lr + ai * li) / den
    ci = (ai * lr - (ar - 1.0) * li) / den
    bre, bim = p['s5_b_re'], p['s5_b_im']
    bbr = cr[..., None] * bre - ci[..., None] * bim
    bbi = cr[..., None] * bim + ci[..., None] * bre
    eye = jnp.eye(g, dtype=F32)

    def expand_in(z):
        return jnp.einsum('gpc,gh->gchp', z, eye).reshape(g * ch, g * ns)

    def expand_out(z):
        return jnp.einsum('gcp,gh->gphc', z, eye).reshape(g * ns, g * ch)

    bblk = jnp.concatenate([expand_in(bbr), expand_in(bbi)], axis=1).astype(BF16)
    cblk = jnp.concatenate([expand_out(p['s5_c_re'][d]), -expand_out(p['s5_c_im'][d])], axis=0).astype(BF16)
    n_doubling = int(math.log2(tm))
    pw = [power(float(1 << k)) for k in range(n_doubling)]
    apr = jnp.stack([z[0].reshape(-1) for z in pw])
    api = jnp.stack([z[1].reshape(-1) for z in pw])
    steps = jnp.arange(1, tm + 1, dtype=F32)[:, None, None]
    mag = jnp.exp(steps * (lr * dt)[None])
    pr = (mag * jnp.cos(steps * (li * dt)[None])).reshape(tm, g * ns)
    pi = (mag * jnp.sin(steps * (li * dt)[None])).reshape(tm, g * ns)
    if reverse:
        pr, pi = pr[::-1], pi[::-1]
    return bblk, cblk, apr, api, pr, pi


def _log_sigmoid(z):
    return jnp.minimum(z, 0.0) - jnp.log(1.0 + jnp.exp(-jnp.abs(z)))


def _gla_kernel(u_ref, a2_ref, ab_ref, o_ref, st_ref, *, reverse):
    tm = u_ref.shape[1]
    qk = GLA_HEADS * GLA_DK
    vw = GLA_HEADS * GLA_DV

    @pl.when(pl.program_id(1) == 0)
    def _():
        st_ref[...] = jnp.zeros_like(st_ref)

    u = u_ref[0]
    alow = u[:, 2 * qk + 2 * vw:]
    log_a = _log_sigmoid(_dot(alow, a2_ref[...], HI) + ab_ref[...]) * (1.0 / GLA_TAU)
    incl = _order_mask(CHUNK, reverse, True)
    incl_f = incl.astype(F32)
    n_chunks = tm // CHUNK
    order = range(n_chunks - 1, -1, -1) if reverse else range(n_chunks)
    last = 0 if reverse else CHUNK - 1
    for c in order:
        r0 = c * CHUNK
        q = u[r0:r0 + CHUNK, 0:qk] * (GLA_DK ** -0.5)
        k = u[r0:r0 + CHUNK, qk:2 * qk]
        v = u[r0:r0 + CHUNK, 2 * qk:2 * qk + vw]
        cum = _dot(incl_f, log_a[r0:r0 + CHUNK], HI)
        total = cum[last:last + 1, :]
        q_dec = q * jnp.exp(cum)
        k_inv = k * jnp.exp(-cum)
        k_end = k * jnp.exp(total - cum)
        decay = jnp.exp(total)
        outs = []
        for h in range(GLA_HEADS):
            ks = slice(h * GLA_DK, (h + 1) * GLA_DK)
            vs = slice(h * GLA_DV, (h + 1) * GLA_DV)
            st = st_ref[h]
            scores = jnp.where(incl, _dot_nt(q_dec[:, ks], k_inv[:, ks], HI), 0.0)
            outs.append(_dot(scores, v[:, vs], HI) + _dot_nt(q_dec[:, ks], st, HI))
            st_ref[h] = st * decay[:, ks] + _dot_tn(v[:, vs], k_end[:, ks], HI)
        o_ref[0, r0:r0 + CHUNK, :] = jnp.concatenate(outs, axis=1)


def gla_scan(u, a2pad, ab, n_ctx, reverse):
    b, s, w = u.shape
    tm = TOKEN_TILE
    ctx_tiles, n_tiles = n_ctx // tm, s // tm
    qk = GLA_HEADS * GLA_DK

    def tile_map(i, j):
        return (i, _scan_order(j, ctx_tiles, n_tiles, reverse), 0)

    return pl.pallas_call(
        functools.partial(_gla_kernel, reverse=reverse),
        out_shape=jax.ShapeDtypeStruct((b, s, GLA_HEADS * GLA_DV), F32),
        grid=(b, n_tiles),
        in_specs=[pl.BlockSpec((1, tm, w), tile_map),
                  pl.BlockSpec((128, qk), lambda i, j: (0, 0)),
                  pl.BlockSpec((1, qk), lambda i, j: (0, 0))],
        out_specs=pl.BlockSpec((1, tm, GLA_HEADS * GLA_DV), tile_map),
        scratch_shapes=[pltpu.VMEM((GLA_HEADS, GLA_DV, GLA_DK), F32)],
        compiler_params=_params(("parallel", "arbitrary")),
        name="gla_scan_rev" if reverse else "gla_scan_fwd",
    )(u, a2pad, ab)


def gla_direction_params(p, d):
    a2pad = jnp.zeros((128, GLA_HEADS * GLA_DK), F32).at[d * GLA_RANK:(d + 1) * GLA_RANK].set(p['gla_a2'][d])
    return a2pad, p['gla_ab'][d].reshape(1, -1)


def _pad_gla(u):
    return jnp.pad(u, ((0, 0), (0, 0), (0, GLA_IN_PAD - GLA_IN)))


def gla_mixer_test(u, p, n_ctx):
    up = _pad_gla(u)
    o = gla_scan(up, *gla_direction_params(p, 0), n_ctx, False) + gla_scan(up, *gla_direction_params(p, 1), n_ctx, True)
    b, s, _ = o.shape
    oh = o.reshape(b, s, GLA_HEADS, GLA_DV)
    oh = oh * lax.rsqrt(jnp.mean(oh * oh, axis=-1, keepdims=True) + EPS)
    r = u[..., 2 * 128 + 256:2 * 128 + 512]
    return oh.reshape(b, s, -1) * p['gla_ln_g'] * jax.nn.silu(r)


def _rwkv_kernel(u_ref, up_ref, un_ref, mu_ref, kkp_ref, ka_ref, rk_ref, w0_ref, w1_ref, w2_ref,
                 a0_ref, a1_ref, a2_ref, g1_ref, g2_ref, *rest, reverse, ctx_tiles, n_tiles, emit_aux):
    if emit_aux:
        y_ref, bonus_ref, gate_ref, st_ref = rest
    else:
        y_ref, st_ref = rest
    tm = u_ref.shape[1]
    w = MIX_W
    j = pl.program_id(1)
    pos = _scan_order(j, ctx_tiles, n_tiles, reverse)

    @pl.when(j == 0)
    def _():
        st_ref[...] = jnp.zeros_like(st_ref)

    first = (pos == 0) | (pos == ctx_tiles)
    last_tile = (pos == ctx_tiles - 1) | (pos == n_tiles - 1)
    prev_row = jnp.where(first, 0.0, up_ref[0, 7:8, :])
    next_row = jnp.where(last_tile, 0.0, un_ref[0, 0:1, :])
    u = u_ref[0]
    row = lax.broadcasted_iota(jnp.int32, u.shape, 0)
    down = jnp.where(row == 0, prev_row, pltpu.roll(u, 1, axis=0))
    up = jnp.where(row == tm - 1, next_row, pltpu.roll(u, tm - 1, axis=0))
    mixed = u + (0.5 * (down + up) - u) * mu_ref[...]
    r, k, v, xa = mixed[:, 0:w], mixed[:, w:2 * w], mixed[:, 2 * w:3 * w], mixed[:, 3 * w:4 * w]
    seg = _seg_ones(w, HEAD_DIM)
    kk = k * kkp_ref[...]
    kk = kk * lax.rsqrt(_dot(kk * kk, seg, HI) + EPS)
    zw = w0_ref[...] + _dot(jnp.tanh(_dot(xa, w1_ref[...], HI)), w2_ref[...], HI)
    log_w = -math.exp(-0.5) * _sigmoid(zw)
    iclr = _sigmoid(a0_ref[...] + _dot(_dot(xa, a1_ref[...], HI), a2_ref[...], HI))
    k_eff = k * (1.0 + (iclr - 1.0) * ka_ref[...])
    a_vec = -kk
    b_vec = kk * iclr
    if emit_aux:
        gate_ref[0] = _dot(_sigmoid(_dot(xa, g1_ref[...], HI)), g2_ref[...], HI)
        bonus_ref[0] = _dot(r * k * rk_ref[...], seg, HI) * v

    incl = _order_mask(CHUNK, reverse, True)
    strict = _order_mask(CHUNK, reverse, False)
    incl_f = incl.astype(F32)
    eye = (lax.broadcasted_iota(jnp.int32, (CHUNK, CHUNK), 0)
           == lax.broadcasted_iota(jnp.int32, (CHUNK, CHUNK), 1)).astype(F32)
    n_chunks = tm // CHUNK
    order = range(n_chunks - 1, -1, -1) if reverse else range(n_chunks)
    last = 0 if reverse else CHUNK - 1
    n_doubling = int(math.log2(CHUNK)) - 1
    for c in order:
        rows = slice(c * CHUNK, (c + 1) * CHUNK)
        lwc = log_w[rows]
        cum = _dot(incl_f, lwc, HI)
        total = cum[last:last + 1, :]
        e_pos = jnp.exp(cum)
        e_neg = jnp.exp(-cum)
        e_end = jnp.exp(total - cum)
        a_t = a_vec[rows] * jnp.exp(cum - lwc)
        r_t = r[rows] * e_pos
        b_t = b_vec[rows] * e_neg
        k_t = k_eff[rows] * e_neg
        b_end = b_vec[rows] * e_end
        k_end = k_eff[rows] * e_end
        decay = jnp.exp(total)
        vc = v[rows]
        outs = []
        for h in range(RWKV_HEADS):
            hs = slice(h * HEAD_DIM, (h + 1) * HEAD_DIM)
            st = st_ref[h]
            lhs = jnp.concatenate([a_t[:, hs], r_t[:, hs]], axis=0)
            rhs = jnp.concatenate([b_t[:, hs], k_t[:, hs]], axis=0)
            m = _dot_nt(lhs, rhs, HI)
            a_ab = jnp.where(strict, m[:CHUNK, :CHUNK], 0.0)
            a_ak = jnp.where(strict, m[:CHUNK, CHUNK:], 0.0)
            a_rb = jnp.where(incl, m[CHUNK:, :CHUNK], 0.0)
            a_rk = jnp.where(incl, m[CHUNK:, CHUNK:], 0.0)
            x0 = _dot_nt(lhs, st, HI)
            inv = eye + a_ab
            pw = a_ab
            for _ in range(n_doubling):
                pw = _dot(pw, pw, HI)
                inv = inv + _dot(inv, pw, HI)
            vh = vc[:, hs]
            uu = _dot(inv, x0[:CHUNK] + _dot(a_ak, vh, HI), HI)
            outs.append(x0[CHUNK:] + _dot(a_rb, uu, HI) + _dot(a_rk, vh, HI))
            st_ref[h] = st * decay[:, hs] + _dot_tn(uu, b_end[:, hs], HI) + _dot_tn(vh, k_end[:, hs], HI)
        y_ref[0, rows, :] = jnp.concatenate(outs, axis=1)


def rwkv_scan(u, prm, n_ctx, reverse, emit_aux):
    b, s, w4 = u.shape
    tm = TOKEN_TILE
    w = MIX_W
    ctx_tiles, n_tiles = n_ctx // tm, s // tm
    halo_blocks = tm // 8

    def tile_map(i, j):
        return (i, _scan_order(j, ctx_tiles, n_tiles, reverse), 0)

    def up_map(i, j):
        return (i, jnp.maximum(_scan_order(j, ctx_tiles, n_tiles, reverse) * halo_blocks - 1, 0), 0)

    def un_map(i, j):
        return (i, jnp.minimum((_scan_order(j, ctx_tiles, n_tiles, reverse) + 1) * halo_blocks, s // 8 - 1), 0)

    def full(a):
        return pl.BlockSpec(a.shape, lambda i, j: (0,) * a.ndim)

    n_out = 3 if emit_aux else 1
    kern = functools.partial(_rwkv_kernel, reverse=reverse, ctx_tiles=ctx_tiles, n_tiles=n_tiles,
                             emit_aux=emit_aux)
    outs = pl.pallas_call(
        kern,
        out_shape=[jax.ShapeDtypeStruct((b, s, w), F32)] * n_out,
        grid=(b, n_tiles),
        in_specs=[pl.BlockSpec((1, tm, w4), tile_map),
                  pl.BlockSpec((1, 8, w4), up_map),
                  pl.BlockSpec((1, 8, w4), un_map)] + [full(a) for a in prm],
        out_specs=[pl.BlockSpec((1, tm, w), tile_map)] * n_out,
        scratch_shapes=[pltpu.VMEM((RWKV_HEADS, HEAD_DIM, HEAD_DIM), F32)],
        compiler_params=_params(("parallel", "arbitrary")),
        name="rwkv_scan_rev" if reverse else "rwkv_scan_fwd",
    )(u, u, u, *prm)
    return outs


def rwkv_direction_params(p, d):
    row = lambda z: z.reshape(1, -1)
    return (row(p['rwkv_mu']), row(p['rwkv_kk']), row(p['rwkv_ka']), row(p['rwkv_rk']),
            row(p['rwkv_w0'][d]), p['rwkv_w1'][d], p['rwkv_w2'][d],
            row(p['rwkv_a0'][d]), p['rwkv_a1'][d], p['rwkv_a2'][d], p['rwkv_g1'], p['rwkv_g2'])


def rwkv_mixer_test(u, p, n_ctx):
    yf, bonus, gate = rwkv_scan(u, rwkv_direction_params(p, 0), n_ctx, False, True)
    yr, = rwkv_scan(u, rwkv_direction_params(p, 1), n_ctx, True, False)
    b, s, _ = yf.shape
    y = (yf + yr).reshape(b, s, RWKV_HEADS, HEAD_DIM)
    mu = jnp.mean(y, axis=-1, keepdims=True)
    var = jnp.mean(jnp.square(y - mu), axis=-1, keepdims=True)
    y = ((y - mu) * lax.rsqrt(var + GN_EPS)).reshape(b, s, -1) * p['rwkv_ln_g']
    return (y + bonus) * gate


def _gelu_tanh(y):
    return 0.5 * y * (1.0 + jnp.tanh(math.sqrt(2.0 / math.pi) * (y + 0.044715 * (y * y * y))))


def _merge_kernel(x_ref, g_ref, mod_ref, wgate_ref, wbr_ref, wout_ref,
                  yaf_ref, yar_ref, bonus_ref, gate_ref, lna_ref,
                  gof_ref, gor_ref, ub_ref, lnb_ref,
                  yc_ref, us5_ref, ysf_ref, ysr_ref, s5d_ref, wglu_ref, bglu_ref, o_ref):
    d = x_ref.shape[2]
    x = x_ref[0]
    h = _norm_mod(x, g_ref[...], mod_ref[0, 0, 0:1, :], mod_ref[0, 0, 1:2, :]).astype(BF16)
    seg = _seg_ones(MIX_W, HEAD_DIM) * (1.0 / HEAD_DIM)

    y = yaf_ref[0] + yar_ref[0]
    mu = _dot(y, seg, HI)
    yc0 = y - mu
    var = _dot(yc0 * yc0, seg, HI)
    ya = (yc0 * lax.rsqrt(var + GN_EPS) * lna_ref[...] + bonus_ref[0]) * gate_ref[0]

    o = gof_ref[0] + gor_ref[0]
    qk = GLA_HEADS * GLA_DK
    r = ub_ref[0][:, 2 * qk + MIX_W:2 * qk + 2 * MIX_W]
    yb = o * lax.rsqrt(_dot(o * o, seg, HI) + EPS) * lnb_ref[...] * (r * _sigmoid(r))

    yd = _gelu_tanh(s5d_ref[...] * us5_ref[0] + ysf_ref[0] + ysr_ref[0])
    yd = yd * _sigmoid(_dot(yd.astype(BF16), wglu_ref[...]) + bglu_ref[...])

    merged = None
    for i, yi in enumerate((ya, yb, yc_ref[0], yd)):
        gate_i = _sigmoid(_dot(h, wgate_ref[:, i * d:(i + 1) * d]))
        term = gate_i * _dot(yi.astype(BF16), wbr_ref[i])
        merged = term if merged is None else merged + term
    o_ref[0] = x + mod_ref[0, 0, 2:3, :] * _dot(merged.astype(BF16), wout_ref[...])


def merge_branches(xs, g, mods, w_gate, w_branch, w_out, branch_inputs, n_ctx):
    b, s, d = xs.shape
    tm = TOKEN_TILE
    ctx_tiles = n_ctx // tm
    (yaf, yar, bonus, gate, lna, gof, gor, ub, lnb, yc, us5, ysf, ysr, s5d, wglu, bglu) = branch_inputs

    def tok(a):
        return pl.BlockSpec((1, tm, a.shape[2]), lambda i, j: (i, j, 0))

    def full(a):
        return pl.BlockSpec(a.shape, lambda i, j: (0,) * a.ndim)

    mod_spec = pl.BlockSpec((1, 1, 6, d), lambda i, j: (i, (j >= ctx_tiles).astype(jnp.int32), 0, 0))
    return pl.pallas_call(
        _merge_kernel,
        out_shape=jax.ShapeDtypeStruct((b, s, d), F32),
        grid=(b, s // tm),
        in_specs=[tok(xs), full(g), mod_spec, full(w_gate), full(w_branch), full(w_out),
                  tok(yaf), tok(yar), tok(bonus), tok(gate), full(lna),
                  tok(gof), tok(gor), tok(ub), full(lnb),
                  tok(yc), tok(us5), tok(ysf), tok(ysr), full(s5d), full(wglu), full(bglu)],
        out_specs=tok(xs),
        compiler_params=_params(("parallel", "parallel")),
        name="merge_branches",
    )(xs, g, mods, w_gate, w_branch, w_out, yaf, yar, bonus, gate, lna, gof, gor, ub, lnb,
      yc, us5, ysf, ysr, s5d, wglu, bglu)


def _router_kernel(x_ref, g_ref, mod_ref, wr_ref, br_ref, h_ref, idx_ref, wt_ref):
    h = _norm_mod(x_ref[0], g_ref[...], mod_ref[0, 0, 3:4, :], mod_ref[0, 0, 4:5, :])
    h_ref[...] = h
    logits = _dot(h, wr_ref[...], HI) + br_ref[...]
    lane = lax.broadcasted_iota(jnp.int32, logits.shape, 1)
    big = jnp.int32(1 << 20)

    def first_max(z):
        m = jnp.max(z, axis=-1, keepdims=True)
        return m, jnp.min(jnp.where(z == m, lane, big), axis=-1, keepdims=True)

    gl = jnp.where(lane < N_GROUPS, logits, -jnp.inf)
    gmax, gsel = first_max(gl)
    g_w = 1.0 / jnp.sum(jnp.exp(gl - gmax), axis=-1, keepdims=True)
    lo = N_GROUPS + EXPERTS_PER_GROUP * gsel
    el = jnp.where((lane >= lo) & (lane < lo + EXPERTS_PER_GROUP), logits, -jnp.inf)
    v1, i1 = first_max(el)
    v2, i2 = first_max(jnp.where(lane == i1, -jnp.inf, el))
    e2 = jnp.exp(v2 - v1)
    w1 = g_w / (1.0 + e2)
    w2 = g_w * e2 / (1.0 + e2)
    idx_ref[...] = jnp.where(lane == 0, i1 - N_GROUPS, jnp.where(lane == 1, i2 - N_GROUPS, 0))
    wt_ref[...] = jnp.where(lane == 0, w1, jnp.where(lane == 1, w2, 0.0))


def moe_router(xs, g, mods, w_router, b_router, n_ctx):
    b, s, d = xs.shape
    tm = TOKEN_TILE
    ctx_tiles, n_tiles = n_ctx // tm, s // tm
    n = b * s
    mod_spec = pl.BlockSpec((1, 1, 6, d), lambda i, j: (i, (j >= ctx_tiles).astype(jnp.int32), 0, 0))
    flat = lambda i, j: (i * n_tiles + j, 0)
    return pl.pallas_call(
        _router_kernel,
        out_shape=[jax.ShapeDtypeStruct((n, d), F32), jax.ShapeDtypeStruct((n, 128), jnp.int32),
                   jax.ShapeDtypeStruct((n, 128), F32)],
        grid=(b, n_tiles),
        in_specs=[pl.BlockSpec((1, tm, d), lambda i, j: (i, j, 0)),
                  pl.BlockSpec((1, d), lambda i, j: (0, 0)), mod_spec,
                  pl.BlockSpec((d, 128), lambda i, j: (0, 0)),
                  pl.BlockSpec((1, 128), lambda i, j: (0, 0))],
        out_specs=[pl.BlockSpec((tm, d), flat), pl.BlockSpec((tm, 128), flat), pl.BlockSpec((tm, 128), flat)],
        compiler_params=_params(("parallel", "parallel")),
        name="moe_router",
    )(xs, g, mods, w_router, b_router)


def _row_copy(src_hbm, src_row, dst, dst_row, sem):
    return pltpu.make_async_copy(src_hbm.at[pl.ds(src_row, 1)], dst.at[pl.ds(dst_row, 1)], sem)


def _expert_kernel(blk_e_ref, tok_ref, h_hbm, wg_ref, wu_ref, wd_ref, y_ref, xbuf, sem):
    del blk_e_ref

    def issue(r, carry):
        _row_copy(h_hbm, tok_ref[0, 0, r], xbuf, r, sem).start()
        return carry

    def drain(r, carry):
        _row_copy(h_hbm, tok_ref[0, 0, r], xbuf, r, sem).wait()
        return carry

    lax.fori_loop(0, MOE_BLOCK, issue, 0)
    lax.fori_loop(0, MOE_BLOCK, drain, 0)
    xb = xbuf[...].astype(BF16)
    gate = _dot(xb, wg_ref[0])
    hid = gate * _sigmoid(gate) * _dot(xb, wu_ref[0])
    y_ref[...] = _dot(hid.astype(BF16), wd_ref[0])


def moe_experts(h2, blk_e, buf_tok, w_gate, w_up, w_down):
    n, d = h2.shape
    n_blk = blk_e.shape[0]
    hid = w_gate.shape[2]
    grid_spec = pltpu.PrefetchScalarGridSpec(
        num_scalar_prefetch=1,
        grid=(n_blk,),
        in_specs=[pl.BlockSpec((1, 1, MOE_BLOCK), lambda i, be: (i, 0, 0), memory_space=pltpu.SMEM),
                  pl.BlockSpec(memory_space=pl.ANY),
                  pl.BlockSpec((1, d, hid), lambda i, be: (be[i], 0, 0)),
                  pl.BlockSpec((1, d, hid), lambda i, be: (be[i], 0, 0)),
                  pl.BlockSpec((1, hid, d), lambda i, be: (be[i], 0, 0))],
        out_specs=pl.BlockSpec((MOE_BLOCK, d), lambda i, be: (i, 0)),
        scratch_shapes=[pltpu.VMEM((MOE_BLOCK, d), F32), pltpu.SemaphoreType.DMA],
    )
    return pl.pallas_call(
        _expert_kernel,
        out_shape=jax.ShapeDtypeStruct((n_blk * MOE_BLOCK, d), F32),
        grid_spec=grid_spec,
        compiler_params=_params(("arbitrary",)),
        name="moe_experts",
    )(blk_e, buf_tok.reshape(n_blk, 1, MOE_BLOCK), h2, w_gate, w_up, w_down)


def _combine_kernel(pos_ref, x_ref, mod_ref, wt_ref, fg_ref, y_hbm, o_ref, rows, sem, *, final_norm):
    tm = x_ref.shape[1]

    def issue(r, carry):
        _row_copy(y_hbm, pos_ref[0, 0, 2 * r], rows.at[0], r, sem).start()
        _row_copy(y_hbm, pos_ref[0, 0, 2 * r + 1], rows.at[1], r, sem).start()
        return carry

    def drain(r, carry):
        _row_copy(y_hbm, pos_ref[0, 0, 2 * r], rows.at[0], r, sem).wait()
        _row_copy(y_hbm, pos_ref[0, 0, 2 * r + 1], rows.at[1], r, sem).wait()
        return carry

    lax.fori_loop(0, tm, issue, 0)
    lax.fori_loop(0, tm, drain, 0)
    wt = wt_ref[...]
    y = wt[:, 0:1] * rows[0] + wt[:, 1:2] * rows[1]
    out = x_ref[0] + mod_ref[0, 0, 5:6, :] * y
    if final_norm:
        out = out * lax.rsqrt(jnp.mean(out * out, axis=-1, keepdims=True) + EPS) * fg_ref[...]
    o_ref[0] = out


def moe_combine(xs, mods, wts, pos, y_buf, final_g, n_ctx, final_norm):
    b, s, d = xs.shape
    tm = MOE_BLOCK
    ctx_tiles, n_tiles = n_ctx // tm, s // tm
    mod_spec = pl.BlockSpec((1, 1, 6, d), lambda i, j: (i, (j >= ctx_tiles).astype(jnp.int32), 0, 0))
    return pl.pallas_call(
        functools.partial(_combine_kernel, final_norm=final_norm),
        out_shape=jax.ShapeDtypeStruct((b, s, d), F32),
        grid=(b, n_tiles),
        in_specs=[pl.BlockSpec((1, 1, 2 * tm), lambda i, j: (i * n_tiles + j, 0, 0), memory_space=pltpu.SMEM),
                  pl.BlockSpec((1, tm, d), lambda i, j: (i, j, 0)), mod_spec,
                  pl.BlockSpec((tm, 128), lambda i, j: (i * n_tiles + j, 0)),
                  pl.BlockSpec((1, d), lambda i, j: (0, 0)),
                  pl.BlockSpec(memory_space=pl.ANY)],
        out_specs=pl.BlockSpec((1, tm, d), lambda i, j: (i, j, 0)),
        scratch_shapes=[pltpu.VMEM((2, tm, d), F32), pltpu.SemaphoreType.DMA],
        compiler_params=_params(("parallel", "arbitrary")),
        name="moe_combine",
    )(pos.reshape(b * n_tiles, 1, 2 * tm), xs, mods, wts, final_g, y_buf)


def moe_dispatch_plan(expert_ids):
    n = expert_ids.shape[0]
    flat_e = expert_ids.reshape(-1)
    n_assign = n * TOP_K
    order = jnp.argsort(flat_e).astype(jnp.int32)
    se = flat_e[order]
    stok = order // TOP_K
    experts = jnp.arange(N_EXPERTS, dtype=jnp.int32)
    start = jnp.searchsorted(se, experts, side='left').astype(jnp.int32)
    end = jnp.searchsorted(se, experts, side='right').astype(jnp.int32)
    counts = end - start
    padded = (counts + MOE_BLOCK - 1) // MOE_BLOCK * MOE_BLOCK
    pend = jnp.cumsum(padded)
    pstart = pend - padded
    dest = pstart[se] + (jnp.arange(n_assign, dtype=jnp.int32) - start[se])
    cap = -(-n_assign // MOE_BLOCK) * MOE_BLOCK + N_EXPERTS * MOE_BLOCK
    n_blk = cap // MOE_BLOCK
    buf_tok = jnp.zeros((cap,), jnp.int32).at[dest].set(stok)
    blk_e = jnp.minimum(jnp.searchsorted(pend, jnp.arange(n_blk, dtype=jnp.int32) * MOE_BLOCK, side='right'),
                        N_EXPERTS - 1).astype(jnp.int32)
    pos = jnp.zeros((n_assign,), jnp.int32).at[order].set(dest)
    return blk_e, buf_tok, pos


def hier_moe_layer(xs, g2, mods, p, final_g, n_ctx, final_norm):
    d = xs.shape[2]
    w_router = jnp.zeros((d, 128), F32).at[:, :N_GROUPS].set(p['w_router_g']) \
        .at[:, N_GROUPS:N_GROUPS + N_EXPERTS].set(p['w_router_e'])
    b_router = jnp.zeros((1, 128), F32).at[0, :N_GROUPS].set(p['b_router_g']) \
        .at[0, N_GROUPS:N_GROUPS + N_EXPERTS].set(p['b_router_e'])
    h2, idx, wts = moe_router(xs, g2, mods, w_router, b_router, n_ctx)
    blk_e, buf_tok, pos = moe_dispatch_plan(idx[:, :TOP_K])
    y_buf = moe_experts(h2, blk_e, buf_tok, p['w_exp_gate'].astype(BF16), p['w_exp_up'].astype(BF16),
                        p['w_exp_down'].astype(BF16))
    return moe_combine(xs, mods, wts, pos, y_buf, final_g, n_ctx, final_norm)


def moe_test(h, p):
    n, d = h.shape
    ones = jnp.ones((1, d), F32)
    mods = jnp.zeros((1, 2, 6, d), F32).at[:, :, 5].set(1.0)
    out = hier_moe_layer(h.reshape(1, n, d), ones, mods, p, ones, 0, False)
    return out.reshape(n, d) - h


def _layer(xs, cvec, p, cos_t, sin_t, final_g, n_ctx, n_batch, final_norm):
    b, s, d = xs.shape
    m = adaln_mods(cvec, p['w_ada'], p['b_ada']).reshape(cvec.shape[0], 6, d)
    mods = jnp.stack([jnp.broadcast_to(m[n_batch], (n_batch, 6, d)), m[:n_batch]], axis=1)
    w_in = p['w_in']
    o1 = RWKV_IN
    o2 = o1 + GLA_IN
    o3 = o2 + ATTN_IN
    o4 = o3 + S5_IN
    w_mix = jnp.concatenate([w_in[:, :o2], jnp.zeros((d, GLA_IN_PAD - GLA_IN), F32), w_in[:, o2:o4]],
                            axis=1).astype(BF16)
    w_gate = w_in[:, o4:].astype(BF16)
    g1 = p['norm1_g'].reshape(1, d)
    ua, ub, uatt, us5 = in_proj(xs, g1, mods, w_mix, n_ctx)

    yaf, bonus, gate = rwkv_scan(ua, rwkv_direction_params(p, 0), n_ctx, False, True)
    yar, = rwkv_scan(ua, rwkv_direction_params(p, 1), n_ctx, True, False)
    gof = gla_scan(ub, *gla_direction_params(p, 0), n_ctx, False)
    gor = gla_scan(ub, *gla_direction_params(p, 1), n_ctx, True)
    sink = jnp.zeros((1, 128), F32).at[0, :ATTN_HEADS].set(p['attn_sink'])
    yc = window_attention(uatt, cos_t, sin_t, sink, n_ctx)
    ysf = s5_scan(us5, s5_direction_params(p, 0, False), n_ctx, False)
    ysr = s5_scan(us5, s5_direction_params(p, 1, True), n_ctx, True)

    row = lambda z: z.reshape(1, -1)
    branch_inputs = (yaf, yar, bonus, gate, row(p['rwkv_ln_g']), gof, gor, ub, row(p['gla_ln_g']),
                     yc, us5, ysf, ysr, row(p['s5_d']), p['s5_w_glu'].astype(BF16), row(p['s5_b_glu']))
    xs = merge_branches(xs, g1, mods, w_gate, p['w_branch'].astype(BF16), p['w_out'].astype(BF16),
                        branch_inputs, n_ctx)
    return hier_moe_layer(xs, p['norm2_g'].reshape(1, d), mods, p, final_g, n_ctx, final_norm)


_LAYER_PARAMS = ('norm1_g', 'norm2_g', 'w_ada', 'b_ada', 'w_in', 'rwkv_mu', 'rwkv_w0', 'rwkv_w1', 'rwkv_w2',
                 'rwkv_a0', 'rwkv_a1', 'rwkv_a2', 'rwkv_kk', 'rwkv_ka', 'rwkv_rk', 'rwkv_g1', 'rwkv_g2',
                 'rwkv_ln_g', 'gla_a2', 'gla_ab', 'gla_ln_g', 'attn_sink', 's5_lam_re', 's5_lam_im', 's5_log_dt',
                 's5_b_re', 's5_b_im', 's5_c_re', 's5_c_im', 's5_d', 's5_w_glu', 's5_b_glu', 'w_branch', 'w_out',
                 'w_router_g', 'b_router_g', 'w_router_e', 'b_router_e', 'w_exp_gate', 'w_exp_up', 'w_exp_down')


def kernel(x, c, ctx, c_ctx, norm1_g, norm2_g, final_norm_g, w_ada, b_ada, w_in, rwkv_mu, rwkv_w0, rwkv_w1, rwkv_w2, rwkv_a0, rwkv_a1, rwkv_a2, rwkv_kk, rwkv_ka, rwkv_rk, rwkv_g1, rwkv_g2, rwkv_ln_g, gla_a2, gla_ab, gla_ln_g, attn_sink, s5_lam_re, s5_lam_im, s5_log_dt, s5_b_re, s5_b_im, s5_c_re, s5_c_im, s5_d, s5_w_glu, s5_b_glu, w_branch, w_out, w_router_g, b_router_g, w_router_e, b_router_e, w_exp_gate, w_exp_up, w_exp_down):
    stacked = dict(zip(_LAYER_PARAMS, (
        norm1_g, norm2_g, w_ada, b_ada, w_in, rwkv_mu, rwkv_w0, rwkv_w1, rwkv_w2, rwkv_a0, rwkv_a1, rwkv_a2,
        rwkv_kk, rwkv_ka, rwkv_rk, rwkv_g1, rwkv_g2, rwkv_ln_g, gla_a2, gla_ab, gla_ln_g, attn_sink,
        s5_lam_re, s5_lam_im, s5_log_dt, s5_b_re, s5_b_im, s5_c_re, s5_c_im, s5_d, s5_w_glu, s5_b_glu,
        w_branch, w_out, w_router_g, b_router_g, w_router_e, b_router_e, w_exp_gate, w_exp_up, w_exp_down)))
    n_batch, n_lat, d = x.shape
    n_ctx = ctx.shape[1]
    depth = w_in.shape[0]
    xs = jnp.concatenate([ctx, x], axis=1)
    cvec = jnp.zeros((16, d), F32).at[:n_batch].set(c).at[n_batch].set(c_ctx)
    cos_t, sin_t = rope_tables(n_ctx, n_lat)
    final_g = final_norm_g.reshape(1, d)
    for l in range(depth):
        p = {k: v[l] for k, v in stacked.items()}
        xs = _layer(xs, cvec, p, cos_t, sin_t, final_g, n_ctx, n_batch, l == depth - 1)
    return xs[:, n_ctx:]


def s5_mixer(u, p, n_ctx):
    yf = s5_scan(u, s5_direction_params(p, 0, False), n_ctx, False)
    yr = s5_scan(u, s5_direction_params(p, 1, True), n_ctx, True)
    y = p['s5_d'].reshape(-1) * u + yf + yr
    y = _gelu_tanh(y)
    return y * jax.nn.sigmoid(y @ p['s5_w_glu'] + p['s5_b_glu'])
```

```python
import functools
import math

import jax
import jax.numpy as jnp
from jax import lax
from jax.experimental import pallas as pl
from jax.experimental.pallas import tpu as pltpu

F32 = jnp.float32
BF16 = jnp.bfloat16
HI = lax.Precision.HIGHEST

EPS = 1e-6
GN_EPS = 64e-5
ROPE_BASE = 10000.0
NEG_INF = -1e30
GRID_W = 64

MIX_W = 256
N_BRANCH = 4
HEAD_DIM = 64
RWKV_HEADS = MIX_W // HEAD_DIM
GLA_HEADS = 4
GLA_DK = 32
GLA_DV = 64
GLA_RANK = 16
GLA_TAU = 16.0
ATTN_HEADS = 4
ATTN_KV_HEADS = 2
WINDOW = 128
ATTN_BLOCK = 128
S5_GROUPS = 16
S5_GROUP_CH = MIX_W // S5_GROUPS
S5_STATE = 64
N_GROUPS = 4
EXPERTS_PER_GROUP = 8
N_EXPERTS = N_GROUPS * EXPERTS_PER_GROUP
TOP_K = 2
MOE_BLOCK = 128

RWKV_IN = 4 * MIX_W
GLA_IN = 2 * GLA_HEADS * GLA_DK + 2 * GLA_HEADS * GLA_DV + 2 * GLA_RANK
GLA_IN_PAD = 896
ATTN_IN = (ATTN_HEADS + 2 * ATTN_KV_HEADS) * HEAD_DIM
S5_IN = MIX_W
MIX_IN_PAD = RWKV_IN + GLA_IN_PAD + ATTN_IN + S5_IN

CHUNK = 64
TOKEN_TILE = 256
VMEM_LIMIT = 56 * 1024 * 1024


def _params(sem):
    return pltpu.CompilerParams(dimension_semantics=sem, vmem_limit_bytes=VMEM_LIMIT)


def _sigmoid(z):
    return 1.0 / (1.0 + jnp.exp(-z))


def _dot(a, b, prec=None):
    return jnp.dot(a, b, precision=prec, preferred_element_type=F32)


def _dot_nt(a, b, prec=None):
    return lax.dot_general(a, b, (((1,), (1,)), ((), ())), precision=prec, preferred_element_type=F32)


def _dot_tn(a, b, prec=None):
    return lax.dot_general(a, b, (((0,), (0,)), ((), ())), precision=prec, preferred_element_type=F32)


def _split_bf16(x):
    hi = x.astype(BF16)
    return hi, (x - hi.astype(F32)).astype(BF16)


def _dot_split(x, m):
    hi, lo = _split_bf16(x)
    return _dot(hi, m) + _dot(lo, m)


def _dot_split_rhs(m, x):
    hi, lo = _split_bf16(x)
    return _dot(m, hi) + _dot(m, lo)


def _seg_ones(width, seg):
    r = lax.broadcasted_iota(jnp.int32, (width, width), 0) // seg
    c = lax.broadcasted_iota(jnp.int32, (width, width), 1) // seg
    return (r == c).astype(F32)


def _order_mask(n, reverse, inclusive):
    t = lax.broadcasted_iota(jnp.int32, (n, n), 0)
    i = lax.broadcasted_iota(jnp.int32, (n, n), 1)
    if reverse:
        return (i >= t) if inclusive else (i > t)
    return (i <= t) if inclusive else (i < t)


def _norm_mod(x, g, shift, scale):
    ms = jnp.mean(x * x, axis=-1, keepdims=True)
    return (x * lax.rsqrt(ms + EPS) * g) * (1.0 + scale) + shift


def _ada_kernel(c_ref, w_ref, b_ref, o_ref):
    cv = c_ref[...]
    o_ref[...] = _dot(cv * _sigmoid(cv), w_ref[...], HI) + b_ref[...]


def adaln_mods(cvec, w_ada, b_ada):
    rows, d = cvec.shape
    n = w_ada.shape[1]
    tn = 1024
    return pl.pallas_call(
        _ada_kernel,
        out_shape=jax.ShapeDtypeStruct((rows, n), F32),
        grid=(n // tn,),
        in_specs=[pl.BlockSpec((rows, d), lambda j: (0, 0)),
                  pl.BlockSpec((d, tn), lambda j: (0, j)),
                  pl.BlockSpec((1, tn), lambda j: (0, j))],
        out_specs=pl.BlockSpec((rows, tn), lambda j: (0, j)),
        compiler_params=_params(("parallel",)),
        name="adaln_mods",
    )(cvec, w_ada, b_ada.reshape(1, n))


def _in_proj_kernel(x_ref, g_ref, mod_ref, w_ref, oa_ref, ob_ref, oc_ref, od_ref):
    h = _norm_mod(x_ref[0], g_ref[...], mod_ref[0, 0, 0:1, :], mod_ref[0, 0, 1:2, :])
    u = _dot(h.astype(BF16), w_ref[...])
    o1 = RWKV_IN
    o2 = o1 + GLA_IN_PAD
    o3 = o2 + ATTN_IN
    oa_ref[0] = u[:, :o1]
    ob_ref[0] = u[:, o1:o2]
    oc_ref[0] = u[:, o2:o3]
    od_ref[0] = u[:, o3:]


def in_proj(xs, g, mods, w_mix, n_ctx):
    b, s, d = xs.shape
    tm = TOKEN_TILE
    ctx_tiles = n_ctx // tm
    widths = (RWKV_IN, GLA_IN_PAD, ATTN_IN, S5_IN)
    return pl.pallas_call(
        _in_proj_kernel,
        out_shape=[jax.ShapeDtypeStruct((b, s, w), F32) for w in widths],
        grid=(b, s // tm),
        in_specs=[pl.BlockSpec((1, tm, d), lambda i, j: (i, j, 0)),
                  pl.BlockSpec((1, d), lambda i, j: (0, 0)),
                  pl.BlockSpec((1, 1, 6, d), lambda i, j: (i, (j >= ctx_tiles).astype(jnp.int32), 0, 0)),
                  pl.BlockSpec((d, MIX_IN_PAD), lambda i, j: (0, 0))],
        out_specs=[pl.BlockSpec((1, tm, w), lambda i, j: (i, j, 0)) for w in widths],
        compiler_params=_params(("parallel", "parallel")),
        name="in_proj",
    )(xs, g.reshape(1, d), mods, w_mix)


def _rope(z, cos, sin):
    lane = lax.broadcasted_iota(jnp.int32, z.shape, 1)
    swapped = jnp.where((lane % HEAD_DIM) < HEAD_DIM // 2,
                        pltpu.roll(z, 128 - HEAD_DIM // 2, axis=1),
                        pltpu.roll(z, HEAD_DIM // 2, axis=1))
    return z * cos + swapped * sin


def _attn_kernel(own_ref, prev_ref, next_ref, ctx_ref, cos_o, sin_o, cos_p, sin_p, cos_n, sin_n,
                 sink_ref, o_ref, *, ctx_blocks, n_lat):
    blk = ATTN_BLOCK
    qw = ATTN_HEADS * HEAD_DIM
    kw = ATTN_KV_HEADS * HEAD_DIM
    n = pl.program_id(1) - ctx_blocks
    own = own_ref[0]
    scale = HEAD_DIM ** -0.5
    q = jnp.concatenate([_rope(own[:, 0:128], cos_o[...], sin_o[...]),
                         _rope(own[:, 128:256], cos_o[...], sin_o[...])], axis=1) * scale
    k_band = jnp.concatenate([_rope(prev_ref[0][:, qw:qw + kw], cos_p[...], sin_p[...]),
                              _rope(own[:, qw:qw + kw], cos_o[...], sin_o[...]),
                              _rope(next_ref[0][:, qw:qw + kw], cos_n[...], sin_n[...])], axis=0)
    v_band = jnp.concatenate([prev_ref[0][:, qw + kw:], own[:, qw + kw:], next_ref[0][:, qw + kw:]], axis=0)
    k_ctx = ctx_ref[0][:, qw:qw + kw]
    v_ctx = ctx_ref[0][:, qw + kw:]
    qpos = n * blk + lax.broadcasted_iota(jnp.int32, (blk, 3 * blk), 0)
    kpos = (n - 1) * blk + lax.broadcasted_iota(jnp.int32, (blk, 3 * blk), 1)
    valid = (jnp.abs(qpos - kpos) <= WINDOW) & (kpos >= 0) & (kpos < n_lat) & (n >= 0)
    group = ATTN_HEADS // ATTN_KV_HEADS
    outs = []
    for h in range(ATTN_HEADS):
        kv = h // group
        qh = q[:, h * HEAD_DIM:(h + 1) * HEAD_DIM]
        kh = k_band[:, kv * HEAD_DIM:(kv + 1) * HEAD_DIM]
        vh = v_band[:, kv * HEAD_DIM:(kv + 1) * HEAD_DIM]
        kch = k_ctx[:, kv * HEAD_DIM:(kv + 1) * HEAD_DIM]
        vch = v_ctx[:, kv * HEAD_DIM:(kv + 1) * HEAD_DIM]
        s_w = jnp.where(valid, _dot_nt(qh.astype(BF16), kh.astype(BF16)), NEG_INF)
        s_c = _dot_nt(qh.astype(BF16), kch.astype(BF16))
        sink = sink_ref[0:1, h:h + 1]
        m = jnp.maximum(jnp.maximum(jnp.max(s_w, axis=-1, keepdims=True),
                                    jnp.max(s_c, axis=-1, keepdims=True)), sink)
        p_w = jnp.exp(s_w - m)
        p_c = jnp.exp(s_c - m)
        den = (jnp.sum(p_w, axis=-1, keepdims=True) + jnp.sum(p_c, axis=-1, keepdims=True)
               + jnp.exp(sink - m))
        acc = _dot(p_w.astype(BF16), vh.astype(BF16)) + _dot(p_c.astype(BF16), vch.astype(BF16))
        outs.append(acc / den)
    o_ref[0] = jnp.concatenate(outs, axis=1)


def window_attention(u_att, cos_t, sin_t, sink, n_ctx):
    b, s, w = u_att.shape
    blk = ATTN_BLOCK
    ctx_blocks = n_ctx // blk
    nblk = s // blk

    def own_map(i, j):
        return (i, j, 0)

    def prev_map(i, j):
        return (i, jnp.maximum(j - 1, 0), 0)

    def next_map(i, j):
        return (i, jnp.minimum(j + 1, nblk - 1), 0)

    def tab(m):
        return lambda i, j: m(i, j)[1:]

    kern = functools.partial(_attn_kernel, ctx_blocks=ctx_blocks, n_lat=s - n_ctx)
    return pl.pallas_call(
        kern,
        out_shape=jax.ShapeDtypeStruct((b, s, ATTN_HEADS * HEAD_DIM), F32),
        grid=(b, nblk),
        in_specs=[pl.BlockSpec((1, blk, w), own_map),
                  pl.BlockSpec((1, blk, w), prev_map),
                  pl.BlockSpec((1, blk, w), next_map),
                  pl.BlockSpec((1, n_ctx, w), lambda i, j: (i, 0, 0)),
                  pl.BlockSpec((blk, 128), tab(own_map)), pl.BlockSpec((blk, 128), tab(own_map)),
                  pl.BlockSpec((blk, 128), tab(prev_map)), pl.BlockSpec((blk, 128), tab(prev_map)),
                  pl.BlockSpec((blk, 128), tab(next_map)), pl.BlockSpec((blk, 128), tab(next_map)),
                  pl.BlockSpec((1, 128), lambda i, j: (0, 0))],
        out_specs=pl.BlockSpec((1, blk, ATTN_HEADS * HEAD_DIM), own_map),
        compiler_params=_params(("parallel", "parallel")),
        name="window_attention",
    )(u_att, u_att, u_att, u_att, cos_t, sin_t, cos_t, sin_t, cos_t, sin_t, sink)


def rope_tables(n_ctx, n_lat):
    rows = n_lat // GRID_W
    row = jnp.repeat(jnp.arange(rows, dtype=F32), GRID_W)
    col = jnp.tile(jnp.arange(GRID_W, dtype=F32), rows)
    n_freq = HEAD_DIM // 4
    inv_freq = ROPE_BASE ** (-jnp.arange(n_freq, dtype=F32) / n_freq)
    ang = jnp.concatenate([row[:, None] * inv_freq, col[:, None] * inv_freq], axis=-1)
    cos, sin = jnp.cos(ang), jnp.sin(ang)
    cos_l = jnp.concatenate([cos, cos, cos, cos], axis=-1)
    sin_l = jnp.concatenate([-sin, sin, -sin, sin], axis=-1)
    cos_t = jnp.concatenate([jnp.ones((n_ctx, 128), F32), cos_l], axis=0)
    sin_t = jnp.concatenate([jnp.zeros((n_ctx, 128), F32), sin_l], axis=0)
    return cos_t, sin_t


def _scan_order(j, ctx_steps, n_steps, reverse):
    if not reverse:
        return j
    return jnp.where(j < ctx_steps, ctx_steps - 1 - j, n_steps - 1 - (j - ctx_steps))


def _s5_kernel(u_ref, bblk_ref, cblk_ref, apr_ref, api_ref, pr_ref, pi_ref, y_ref, car_ref, cai_ref,
               *, reverse, n_doubling):
    tm = u_ref.shape[1]
    ns = pr_ref.shape[1]

    @pl.when(pl.program_id(1) == 0)
    def _():
        car_ref[...] = jnp.zeros_like(car_ref)
        cai_ref[...] = jnp.zeros_like(cai_ref)

    bu = _dot(u_ref[0].astype(BF16), bblk_ref[...])
    xr, xi = bu[:, :ns], bu[:, ns:]
    row = lax.broadcasted_iota(jnp.int32, (tm, ns), 0)
    for k in range(n_doubling):
        sh = 1 << k
        if reverse:
            keep = row < tm - sh
            sxr = jnp.where(keep, pltpu.roll(xr, tm - sh, axis=0), 0.0)
            sxi = jnp.where(keep, pltpu.roll(xi, tm - sh, axis=0), 0.0)
        else:
            keep = row >= sh
            sxr = jnp.where(keep, pltpu.roll(xr, sh, axis=0), 0.0)
            sxi = jnp.where(keep, pltpu.roll(xi, sh, axis=0), 0.0)
        ar, ai = apr_ref[k:k + 1, :], api_ref[k:k + 1, :]
        xr, xi = xr + ar * sxr - ai * sxi, xi + ar * sxi + ai * sxr
    cr, ci = car_ref[0:1, :], cai_ref[0:1, :]
    pr, pi = pr_ref[...], pi_ref[...]
    xr, xi = xr + pr * cr - pi * ci, xi + pr * ci + pi * cr
    last = 0 if reverse else tm - 1
    car_ref[0:1, :] = xr[last:last + 1, :]
    cai_ref[0:1, :] = xi[last:last + 1, :]
    y_ref[0] = _dot(jnp.concatenate([xr, xi], axis=1).astype(BF16), cblk_ref[...])


def s5_scan(u, prm, n_ctx, reverse):
    b, s, w = u.shape
    tm = TOKEN_TILE
    ns = S5_GROUPS * S5_STATE
    ctx_tiles, n_tiles = n_ctx // tm, s // tm
    n_doubling = int(math.log2(tm))
    bblk, cblk, apr, api, pr, pi = prm
    kern = functools.partial(_s5_kernel, reverse=reverse, n_doubling=n_doubling)

    def tile_map(i, j):
        return (i, _scan_order(j, ctx_tiles, n_tiles, reverse), 0)

    def const(i, j):
        return (0, 0)

    return pl.pallas_call(
        kern,
        out_shape=jax.ShapeDtypeStruct((b, s, w), F32),
        grid=(b, n_tiles),
        in_specs=[pl.BlockSpec((1, tm, w), tile_map),
                  pl.BlockSpec((w, 2 * ns), const), pl.BlockSpec((2 * ns, w), const),
                  pl.BlockSpec(apr.shape, const), pl.BlockSpec(api.shape, const),
                  pl.BlockSpec((tm, ns), const), pl.BlockSpec((tm, ns), const)],
        out_specs=pl.BlockSpec((1, tm, w), tile_map),
        scratch_shapes=[pltpu.VMEM((8, ns), F32), pltpu.VMEM((8, ns), F32)],
        compiler_params=_params(("parallel", "arbitrary")),
        name="s5_scan_rev" if reverse else "s5_scan_fwd",
    )(u, bblk, cblk, apr, api, pr, pi)


def s5_direction_params(p, d, reverse):
    g, ns, ch = S5_GROUPS, S5_STATE, S5_GROUP_CH
    tm = TOKEN_TILE
    lr, li = p['s5_lam_re'][d], p['s5_lam_im'][d]
    dt = jnp.exp(p['s5_log_dt'][d])[:, None]

    def power(n):
        mag = jnp.exp(n * lr * dt)
        return mag * jnp.cos(n * li * dt), mag * jnp.sin(n * li * dt)

    ar, ai = power(1.0)
    den = lr * lr + li * li
    cr = ((ar - 1.0) * lr + ai * li) / den
    ci = (ai * lr - (ar - 1.0) * li) / den
    bre, bim = p['s5_b_re'], p['s5_b_im']
    bbr = cr[..., None] * bre - ci[..., None] * bim
    bbi = cr[..., None] * bim + ci[..., None] * bre
    eye = jnp.eye(g, dtype=F32)

    def expand_in(z):
        return jnp.einsum('gpc,gh->gchp', z, eye).reshape(g * ch, g * ns)

    def expand_out(z):
        return jnp.einsum('gcp,gh->gphc', z, eye).reshape(g * ns, g * ch)

    bblk = jnp.concatenate([expand_in(bbr), expand_in(bbi)], axis=1).astype(BF16)
    cblk = jnp.concatenate([expand_out(p['s5_c_re'][d]), -expand_out(p['s5_c_im'][d])], axis=0).astype(BF16)
    n_doubling = int(math.log2(tm))
    pw = [power(float(1 << k)) for k in range(n_doubling)]
    apr = jnp.stack([z[0].reshape(-1) for z in pw])
    api = jnp.stack([z[1].reshape(-1) for z in pw])
    steps = jnp.arange(1, tm + 1, dtype=F32)[:, None, None]
    mag = jnp.exp(steps * (lr * dt)[None])
    pr = (mag * jnp.cos(steps * (li * dt)[None])).reshape(tm, g * ns)
    pi = (mag * jnp.sin(steps * (li * dt)[None])).reshape(tm, g * ns)
    if reverse:
        pr, pi = pr[::-1], pi[::-1]
    return bblk, cblk, apr, api, pr, pi


def _log_sigmoid(z):
    return jnp.minimum(z, 0.0) - jnp.log(1.0 + jnp.exp(-jnp.abs(z)))


def _gla_kernel(u_ref, a2_ref, ab_ref, o_ref, st_ref, *, reverse):
    tm = u_ref.shape[1]
    qk = GLA_HEADS * GLA_DK
    vw = GLA_HEADS * GLA_DV

    @pl.when(pl.program_id(1) == 0)
    def _():
        st_ref[...] = jnp.zeros_like(st_ref)

    u = u_ref[0]
    alow = u[:, 2 * qk + 2 * vw:]
    log_a = _log_sigmoid(_dot(alow, a2_ref[...], HI) + ab_ref[...]) * (1.0 / GLA_TAU)
    ti = lax.broadcasted_iota(jnp.int32, (tm, tm), 0)
    tj = lax.broadcasted_iota(jnp.int32, (tm, tm), 1)
    same_chunk = (ti // CHUNK) == (tj // CHUNK)
    before = (tj >= ti) if reverse else (tj <= ti)
    cum = _dot_split_rhs((same_chunk & before).astype(BF16), log_a)
    tot = _dot_split_rhs(same_chunk.astype(BF16), log_a)
    q_dec = u[:, 0:qk] * (GLA_DK ** -0.5) * jnp.exp(cum)
    k = u[:, qk:2 * qk]
    k_inv = (k * jnp.exp(-cum)).astype(BF16)
    k_end = k * jnp.exp(tot - cum)
    decay = jnp.exp(tot)
    v_b = u[:, 2 * qk:2 * qk + vw].astype(BF16)
    head_of_lane = lax.broadcasted_iota(jnp.int32, (tm, qk), 1) // GLA_DK
    q_m = [jnp.where(head_of_lane == h, q_dec, 0.0).astype(BF16) for h in range(GLA_HEADS)]
    ke_m = [jnp.where(head_of_lane == h, k_end, 0.0).astype(BF16) for h in range(GLA_HEADS)]

    incl = _order_mask(CHUNK, reverse, True)
    n_chunks = tm // CHUNK
    units = [(c, h) for c in range(n_chunks) for h in range(GLA_HEADS)]

    def rows_of(c):
        return slice(c * CHUNK, (c + 1) * CHUNK)

    vh = {(c, h): v_b[rows_of(c), h * GLA_DV:(h + 1) * GLA_DV] for c, h in units}
    scores = {(c, h): jnp.where(incl, _dot_nt(q_m[h][rows_of(c)], k_inv[rows_of(c)]), 0.0).astype(BF16)
              for c, h in units}
    intra = {u_: _dot(scores[u_], vh[u_]) for u_ in units}
    vk = {(c, h): _dot_tn(vh[c, h], ke_m[h][rows_of(c)]) for c, h in units}

    heads = range(GLA_HEADS)
    order = range(n_chunks - 1, -1, -1) if reverse else range(n_chunks)
    for c in order:
        st = [st_ref[h] for h in heads]
        o_ref[0, rows_of(c), :] = jnp.concatenate(
            [intra[c, h] + _dot_nt(q_m[h][rows_of(c)], st[h].astype(BF16)) for h in heads], axis=1)
        for h in heads:
            st_ref[h] = st[h] * decay[c * CHUNK:c * CHUNK + 1, :] + vk[c, h]


def gla_scan(u, a2pad, ab, n_ctx, reverse):
    b, s, w = u.shape
    tm = TOKEN_TILE
    ctx_tiles, n_tiles = n_ctx // tm, s // tm
    qk = GLA_HEADS * GLA_DK

    def tile_map(i, j):
        return (i, _scan_order(j, ctx_tiles, n_tiles, reverse), 0)

    return pl.pallas_call(
        functools.partial(_gla_kernel, reverse=reverse),
        out_shape=jax.ShapeDtypeStruct((b, s, GLA_HEADS * GLA_DV), F32),
        grid=(b, n_tiles),
        in_specs=[pl.BlockSpec((1, tm, w), tile_map),
                  pl.BlockSpec((128, qk), lambda i, j: (0, 0)),
                  pl.BlockSpec((1, qk), lambda i, j: (0, 0))],
        out_specs=pl.BlockSpec((1, tm, GLA_HEADS * GLA_DV), tile_map),
        scratch_shapes=[pltpu.VMEM((GLA_HEADS, GLA_DV, GLA_HEADS * GLA_DK), F32)],
        compiler_params=_params(("parallel", "arbitrary")),
        name="gla_scan_rev" if reverse else "gla_scan_fwd",
    )(u, a2pad, ab)


def gla_direction_params(p, d):
    a2pad = jnp.zeros((128, GLA_HEADS * GLA_DK), F32).at[d * GLA_RANK:(d + 1) * GLA_RANK].set(p['gla_a2'][d])
    return a2pad, p['gla_ab'][d].reshape(1, -1)


def _rwkv_kernel(u_ref, up_ref, un_ref, mu_ref, kkp_ref, ka_ref, rk_ref, w0_ref, w1_ref, w2_ref,
                 a0_ref, a1_ref, a2_ref, g1_ref, g2_ref, *rest, reverse, ctx_tiles, n_tiles, emit_aux):
    if emit_aux:
        y_ref, bonus_ref, gate_ref, st_ref = rest
    else:
        y_ref, st_ref = rest
    tm = u_ref.shape[1]
    w = MIX_W
    j = pl.program_id(1)
    pos = _scan_order(j, ctx_tiles, n_tiles, reverse)

    @pl.when(j == 0)
    def _():
        st_ref[...] = jnp.zeros_like(st_ref)

    first = (pos == 0) | (pos == ctx_tiles)
    last_tile = (pos == ctx_tiles - 1) | (pos == n_tiles - 1)
    prev_row = jnp.where(first, 0.0, up_ref[0, 7:8, :])
    next_row = jnp.where(last_tile, 0.0, un_ref[0, 0:1, :])
    u = u_ref[0]
    row = lax.broadcasted_iota(jnp.int32, u.shape, 0)
    down = jnp.where(row == 0, prev_row, pltpu.roll(u, 1, axis=0))
    up = jnp.where(row == tm - 1, next_row, pltpu.roll(u, tm - 1, axis=0))
    mixed = u + (0.5 * (down + up) - u) * mu_ref[...]
    r, k, v, xa = mixed[:, 0:w], mixed[:, w:2 * w], mixed[:, 2 * w:3 * w], mixed[:, 3 * w:4 * w]
    seg = _seg_ones(w, HEAD_DIM).astype(BF16)
    kk = k * kkp_ref[...]
    kk = kk * lax.rsqrt(_dot_split(kk * kk, seg) + EPS)
    xab = xa.astype(BF16)
    zw = w0_ref[...] + _dot(jnp.tanh(_dot(xab, w1_ref[...])).astype(BF16), w2_ref[...])
    log_w = -math.exp(-0.5) * _sigmoid(zw)
    iclr = _sigmoid(a0_ref[...] + _dot(_dot(xab, a1_ref[...]).astype(BF16), a2_ref[...]))
    k_eff = k * (1.0 + (iclr - 1.0) * ka_ref[...])
    a_vec = -kk
    b_vec = kk * iclr
    if emit_aux:
        gate_ref[0] = _dot(_sigmoid(_dot(xab, g1_ref[...])).astype(BF16), g2_ref[...])
        bonus_ref[0] = _dot_split(r * k * rk_ref[...], seg) * v

    ti = lax.broadcasted_iota(jnp.int32, (tm, tm), 0)
    tj = lax.broadcasted_iota(jnp.int32, (tm, tm), 1)
    same_chunk = (ti // CHUNK) == (tj // CHUNK)
    before = (tj >= ti) if reverse else (tj <= ti)
    cum = _dot_split_rhs((same_chunk & before).astype(BF16), log_w)
    tot = _dot_split_rhs(same_chunk.astype(BF16), log_w)
    e_neg = jnp.exp(-cum)
    e_end = jnp.exp(tot - cum)
    decay = jnp.exp(tot)
    a_t = a_vec * jnp.exp(cum - log_w)
    r_t = r * jnp.exp(cum)
    b_end = b_vec * e_end
    k_end = k_eff * e_end
    lane = lax.broadcasted_iota(jnp.int32, (tm, w), 1)
    even = (lane // HEAD_DIM) % 2 == 0

    def by_parity(z):
        return jnp.where(even, z, 0.0).astype(BF16), jnp.where(even, 0.0, z).astype(BF16)

    a_par, r_par, be_par, ke_par = by_parity(a_t), by_parity(r_t), by_parity(b_end), by_parity(k_end)
    b_b = (b_vec * e_neg).astype(BF16)
    k_b = (k_eff * e_neg).astype(BF16)
    v_b = v.astype(BF16)

    incl = _order_mask(CHUNK, reverse, True)
    strict = _order_mask(CHUNK, reverse, False)
    eye = (lax.broadcasted_iota(jnp.int32, (CHUNK, CHUNK), 0)
           == lax.broadcasted_iota(jnp.int32, (CHUNK, CHUNK), 1)).astype(F32)
    n_chunks = tm // CHUNK
    n_doubling = int(math.log2(CHUNK)) - 1

    units = [(c, h) for c in range(n_chunks) for h in range(RWKV_HEADS)]

    def rows_of(c):
        return slice(c * CHUNK, (c + 1) * CHUNK)

    def lanes_of(h):
        return slice((h // 2) * 128, (h // 2 + 1) * 128)

    lhs = {(c, h): jnp.concatenate([a_par[h % 2][rows_of(c), lanes_of(h)],
                                    r_par[h % 2][rows_of(c), lanes_of(h)]], axis=0) for c, h in units}
    m = {(c, h): _dot_nt(lhs[c, h], jnp.concatenate([b_b[rows_of(c), lanes_of(h)],
                                                     k_b[rows_of(c), lanes_of(h)]], axis=0)) for c, h in units}
    a_ab = {u_: jnp.where(strict, m[u_][:CHUNK, :CHUNK], 0.0) for u_ in units}
    a_ak = {u_: jnp.where(strict, m[u_][:CHUNK, CHUNK:], 0.0).astype(BF16) for u_ in units}
    a_rb = {u_: jnp.where(incl, m[u_][CHUNK:, :CHUNK], 0.0).astype(BF16) for u_ in units}
    a_rk = {u_: jnp.where(incl, m[u_][CHUNK:, CHUNK:], 0.0).astype(BF16) for u_ in units}
    inv = {u_: eye + a_ab[u_] for u_ in units}
    pw = a_ab
    for _ in range(n_doubling):
        pwb = {u_: pw[u_].astype(BF16) for u_ in units}
        pw = {u_: _dot(pwb[u_], pwb[u_]) for u_ in units}
        inv = {u_: inv[u_] + _dot(inv[u_].astype(BF16), pw[u_].astype(BF16)) for u_ in units}
    inv_b = {u_: inv[u_].astype(BF16) for u_ in units}
    vh = {(c, h): v_b[rows_of(c), h * HEAD_DIM:(h + 1) * HEAD_DIM] for c, h in units}
    akv = {u_: _dot(a_ak[u_], vh[u_]) for u_ in units}
    arkv = {u_: _dot(a_rk[u_], vh[u_]) for u_ in units}
    vk = {(c, h): _dot_tn(vh[c, h], ke_par[h % 2][rows_of(c), lanes_of(h)]) for c, h in units}

    heads = range(RWKV_HEADS)
    order = range(n_chunks - 1, -1, -1) if reverse else range(n_chunks)
    for c in order:
        st = [st_ref[h] for h in heads]
        x0 = [_dot_nt(lhs[c, h], st[h].astype(BF16)) for h in heads]
        uu = [_dot(inv_b[c, h], (x0[h][:CHUNK] + akv[c, h]).astype(BF16)).astype(BF16) for h in heads]
        for h in heads:
            st_ref[h] = (st[h] * decay[c * CHUNK:c * CHUNK + 1, lanes_of(h)]
                         + _dot_tn(uu[h], be_par[h % 2][rows_of(c), lanes_of(h)]) + vk[c, h])
        y_ref[0, rows_of(c), :] = jnp.concatenate(
            [x0[h][CHUNK:] + _dot(a_rb[c, h], uu[h]) + arkv[c, h] for h in heads], axis=1)


def rwkv_scan(u, prm, n_ctx, reverse, emit_aux):
    b, s, w4 = u.shape
    tm = TOKEN_TILE
    w = MIX_W
    ctx_tiles, n_tiles = n_ctx // tm, s // tm
    halo_blocks = tm // 8

    def tile_map(i, j):
        return (i, _scan_order(j, ctx_tiles, n_tiles, reverse), 0)

    def up_map(i, j):
        return (i, jnp.maximum(_scan_order(j, ctx_tiles, n_tiles, reverse) * halo_blocks - 1, 0), 0)

    def un_map(i, j):
        return (i, jnp.minimum((_scan_order(j, ctx_tiles, n_tiles, reverse) + 1) * halo_blocks, s // 8 - 1), 0)

    def full(a):
        return pl.BlockSpec(a.shape, lambda i, j: (0,) * a.ndim)

    n_out = 3 if emit_aux else 1
    kern = functools.partial(_rwkv_kernel, reverse=reverse, ctx_tiles=ctx_tiles, n_tiles=n_tiles,
                             emit_aux=emit_aux)
    outs = pl.pallas_call(
        kern,
        out_shape=[jax.ShapeDtypeStruct((b, s, w), F32)] * n_out,
        grid=(b, n_tiles),
        in_specs=[pl.BlockSpec((1, tm, w4), tile_map),
                  pl.BlockSpec((1, 8, w4), up_map),
                  pl.BlockSpec((1, 8, w4), un_map)] + [full(a) for a in prm],
        out_specs=[pl.BlockSpec((1, tm, w), tile_map)] * n_out,
        scratch_shapes=[pltpu.VMEM((RWKV_HEADS, HEAD_DIM, 2 * HEAD_DIM), F32)],
        compiler_params=_params(("parallel", "arbitrary")),
        name="rwkv_scan_rev" if reverse else "rwkv_scan_fwd",
    )(u, u, u, *prm)
    return outs


def rwkv_direction_params(p, d):
    row = lambda z: z.reshape(1, -1)
    bf = lambda z: z.astype(BF16)
    return (row(p['rwkv_mu']), row(p['rwkv_kk']), row(p['rwkv_ka']), row(p['rwkv_rk']),
            row(p['rwkv_w0'][d]), bf(p['rwkv_w1'][d]), bf(p['rwkv_w2'][d]),
            row(p['rwkv_a0'][d]), bf(p['rwkv_a1'][d]), bf(p['rwkv_a2'][d]), bf(p['rwkv_g1']), bf(p['rwkv_g2']))


def _gelu_tanh(y):
    return 0.5 * y * (1.0 + jnp.tanh(math.sqrt(2.0 / math.pi) * (y + 0.044715 * (y * y * y))))


def _merge_kernel(x_ref, g_ref, mod_ref, wgate_ref, wbr_ref, wout_ref,
                  yaf_ref, yar_ref, bonus_ref, gate_ref, lna_ref,
                  gof_ref, gor_ref, ub_ref, lnb_ref,
                  yc_ref, us5_ref, ysf_ref, ysr_ref, s5d_ref, wglu_ref, bglu_ref, o_ref):
    d = x_ref.shape[2]
    x = x_ref[0]
    h = _norm_mod(x, g_ref[...], mod_ref[0, 0, 0:1, :], mod_ref[0, 0, 1:2, :]).astype(BF16)
    seg = _seg_ones(MIX_W, HEAD_DIM) * (1.0 / HEAD_DIM)

    y = yaf_ref[0] + yar_ref[0]
    mu = _dot(y, seg, HI)
    yc0 = y - mu
    var = _dot(yc0 * yc0, seg, HI)
    ya = (yc0 * lax.rsqrt(var + GN_EPS) * lna_ref[...] + bonus_ref[0]) * gate_ref[0]

    o = gof_ref[0] + gor_ref[0]
    qk = GLA_HEADS * GLA_DK
    r = ub_ref[0][:, 2 * qk + MIX_W:2 * qk + 2 * MIX_W]
    yb = o * lax.rsqrt(_dot(o * o, seg, HI) + EPS) * lnb_ref[...] * (r * _sigmoid(r))

    yd = _gelu_tanh(s5d_ref[...] * us5_ref[0] + ysf_ref[0] + ysr_ref[0])
    yd = yd * _sigmoid(_dot(yd.astype(BF16), wglu_ref[...]) + bglu_ref[...])

    merged = None
    for i, yi in enumerate((ya, yb, yc_ref[0], yd)):
        gate_i = _sigmoid(_dot(h, wgate_ref[:, i * d:(i + 1) * d]))
        term = gate_i * _dot(yi.astype(BF16), wbr_ref[i])
        merged = term if merged is None else merged + term
    o_ref[0] = x + mod_ref[0, 0, 2:3, :] * _dot(merged.astype(BF16), wout_ref[...])


def merge_branches(xs, g, mods, w_gate, w_branch, w_out, branch_inputs, n_ctx):
    b, s, d = xs.shape
    tm = TOKEN_TILE
    ctx_tiles = n_ctx // tm
    (yaf, yar, bonus, gate, lna, gof, gor, ub, lnb, yc, us5, ysf, ysr, s5d, wglu, bglu) = branch_inputs

    def tok(a):
        return pl.BlockSpec((1, tm, a.shape[2]), lambda i, j: (i, j, 0))

    def full(a):
        return pl.BlockSpec(a.shape, lambda i, j: (0,) * a.ndim)

    mod_spec = pl.BlockSpec((1, 1, 6, d), lambda i, j: (i, (j >= ctx_tiles).astype(jnp.int32), 0, 0))
    return pl.pallas_call(
        _merge_kernel,
        out_shape=jax.ShapeDtypeStruct((b, s, d), F32),
        grid=(b, s // tm),
        in_specs=[tok(xs), full(g), mod_spec, full(w_gate), full(w_branch), full(w_out),
                  tok(yaf), tok(yar), tok(bonus), tok(gate), full(lna),
                  tok(gof), tok(gor), tok(ub), full(lnb),
                  tok(yc), tok(us5), tok(ysf), tok(ysr), full(s5d), full(wglu), full(bglu)],
        out_specs=tok(xs),
        compiler_params=_params(("parallel", "parallel")),
        name="merge_branches",
    )(xs, g, mods, w_gate, w_branch, w_out, yaf, yar, bonus, gate, lna, gof, gor, ub, lnb,
      yc, us5, ysf, ysr, s5d, wglu, bglu)


def _router_kernel(x_ref, g_ref, mod_ref, wr_ref, br_ref, h_ref, idx_ref, wt_ref):
    h = _norm_mod(x_ref[0], g_ref[...], mod_ref[0, 0, 3:4, :], mod_ref[0, 0, 4:5, :])
    h_ref[...] = h
    logits = _dot(h, wr_ref[...], HI) + br_ref[...]
    lane = lax.broadcasted_iota(jnp.int32, logits.shape, 1)
    big = jnp.int32(1 << 20)

    def first_max(z):
        m = jnp.max(z, axis=-1, keepdims=True)
        return m, jnp.min(jnp.where(z == m, lane, big), axis=-1, keepdims=True)

    gl = jnp.where(lane < N_GROUPS, logits, -jnp.inf)
    gmax, gsel = first_max(gl)
    g_w = 1.0 / jnp.sum(jnp.exp(gl - gmax), axis=-1, keepdims=True)
    lo = N_GROUPS + EXPERTS_PER_GROUP * gsel
    el = jnp.where((lane >= lo) & (lane < lo + EXPERTS_PER_GROUP), logits, -jnp.inf)
    v1, i1 = first_max(el)
    v2, i2 = first_max(jnp.where(lane == i1, -jnp.inf, el))
    e2 = jnp.exp(v2 - v1)
    w1 = g_w / (1.0 + e2)
    w2 = g_w * e2 / (1.0 + e2)
    idx_ref[...] = jnp.where(lane == 0, i1 - N_GROUPS, jnp.where(lane == 1, i2 - N_GROUPS, 0))
    wt_ref[...] = jnp.where(lane == 0, w1, jnp.where(lane == 1, w2, 0.0))


def moe_router(xs, g, mods, w_router, b_router, n_ctx):
    b, s, d = xs.shape
    tm = TOKEN_TILE
    ctx_tiles, n_tiles = n_ctx // tm, s // tm
    n = b * s
    mod_spec = pl.BlockSpec((1, 1, 6, d), lambda i, j: (i, (j >= ctx_tiles).astype(jnp.int32), 0, 0))
    flat = lambda i, j: (i * n_tiles + j, 0)
    return pl.pallas_call(
        _router_kernel,
        out_shape=[jax.ShapeDtypeStruct((n, d), F32), jax.ShapeDtypeStruct((n, 128), jnp.int32),
                   jax.ShapeDtypeStruct((n, 128), F32)],
        grid=(b, n_tiles),
        in_specs=[pl.BlockSpec((1, tm, d), lambda i, j: (i, j, 0)),
                  pl.BlockSpec((1, d), lambda i, j: (0, 0)), mod_spec,
                  pl.BlockSpec((d, 128), lambda i, j: (0, 0)),
                  pl.BlockSpec((1, 128), lambda i, j: (0, 0))],
        out_specs=[pl.BlockSpec((tm, d), flat), pl.BlockSpec((tm, 128), flat), pl.BlockSpec((tm, 128), flat)],
        compiler_params=_params(("parallel", "parallel")),
        name="moe_router",
    )(xs, g, mods, w_router, b_router)


def _row_copy(src_hbm, src_row, dst, dst_row, sem):
    return pltpu.make_async_copy(src_hbm.at[pl.ds(src_row, 1)], dst.at[pl.ds(dst_row, 1)], sem)


def _expert_kernel(blk_e_ref, tok_ref, tok_next_ref, h_hbm, wg_ref, wu_ref, wd_ref, y_ref, xbuf, sem):
    del blk_e_ref
    i = pl.program_id(0)
    slot = i % 2

    def gather(ids_ref, s, start):
        for r in range(MOE_BLOCK):
            cp = _row_copy(h_hbm, ids_ref[0, 0, r], xbuf.at[s], r, sem.at[s])
            if start:
                cp.start()
            else:
                cp.wait()

    @pl.when(i == 0)
    def _():
        gather(tok_ref, 0, True)

    gather(tok_ref, slot, False)
    gather(tok_next_ref, 1 - slot, True)
    xb = xbuf[slot].astype(BF16)
    gate = _dot(xb, wg_ref[0])
    hid = gate * _sigmoid(gate) * _dot(xb, wu_ref[0])
    y_ref[...] = _dot(hid.astype(BF16), wd_ref[0])

    @pl.when(i == pl.num_programs(0) - 1)
    def _():
        gather(tok_next_ref, 1 - slot, False)


def moe_experts(h2, blk_e, buf_tok, w_gate, w_up, w_down):
    n, d = h2.shape
    n_blk = blk_e.shape[0]
    hid = w_gate.shape[2]
    grid_spec = pltpu.PrefetchScalarGridSpec(
        num_scalar_prefetch=1,
        grid=(n_blk,),
        in_specs=[pl.BlockSpec((1, 1, MOE_BLOCK), lambda i, be: (i, 0, 0), memory_space=pltpu.SMEM),
                  pl.BlockSpec((1, 1, MOE_BLOCK), lambda i, be: (jnp.minimum(i + 1, n_blk - 1), 0, 0),
                               memory_space=pltpu.SMEM),
                  pl.BlockSpec(memory_space=pl.ANY),
                  pl.BlockSpec((1, d, hid), lambda i, be: (be[i], 0, 0)),
                  pl.BlockSpec((1, d, hid), lambda i, be: (be[i], 0, 0)),
                  pl.BlockSpec((1, hid, d), lambda i, be: (be[i], 0, 0))],
        out_specs=pl.BlockSpec((MOE_BLOCK, d), lambda i, be: (i, 0)),
        scratch_shapes=[pltpu.VMEM((2, MOE_BLOCK, d), F32), pltpu.SemaphoreType.DMA((2,))],
    )
    tok = buf_tok.reshape(n_blk, 1, MOE_BLOCK)
    return pl.pallas_call(
        _expert_kernel,
        out_shape=jax.ShapeDtypeStruct((n_blk * MOE_BLOCK, d), F32),
        grid_spec=grid_spec,
        compiler_params=_params(("arbitrary",)),
        name="moe_experts",
    )(blk_e, tok, tok, h2, w_gate, w_up, w_down)


def _combine_kernel(pos_ref, pos_next_ref, x_ref, mod_ref, wt_ref, fg_ref, y_hbm, o_ref, rows, sem, *, final_norm):
    tm = x_ref.shape[1]
    i = pl.program_id(0)
    slot = i % 2

    def gather(ids_ref, s, start):
        for r in range(tm):
            for k in range(TOP_K):
                cp = _row_copy(y_hbm, ids_ref[0, 0, TOP_K * r + k], rows.at[s, k], r, sem.at[s])
                if start:
                    cp.start()
                else:
                    cp.wait()

    @pl.when(i == 0)
    def _():
        gather(pos_ref, 0, True)

    gather(pos_ref, slot, False)
    gather(pos_next_ref, 1 - slot, True)
    wt = wt_ref[...]
    y = wt[:, 0:1] * rows[slot, 0] + wt[:, 1:2] * rows[slot, 1]
    out = x_ref[0] + mod_ref[0, 0, 5:6, :] * y
    if final_norm:
        out = out * lax.rsqrt(jnp.mean(out * out, axis=-1, keepdims=True) + EPS) * fg_ref[...]
    o_ref[0] = out

    @pl.when(i == pl.num_programs(0) - 1)
    def _():
        gather(pos_next_ref, 1 - slot, False)


def moe_combine(xs, mods, wts, pos, y_buf, final_g, n_ctx, final_norm):
    b, s, d = xs.shape
    tm = MOE_BLOCK
    ctx_tiles, n_tiles = n_ctx // tm, s // tm
    first = ctx_tiles if final_norm else 0
    per_batch = n_tiles - first
    n_steps = b * per_batch

    def tile_of(i):
        return i // per_batch, i % per_batch + first

    def flat(i):
        bi, j = tile_of(i)
        return bi * n_tiles + j

    def x_map(i):
        bi, j = tile_of(i)
        return (bi, j, 0)

    def mod_map(i):
        bi, j = tile_of(i)
        return (bi, (j >= ctx_tiles).astype(jnp.int32), 0, 0)

    ids = pos.reshape(b * n_tiles, 1, TOP_K * tm)
    return pl.pallas_call(
        functools.partial(_combine_kernel, final_norm=final_norm),
        out_shape=jax.ShapeDtypeStruct((b, per_batch * tm, d), F32),
        grid=(n_steps,),
        in_specs=[pl.BlockSpec((1, 1, TOP_K * tm), lambda i: (flat(i), 0, 0), memory_space=pltpu.SMEM),
                  pl.BlockSpec((1, 1, TOP_K * tm), lambda i: (flat(jnp.minimum(i + 1, n_steps - 1)), 0, 0),
                               memory_space=pltpu.SMEM),
                  pl.BlockSpec((1, tm, d), x_map),
                  pl.BlockSpec((1, 1, 6, d), mod_map),
                  pl.BlockSpec((tm, 128), lambda i: (flat(i), 0)),
                  pl.BlockSpec((1, d), lambda i: (0, 0)),
                  pl.BlockSpec(memory_space=pl.ANY)],
        out_specs=pl.BlockSpec((1, tm, d), lambda i: (i // per_batch, i % per_batch, 0)),
        scratch_shapes=[pltpu.VMEM((2, TOP_K, tm, d), F32), pltpu.SemaphoreType.DMA((2,))],
        compiler_params=_params(("arbitrary",)),
        name="moe_combine",
    )(ids, ids, xs, mods, wts, final_g, y_buf)


def moe_dispatch_plan(expert_ids):
    n = expert_ids.shape[0]
    flat_e = expert_ids.reshape(-1)
    n_assign = n * TOP_K
    iota = jnp.arange(n_assign, dtype=jnp.int32)
    _, order = lax.sort((flat_e, iota), num_keys=1, is_stable=True)
    _, rank = lax.sort((order, iota), num_keys=1, is_stable=True)
    one_hot = flat_e[:, None] == jnp.arange(N_EXPERTS, dtype=jnp.int32)[None, :]
    counts = jnp.sum(one_hot.astype(jnp.int32), axis=0)
    start = jnp.cumsum(counts) - counts
    padded = (counts + MOE_BLOCK - 1) // MOE_BLOCK * MOE_BLOCK
    pend = jnp.cumsum(padded)
    pstart = pend - padded
    pos = rank + jnp.sum(jnp.where(one_hot, (pstart - start)[None, :], 0), axis=1)
    cap = -(-n_assign // MOE_BLOCK) * MOE_BLOCK + N_EXPERTS * MOE_BLOCK
    n_blk = cap // MOE_BLOCK
    blk_first = jnp.arange(n_blk, dtype=jnp.int32) * MOE_BLOCK
    blk_e = jnp.minimum(jnp.sum((pend[None, :] <= blk_first[:, None]).astype(jnp.int32), axis=1), N_EXPERTS - 1)
    in_expert = (blk_first - pstart[blk_e])[:, None] + jnp.arange(MOE_BLOCK, dtype=jnp.int32)[None, :]
    is_real = in_expert < counts[blk_e][:, None]
    sorted_pos = jnp.clip(start[blk_e][:, None] + in_expert, 0, n_assign - 1)
    buf_tok = jnp.where(is_real, order[sorted_pos] // TOP_K, 0).reshape(-1)
    return blk_e, buf_tok, pos


def hier_moe_layer(xs, g2, mods, p, final_g, n_ctx, final_norm):
    d = xs.shape[2]
    w_router = jnp.zeros((d, 128), F32).at[:, :N_GROUPS].set(p['w_router_g']) \
        .at[:, N_GROUPS:N_GROUPS + N_EXPERTS].set(p['w_router_e'])
    b_router = jnp.zeros((1, 128), F32).at[0, :N_GROUPS].set(p['b_router_g']) \
        .at[0, N_GROUPS:N_GROUPS + N_EXPERTS].set(p['b_router_e'])
    h2, idx, wts = moe_router(xs, g2, mods, w_router, b_router, n_ctx)
    blk_e, buf_tok, pos = moe_dispatch_plan(idx[:, :TOP_K])
    y_buf = moe_experts(h2, blk_e, buf_tok, p['w_exp_gate'].astype(BF16), p['w_exp_up'].astype(BF16),
                        p['w_exp_down'].astype(BF16))
    return moe_combine(xs, mods, wts, pos, y_buf, final_g, n_ctx, final_norm)


def _layer(xs, cvec, p, cos_t, sin_t, final_g, n_ctx, n_batch, final_norm):
    b, s, d = xs.shape
    m = adaln_mods(cvec, p['w_ada'], p['b_ada']).reshape(cvec.shape[0], 6, d)
    mods = jnp.stack([jnp.broadcast_to(m[n_batch], (n_batch, 6, d)), m[:n_batch]], axis=1)
    w_in = p['w_in']
    o1 = RWKV_IN
    o2 = o1 + GLA_IN
    o3 = o2 + ATTN_IN
    o4 = o3 + S5_IN
    w_mix = jnp.concatenate([w_in[:, :o2], jnp.zeros((d, GLA_IN_PAD - GLA_IN), F32), w_in[:, o2:o4]],
                            axis=1).astype(BF16)
    w_gate = w_in[:, o4:].astype(BF16)
    g1 = p['norm1_g'].reshape(1, d)
    ua, ub, uatt, us5 = in_proj(xs, g1, mods, w_mix, n_ctx)

    yaf, bonus, gate = rwkv_scan(ua, rwkv_direction_params(p, 0), n_ctx, False, True)
    yar, = rwkv_scan(ua, rwkv_direction_params(p, 1), n_ctx, True, False)
    gof = gla_scan(ub, *gla_direction_params(p, 0), n_ctx, False)
    gor = gla_scan(ub, *gla_direction_params(p, 1), n_ctx, True)
    sink = jnp.zeros((1, 128), F32).at[0, :ATTN_HEADS].set(p['attn_sink'])
    yc = window_attention(uatt, cos_t, sin_t, sink, n_ctx)
    ysf = s5_scan(us5, s5_direction_params(p, 0, False), n_ctx, False)
    ysr = s5_scan(us5, s5_direction_params(p, 1, True), n_ctx, True)

    row = lambda z: z.reshape(1, -1)
    branch_inputs = (yaf, yar, bonus, gate, row(p['rwkv_ln_g']), gof, gor, ub, row(p['gla_ln_g']),
                     yc, us5, ysf, ysr, row(p['s5_d']), p['s5_w_glu'].astype(BF16), row(p['s5_b_glu']))
    xs = merge_branches(xs, g1, mods, w_gate, p['w_branch'].astype(BF16), p['w_out'].astype(BF16),
                        branch_inputs, n_ctx)
    return hier_moe_layer(xs, p['norm2_g'].reshape(1, d), mods, p, final_g, n_ctx, final_norm)


_LAYER_PARAMS = ('norm1_g', 'norm2_g', 'w_ada', 'b_ada', 'w_in', 'rwkv_mu', 'rwkv_w0', 'rwkv_w1', 'rwkv_w2',
                 'rwkv_a0', 'rwkv_a1', 'rwkv_a2', 'rwkv_kk', 'rwkv_ka', 'rwkv_rk', 'rwkv_g1', 'rwkv_g2',
                 'rwkv_ln_g', 'gla_a2', 'gla_ab', 'gla_ln_g', 'attn_sink', 's5_lam_re', 's5_lam_im', 's5_log_dt',
                 's5_b_re', 's5_b_im', 's5_c_re', 's5_c_im', 's5_d', 's5_w_glu', 's5_b_glu', 'w_branch', 'w_out',
                 'w_router_g', 'b_router_g', 'w_router_e', 'b_router_e', 'w_exp_gate', 'w_exp_up', 'w_exp_down')


def kernel(x, c, ctx, c_ctx, norm1_g, norm2_g, final_norm_g, w_ada, b_ada, w_in, rwkv_mu, rwkv_w0, rwkv_w1, rwkv_w2, rwkv_a0, rwkv_a1, rwkv_a2, rwkv_kk, rwkv_ka, rwkv_rk, rwkv_g1, rwkv_g2, rwkv_ln_g, gla_a2, gla_ab, gla_ln_g, attn_sink, s5_lam_re, s5_lam_im, s5_log_dt, s5_b_re, s5_b_im, s5_c_re, s5_c_im, s5_d, s5_w_glu, s5_b_glu, w_branch, w_out, w_router_g, b_router_g, w_router_e, b_router_e, w_exp_gate, w_exp_up, w_exp_down):
    stacked = dict(zip(_LAYER_PARAMS, (
        norm1_g, norm2_g, w_ada, b_ada, w_in, rwkv_mu, rwkv_w0, rwkv_w1, rwkv_w2, rwkv_a0, rwkv_a1, rwkv_a2,
        rwkv_kk, rwkv_ka, rwkv_rk, rwkv_g1, rwkv_g2, rwkv_ln_g, gla_a2, gla_ab, gla_ln_g, attn_sink,
        s5_lam_re, s5_lam_im, s5_log_dt, s5_b_re, s5_b_im, s5_c_re, s5_c_im, s5_d, s5_w_glu, s5_b_glu,
        w_branch, w_out, w_router_g, b_router_g, w_router_e, b_router_e, w_exp_gate, w_exp_up, w_exp_down)))
    n_batch, n_lat, d = x.shape
    n_ctx = ctx.shape[1]
    depth = w_in.shape[0]
    xs = jnp.concatenate([ctx, x], axis=1)
    cvec = jnp.zeros((16, d), F32).at[:n_batch].set(c).at[n_batch].set(c_ctx)
    cos_t, sin_t = rope_tables(n_ctx, n_lat)
    final_g = final_norm_g.reshape(1, d)
    for l in range(depth):
        p = {k: v[l] for k, v in stacked.items()}
        xs = _layer(xs, cvec, p, cos_t, sin_t, final_g, n_ctx, n_batch, l == depth - 1)
    return xs
```

```python
import functools
import math

import jax
import jax.numpy as jnp
from jax import lax
from jax.experimental import pallas as pl
from jax.experimental.pallas import tpu as pltpu

F32 = jnp.float32
BF16 = jnp.bfloat16
HI = lax.Precision.HIGHEST

EPS = 1e-6
GN_EPS = 64e-5
ROPE_BASE = 10000.0
NEG_INF = -1e30
GRID_W = 64

MIX_W = 256
N_BRANCH = 4
HEAD_DIM = 64
RWKV_HEADS = MIX_W // HEAD_DIM
GLA_HEADS = 4
GLA_DK = 32
GLA_DV = 64
GLA_RANK = 16
GLA_TAU = 16.0
ATTN_HEADS = 4
ATTN_KV_HEADS = 2
WINDOW = 128
ATTN_BLOCK = 128
S5_GROUPS = 16
S5_GROUP_CH = MIX_W // S5_GROUPS
S5_STATE = 64
N_GROUPS = 4
EXPERTS_PER_GROUP = 8
N_EXPERTS = N_GROUPS * EXPERTS_PER_GROUP
TOP_K = 2
MOE_BLOCK = 128

RWKV_IN = 4 * MIX_W
GLA_IN = 2 * GLA_HEADS * GLA_DK + 2 * GLA_HEADS * GLA_DV + 2 * GLA_RANK
GLA_IN_PAD = 896
ATTN_IN = (ATTN_HEADS + 2 * ATTN_KV_HEADS) * HEAD_DIM
S5_IN = MIX_W
MIX_IN_PAD = RWKV_IN + GLA_IN_PAD + ATTN_IN + S5_IN

CHUNK = 64
TOKEN_TILE = 256
VMEM_LIMIT = 56 * 1024 * 1024


def _params(sem):
    return pltpu.CompilerParams(dimension_semantics=sem, vmem_limit_bytes=VMEM_LIMIT)


def _sigmoid(z):
    return 1.0 / (1.0 + jnp.exp(-z))


def _dot(a, b, prec=None):
    return jnp.dot(a, b, precision=prec, preferred_element_type=F32)


def _dot_nt(a, b, prec=None):
    return lax.dot_general(a, b, (((1,), (1,)), ((), ())), precision=prec, preferred_element_type=F32)


def _dot_tn(a, b, prec=None):
    return lax.dot_general(a, b, (((0,), (0,)), ((), ())), precision=prec, preferred_element_type=F32)


def _split_bf16(x):
    hi = x.astype(BF16)
    return hi, (x - hi.astype(F32)).astype(BF16)


def _dot_split(x, m):
    hi, lo = _split_bf16(x)
    return _dot(hi, m) + _dot(lo, m)


def _dot_split_rhs(m, x):
    hi, lo = _split_bf16(x)
    return _dot(m, hi) + _dot(m, lo)


def _seg_ones(width, seg):
    r = lax.broadcasted_iota(jnp.int32, (width, width), 0) // seg
    c = lax.broadcasted_iota(jnp.int32, (width, width), 1) // seg
    return (r == c).astype(F32)


def _order_mask(n, reverse, inclusive):
    t = lax.broadcasted_iota(jnp.int32, (n, n), 0)
    i = lax.broadcasted_iota(jnp.int32, (n, n), 1)
    if reverse:
        return (i >= t) if inclusive else (i > t)
    return (i <= t) if inclusive else (i < t)


def _norm_mod(x, g, shift, scale):
    ms = jnp.mean(x * x, axis=-1, keepdims=True)
    return (x * lax.rsqrt(ms + EPS) * g) * (1.0 + scale) + shift


def _ada_kernel(c_ref, w_ref, b_ref, o_ref):
    cv = c_ref[...]
    o_ref[...] = _dot(cv * _sigmoid(cv), w_ref[...], HI) + b_ref[...]


def adaln_mods(cvec, w_ada, b_ada):
    rows, d = cvec.shape
    n = w_ada.shape[1]
    tn = 1024
    return pl.pallas_call(
        _ada_kernel,
        out_shape=jax.ShapeDtypeStruct((rows, n), F32),
        grid=(n // tn,),
        in_specs=[pl.BlockSpec((rows, d), lambda j: (0, 0)),
                  pl.BlockSpec((d, tn), lambda j: (0, j)),
                  pl.BlockSpec((1, tn), lambda j: (0, j))],
        out_specs=pl.BlockSpec((rows, tn), lambda j: (0, j)),
        compiler_params=_params(("parallel",)),
        name="adaln_mods",
    )(cvec, w_ada, b_ada.reshape(1, n))


def _rope(z, cos, sin):
    lane = lax.broadcasted_iota(jnp.int32, z.shape, 1)
    swapped = jnp.where((lane % HEAD_DIM) < HEAD_DIM // 2,
                        pltpu.roll(z, 128 - HEAD_DIM // 2, axis=1),
                        pltpu.roll(z, HEAD_DIM // 2, axis=1))
    return z * cos + swapped * sin


def _in_proj_kernel(x_ref, g_ref, mod_ref, w_ref, cos_ref, sin_ref, oa_ref, ob_ref, oc_ref, od_ref):
    h = _norm_mod(x_ref[0], g_ref[...], mod_ref[0, 0, 0:1, :], mod_ref[0, 0, 1:2, :])
    u = _dot(h.astype(BF16), w_ref[...])
    o1 = RWKV_IN
    o2 = o1 + GLA_IN_PAD
    o3 = o2 + ATTN_IN
    oa_ref[0] = u[:, :o1]
    ob_ref[0] = u[:, o1:o2]
    od_ref[0] = u[:, o3:]
    cos, sin = cos_ref[...], sin_ref[...]
    scale = HEAD_DIM ** -0.5
    qw = ATTN_HEADS * HEAD_DIM
    kw = ATTN_KV_HEADS * HEAD_DIM
    parts = [_rope(u[:, o2 + c:o2 + c + 128], cos, sin) * scale for c in range(0, qw, 128)]
    parts += [_rope(u[:, o2 + qw + c:o2 + qw + c + 128], cos, sin) for c in range(0, kw, 128)]
    parts.append(u[:, o2 + qw + kw:o3])
    oc_ref[0] = jnp.concatenate(parts, axis=1).astype(BF16)


def in_proj(xs, g, mods, w_mix, cos_t, sin_t, n_ctx):
    b, s, d = xs.shape
    tm = TOKEN_TILE
    ctx_tiles = n_ctx // tm
    widths = (RWKV_IN, GLA_IN_PAD, ATTN_IN, S5_IN)
    dtypes = (F32, F32, BF16, F32)
    return pl.pallas_call(
        _in_proj_kernel,
        out_shape=[jax.ShapeDtypeStruct((b, s, w), t) for w, t in zip(widths, dtypes)],
        grid=(b, s // tm),
        in_specs=[pl.BlockSpec((1, tm, d), lambda i, j: (i, j, 0)),
                  pl.BlockSpec((1, d), lambda i, j: (0, 0)),
                  pl.BlockSpec((1, 1, 6, d), lambda i, j: (i, (j >= ctx_tiles).astype(jnp.int32), 0, 0)),
                  pl.BlockSpec((d, MIX_IN_PAD), lambda i, j: (0, 0)),
                  pl.BlockSpec((tm, 128), lambda i, j: (j, 0)),
                  pl.BlockSpec((tm, 128), lambda i, j: (j, 0))],
        out_specs=[pl.BlockSpec((1, tm, w), lambda i, j: (i, j, 0)) for w in widths],
        compiler_params=_params(("parallel", "parallel")),
        name="in_proj",
    )(xs, g.reshape(1, d), mods, w_mix, cos_t, sin_t)


def _attn_kernel(own_ref, prev_ref, next_ref, ctx_ref, sink_ref, o_ref, *, ctx_blocks, n_lat):
    blk = ATTN_BLOCK
    qw = ATTN_HEADS * HEAD_DIM
    kw = ATTN_KV_HEADS * HEAD_DIM
    n = pl.program_id(1) - ctx_blocks
    own = own_ref[0]
    k_all = jnp.concatenate([prev_ref[0][:, qw:qw + kw], own[:, qw:qw + kw], next_ref[0][:, qw:qw + kw],
                             ctx_ref[0][:, qw:qw + kw]], axis=0)
    v_all = jnp.concatenate([prev_ref[0][:, qw + kw:], own[:, qw + kw:], next_ref[0][:, qw + kw:],
                             ctx_ref[0][:, qw + kw:]], axis=0)
    n_keys = k_all.shape[0]
    qpos = n * blk + lax.broadcasted_iota(jnp.int32, (2 * blk, n_keys), 0) % blk
    kcol = lax.broadcasted_iota(jnp.int32, (2 * blk, n_keys), 1)
    kpos = (n - 1) * blk + kcol
    valid = ((jnp.abs(qpos - kpos) <= WINDOW) & (kpos >= 0) & (kpos < n_lat) & (n >= 0)) | (kcol >= 3 * blk)
    low_half = lax.broadcasted_iota(jnp.int32, (blk, 128), 1) < HEAD_DIM
    top_rows = lax.broadcasted_iota(jnp.int32, (2 * blk, 1), 0) < blk
    zero = jnp.zeros((), BF16)
    outs = []
    for c in range(qw // 128):
        qc = own[:, c * 128:(c + 1) * 128]
        lhs = jnp.concatenate([jnp.where(low_half, qc, zero), jnp.where(low_half, zero, qc)], axis=0)
        s = jnp.where(valid, _dot_nt(lhs, k_all), NEG_INF)
        sink = jnp.where(top_rows, sink_ref[0:1, c:c + 1], sink_ref[0:1, c + 2:c + 3])
        m = jnp.maximum(jnp.max(s, axis=-1, keepdims=True), sink)
        p = jnp.exp(s - m)
        den = jnp.sum(p, axis=-1, keepdims=True) + jnp.exp(sink - m)
        o = _dot(p.astype(BF16), v_all) / den
        outs.append(jnp.where(low_half, o[:blk], o[blk:]))
    o_ref[0] = jnp.concatenate(outs, axis=1)


def window_attention(u_att, sink, n_ctx):
    b, s, w = u_att.shape
    blk = ATTN_BLOCK
    ctx_blocks = n_ctx // blk
    nblk = s // blk
    kern = functools.partial(_attn_kernel, ctx_blocks=ctx_blocks, n_lat=s - n_ctx)
    return pl.pallas_call(
        kern,
        out_shape=jax.ShapeDtypeStruct((b, s, ATTN_HEADS * HEAD_DIM), F32),
        grid=(b, nblk),
        in_specs=[pl.BlockSpec((1, blk, w), lambda i, j: (i, j, 0)),
                  pl.BlockSpec((1, blk, w), lambda i, j: (i, jnp.maximum(j - 1, 0), 0)),
                  pl.BlockSpec((1, blk, w), lambda i, j: (i, jnp.minimum(j + 1, nblk - 1), 0)),
                  pl.BlockSpec((1, n_ctx, w), lambda i, j: (i, 0, 0)),
                  pl.BlockSpec((1, 128), lambda i, j: (0, 0))],
        out_specs=pl.BlockSpec((1, blk, ATTN_HEADS * HEAD_DIM), lambda i, j: (i, j, 0)),
        compiler_params=_params(("parallel", "parallel")),
        name="window_attention",
    )(u_att, u_att, u_att, u_att, sink)


def rope_tables(n_ctx, n_lat):
    rows = n_lat // GRID_W
    row = jnp.repeat(jnp.arange(rows, dtype=F32), GRID_W)
    col = jnp.tile(jnp.arange(GRID_W, dtype=F32), rows)
    n_freq = HEAD_DIM // 4
    inv_freq = ROPE_BASE ** (-jnp.arange(n_freq, dtype=F32) / n_freq)
    ang = jnp.concatenate([row[:, None] * inv_freq, col[:, None] * inv_freq], axis=-1)
    cos, sin = jnp.cos(ang), jnp.sin(ang)
    cos_l = jnp.concatenate([cos, cos, cos, cos], axis=-1)
    sin_l = jnp.concatenate([-sin, sin, -sin, sin], axis=-1)
    cos_t = jnp.concatenate([jnp.ones((n_ctx, 128), F32), cos_l], axis=0)
    sin_t = jnp.concatenate([jnp.zeros((n_ctx, 128), F32), sin_l], axis=0)
    return cos_t, sin_t


def _scan_order(j, ctx_steps, n_steps, reverse):
    if not reverse:
        return j
    return jnp.where(j < ctx_steps, ctx_steps - 1 - j, n_steps - 1 - (j - ctx_steps))


S5_TILE = 128
S5_ROW_PAD = 8


def _s5_kernel(u_ref, bblk_ref, cblk_ref, ar_ref, ai_ref, y_ref, bu_ref, xs_ref, car_ref, cai_ref, *, reverse):
    nb, tm, _ = u_ref.shape
    ns = ar_ref.shape[1]
    stride = tm + S5_ROW_PAD

    @pl.when(pl.program_id(0) == 0)
    def _():
        car_ref[...] = jnp.zeros_like(car_ref)
        cai_ref[...] = jnp.zeros_like(cai_ref)

    nk = ns // 128
    for b in range(nb):
        bu = _dot(u_ref[b].astype(BF16), bblk_ref[...])
        for k in range(2 * nk):
            bu_ref[k, b * stride:b * stride + tm, :] = bu[:, k * 128:(k + 1) * 128]
    ar = [jnp.broadcast_to(ar_ref[:, k * 128:(k + 1) * 128], (nb, 128)) for k in range(nk)]
    ai = [jnp.broadcast_to(ai_ref[:, k * 128:(k + 1) * 128], (nb, 128)) for k in range(nk)]

    def step(i, carry):
        t = (tm - 1 - i) if reverse else i
        rows = pl.ds(t, nb, stride=stride)
        new = []
        for k in range(nk):
            xr, xi = carry[k], carry[nk + k]
            new.append((ar[k] * xr - ai[k] * xi + bu_ref[k, rows, :], ar[k] * xi + ai[k] * xr + bu_ref[nk + k, rows, :]))
        for k in range(nk):
            xs_ref[k, rows, :] = new[k][0]
            xs_ref[nk + k, rows, :] = new[k][1]
        return tuple(z[0] for z in new) + tuple(z[1] for z in new)

    init = tuple(car_ref[:, k * 128:(k + 1) * 128] for k in range(nk)) \
        + tuple(cai_ref[:, k * 128:(k + 1) * 128] for k in range(nk))
    final = lax.fori_loop(0, tm, step, init, unroll=4)
    for k in range(nk):
        car_ref[:, k * 128:(k + 1) * 128] = final[k]
        cai_ref[:, k * 128:(k + 1) * 128] = final[nk + k]
    for b in range(nb):
        xs = jnp.concatenate([xs_ref[k, b * stride:b * stride + tm, :] for k in range(2 * nk)], axis=1)
        y_ref[b] = _dot(xs.astype(BF16), cblk_ref[...])


def s5_scan(u, prm, n_ctx, reverse):
    b, s, w = u.shape
    tm = S5_TILE
    ns = S5_GROUPS * S5_STATE
    ctx_tiles, n_tiles = n_ctx // tm, s // tm
    bblk, cblk, ar, ai = prm

    def tile_map(j):
        return (0, _scan_order(j, ctx_tiles, n_tiles, reverse), 0)

    def const(j):
        return (0, 0)

    rows = b * (tm + S5_ROW_PAD)
    return pl.pallas_call(
        functools.partial(_s5_kernel, reverse=reverse),
        out_shape=jax.ShapeDtypeStruct((b, s, w), F32),
        grid=(n_tiles,),
        in_specs=[pl.BlockSpec((b, tm, w), tile_map),
                  pl.BlockSpec((w, 2 * ns), const), pl.BlockSpec((2 * ns, w), const),
                  pl.BlockSpec((1, ns), const), pl.BlockSpec((1, ns), const)],
        out_specs=pl.BlockSpec((b, tm, w), tile_map),
        scratch_shapes=[pltpu.VMEM((2 * ns // 128, rows, 128), F32), pltpu.VMEM((2 * ns // 128, rows, 128), F32),
                        pltpu.VMEM((b, ns), F32), pltpu.VMEM((b, ns), F32)],
        compiler_params=_params(("arbitrary",)),
        name="s5_scan_rev" if reverse else "s5_scan_fwd",
    )(u, bblk, cblk, ar, ai)


def s5_direction_params(p, d):
    g, ns, ch = S5_GROUPS, S5_STATE, S5_GROUP_CH
    lr, li = p['s5_lam_re'][d], p['s5_lam_im'][d]
    dt = jnp.exp(p['s5_log_dt'][d])[:, None]

    def power(n):
        mag = jnp.exp(n * lr * dt)
        return mag * jnp.cos(n * li * dt), mag * jnp.sin(n * li * dt)

    ar, ai = power(1.0)
    den = lr * lr + li * li
    cr = ((ar - 1.0) * lr + ai * li) / den
    ci = (ai * lr - (ar - 1.0) * li) / den
    bre, bim = p['s5_b_re'], p['s5_b_im']
    bbr = cr[..., None] * bre - ci[..., None] * bim
    bbi = cr[..., None] * bim + ci[..., None] * bre
    eye = jnp.eye(g, dtype=F32)

    def expand_in(z):
        return jnp.einsum('gpc,gh->gchp', z, eye).reshape(g * ch, g * ns)

    def expand_out(z):
        return jnp.einsum('gcp,gh->gphc', z, eye).reshape(g * ns, g * ch)

    bblk = jnp.concatenate([expand_in(bbr), expand_in(bbi)], axis=1).astype(BF16)
    cblk = jnp.concatenate([expand_out(p['s5_c_re'][d]), -expand_out(p['s5_c_im'][d])], axis=0).astype(BF16)
    return bblk, cblk, ar.reshape(1, g * ns), ai.reshape(1, g * ns)


def _log_sigmoid(z):
    return jnp.minimum(z, 0.0) - jnp.log(1.0 + jnp.exp(-jnp.abs(z)))


def _gla_kernel(u_ref, a2_ref, ab_ref, o_ref, st_ref, *, reverse):
    tm = u_ref.shape[1]
    qk = GLA_HEADS * GLA_DK
    vw = GLA_HEADS * GLA_DV

    @pl.when(pl.program_id(1) == 0)
    def _():
        st_ref[...] = jnp.zeros_like(st_ref)

    u = u_ref[0]
    alow = u[:, 2 * qk + 2 * vw:]
    log_a = _log_sigmoid(_dot(alow, a2_ref[...], HI) + ab_ref[...]) * (1.0 / GLA_TAU)
    ti = lax.broadcasted_iota(jnp.int32, (tm, tm), 0)
    tj = lax.broadcasted_iota(jnp.int32, (tm, tm), 1)
    same_chunk = (ti // CHUNK) == (tj // CHUNK)
    before = (tj >= ti) if reverse else (tj <= ti)
    cum = _dot_split_rhs((same_chunk & before).astype(BF16), log_a)
    tot = _dot_split_rhs(same_chunk.astype(BF16), log_a)
    q_dec = u[:, 0:qk] * (GLA_DK ** -0.5) * jnp.exp(cum)
    k = u[:, qk:2 * qk]
    k_inv = (k * jnp.exp(-cum)).astype(BF16)
    k_end = k * jnp.exp(tot - cum)
    decay = jnp.exp(tot)
    v_b = u[:, 2 * qk:2 * qk + vw].astype(BF16)
    head_of_lane = lax.broadcasted_iota(jnp.int32, (tm, qk), 1) // GLA_DK
    q_m = [jnp.where(head_of_lane == h, q_dec, 0.0).astype(BF16) for h in range(GLA_HEADS)]
    ke_m = [jnp.where(head_of_lane == h, k_end, 0.0).astype(BF16) for h in range(GLA_HEADS)]

    incl = _order_mask(CHUNK, reverse, True)
    n_chunks = tm // CHUNK
    units = [(c, h) for c in range(n_chunks) for h in range(GLA_HEADS)]

    def rows_of(c):
        return slice(c * CHUNK, (c + 1) * CHUNK)

    vh = {(c, h): v_b[rows_of(c), h * GLA_DV:(h + 1) * GLA_DV] for c, h in units}
    scores = {(c, h): jnp.where(incl, _dot_nt(q_m[h][rows_of(c)], k_inv[rows_of(c)]), 0.0).astype(BF16)
              for c, h in units}
    intra = {u_: _dot(scores[u_], vh[u_]) for u_ in units}
    vk = {(c, h): _dot_tn(vh[c, h], ke_m[h][rows_of(c)]) for c, h in units}

    heads = range(GLA_HEADS)
    order = range(n_chunks - 1, -1, -1) if reverse else range(n_chunks)
    for c in order:
        st = [st_ref[h] for h in heads]
        o_ref[0, rows_of(c), :] = jnp.concatenate(
            [intra[c, h] + _dot_nt(q_m[h][rows_of(c)], st[h].astype(BF16)) for h in heads], axis=1)
        for h in heads:
            st_ref[h] = st[h] * decay[c * CHUNK:c * CHUNK + 1, :] + vk[c, h]


def gla_scan(u, a2pad, ab, n_ctx, reverse):
    b, s, w = u.shape
    tm = TOKEN_TILE
    ctx_tiles, n_tiles = n_ctx // tm, s // tm
    qk = GLA_HEADS * GLA_DK

    def tile_map(i, j):
        return (i, _scan_order(j, ctx_tiles, n_tiles, reverse), 0)

    return pl.pallas_call(
        functools.partial(_gla_kernel, reverse=reverse),
        out_shape=jax.ShapeDtypeStruct((b, s, GLA_HEADS * GLA_DV), F32),
        grid=(b, n_tiles),
        in_specs=[pl.BlockSpec((1, tm, w), tile_map),
                  pl.BlockSpec((128, qk), lambda i, j: (0, 0)),
                  pl.BlockSpec((1, qk), lambda i, j: (0, 0))],
        out_specs=pl.BlockSpec((1, tm, GLA_HEADS * GLA_DV), tile_map),
        scratch_shapes=[pltpu.VMEM((GLA_HEADS, GLA_DV, GLA_HEADS * GLA_DK), F32)],
        compiler_params=_params(("parallel", "arbitrary")),
        name="gla_scan_rev" if reverse else "gla_scan_fwd",
    )(u, a2pad, ab)


def gla_direction_params(p, d):
    a2pad = jnp.zeros((128, GLA_HEADS * GLA_DK), F32).at[d * GLA_RANK:(d + 1) * GLA_RANK].set(p['gla_a2'][d])
    return a2pad, p['gla_ab'][d].reshape(1, -1)


def _rwkv_kernel(u_ref, up_ref, un_ref, mu_ref, kkp_ref, ka_ref, rk_ref, w0_ref, w1_ref, w2_ref,
                 a0_ref, a1_ref, a2_ref, g1_ref, g2_ref, *rest, reverse, ctx_tiles, n_tiles, emit_aux):
    if emit_aux:
        y_ref, bonus_ref, gate_ref, st_ref = rest
    else:
        y_ref, st_ref = rest
    tm = u_ref.shape[1]
    w = MIX_W
    j = pl.program_id(1)
    pos = _scan_order(j, ctx_tiles, n_tiles, reverse)

    @pl.when(j == 0)
    def _():
        st_ref[...] = jnp.zeros_like(st_ref)

    first = (pos == 0) | (pos == ctx_tiles)
    last_tile = (pos == ctx_tiles - 1) | (pos == n_tiles - 1)
    prev_row = jnp.where(first, 0.0, up_ref[0, 7:8, :])
    next_row = jnp.where(last_tile, 0.0, un_ref[0, 0:1, :])
    u = u_ref[0]
    row = lax.broadcasted_iota(jnp.int32, u.shape, 0)
    down = jnp.where(row == 0, prev_row, pltpu.roll(u, 1, axis=0))
    up = jnp.where(row == tm - 1, next_row, pltpu.roll(u, tm - 1, axis=0))
    mixed = u + (0.5 * (down + up) - u) * mu_ref[...]
    r, k, v, xa = mixed[:, 0:w], mixed[:, w:2 * w], mixed[:, 2 * w:3 * w], mixed[:, 3 * w:4 * w]
    seg = _seg_ones(w, HEAD_DIM).astype(BF16)
    kk = k * kkp_ref[...]
    kk = kk * lax.rsqrt(_dot_split(kk * kk, seg) + EPS)
    xab = xa.astype(BF16)
    zw = w0_ref[...] + _dot(jnp.tanh(_dot(xab, w1_ref[...])).astype(BF16), w2_ref[...])
    log_w = -math.exp(-0.5) * _sigmoid(zw)
    iclr = _sigmoid(a0_ref[...] + _dot(_dot(xab, a1_ref[...]).astype(BF16), a2_ref[...]))
    k_eff = k * (1.0 + (iclr - 1.0) * ka_ref[...])
    a_vec = -kk
    b_vec = kk * iclr
    if emit_aux:
        gate_ref[0] = _dot(_sigmoid(_dot(xab, g1_ref[...])).astype(BF16), g2_ref[...])
        bonus_ref[0] = _dot_split(r * k * rk_ref[...], seg) * v

    ti = lax.broadcasted_iota(jnp.int32, (tm, tm), 0)
    tj = lax.broadcasted_iota(jnp.int32, (tm, tm), 1)
    same_chunk = (ti // CHUNK) == (tj // CHUNK)
    before = (tj >= ti) if reverse else (tj <= ti)
    cum = _dot_split_rhs((same_chunk & before).astype(BF16), log_w)
    tot = _dot_split_rhs(same_chunk.astype(BF16), log_w)
    e_neg = jnp.exp(-cum)
    e_end = jnp.exp(tot - cum)
    decay = jnp.exp(tot)
    a_t = a_vec * jnp.exp(cum - log_w)
    r_t = r * jnp.exp(cum)
    b_end = b_vec * e_end
    k_end = k_eff * e_end
    lane = lax.broadcasted_iota(jnp.int32, (tm, w), 1)
    even = (lane // HEAD_DIM) % 2 == 0

    def by_parity(z):
        return jnp.where(even, z, 0.0).astype(BF16), jnp.where(even, 0.0, z).astype(BF16)

    a_par, r_par, be_par, ke_par = by_parity(a_t), by_parity(r_t), by_parity(b_end), by_parity(k_end)
    b_b = (b_vec * e_neg).astype(BF16)
    k_b = (k_eff * e_neg).astype(BF16)
    v_b = v.astype(BF16)

    incl = _order_mask(CHUNK, reverse, True)
    strict = _order_mask(CHUNK, reverse, False)
    eye = (lax.broadcasted_iota(jnp.int32, (CHUNK, CHUNK), 0)
           == lax.broadcasted_iota(jnp.int32, (CHUNK, CHUNK), 1)).astype(F32)
    n_chunks = tm // CHUNK
    n_doubling = int(math.log2(CHUNK)) - 1

    units = [(c, h) for c in range(n_chunks) for h in range(RWKV_HEADS)]

    def rows_of(c):
        return slice(c * CHUNK, (c + 1) * CHUNK)

    def lanes_of(h):
        return slice((h // 2) * 128, (h // 2 + 1) * 128)

    lhs = {(c, h): jnp.concatenate([a_par[h % 2][rows_of(c), lanes_of(h)],
                                    r_par[h % 2][rows_of(c), lanes_of(h)]], axis=0) for c, h in units}
    m = {(c, h): _dot_nt(lhs[c, h], jnp.concatenate([b_b[rows_of(c), lanes_of(h)],
                                                     k_b[rows_of(c), lanes_of(h)]], axis=0)) for c, h in units}
    a_ab = {u_: jnp.where(strict, m[u_][:CHUNK, :CHUNK], 0.0) for u_ in units}
    a_ak = {u_: jnp.where(strict, m[u_][:CHUNK, CHUNK:], 0.0).astype(BF16) for u_ in units}
    a_rb = {u_: jnp.where(incl, m[u_][CHUNK:, :CHUNK], 0.0).astype(BF16) for u_ in units}
    a_rk = {u_: jnp.where(incl, m[u_][CHUNK:, CHUNK:], 0.0).astype(BF16) for u_ in units}
    inv = {u_: eye + a_ab[u_] for u_ in units}
    pw = a_ab
    for _ in range(n_doubling):
        pwb = {u_: pw[u_].astype(BF16) for u_ in units}
        pw = {u_: _dot(pwb[u_], pwb[u_]) for u_ in units}
        inv = {u_: inv[u_] + _dot(inv[u_].astype(BF16), pw[u_].astype(BF16)) for u_ in units}
    inv_b = {u_: inv[u_].astype(BF16) for u_ in units}
    vh = {(c, h): v_b[rows_of(c), h * HEAD_DIM:(h + 1) * HEAD_DIM] for c, h in units}
    akv = {u_: _dot(a_ak[u_], vh[u_]) for u_ in units}
    arkv = {u_: _dot(a_rk[u_], vh[u_]) for u_ in units}
    vk = {(c, h): _dot_tn(vh[c, h], ke_par[h % 2][rows_of(c), lanes_of(h)]) for c, h in units}

    heads = range(RWKV_HEADS)
    order = range(n_chunks - 1, -1, -1) if reverse else range(n_chunks)
    for c in order:
        st = [st_ref[h] for h in heads]
        x0 = [_dot_nt(lhs[c, h], st[h].astype(BF16)) for h in heads]
        uu = [_dot(inv_b[c, h], (x0[h][:CHUNK] + akv[c, h]).astype(BF16)).astype(BF16) for h in heads]
        for h in heads:
            st_ref[h] = (st[h] * decay[c * CHUNK:c * CHUNK + 1, lanes_of(h)]
                         + _dot_tn(uu[h], be_par[h % 2][rows_of(c), lanes_of(h)]) + vk[c, h])
        y_ref[0, rows_of(c), :] = jnp.concatenate(
            [x0[h][CHUNK:] + _dot(a_rb[c, h], uu[h]) + arkv[c, h] for h in heads], axis=1)


def rwkv_scan(u, prm, n_ctx, reverse, emit_aux):
    b, s, w4 = u.shape
    tm = TOKEN_TILE
    w = MIX_W
    ctx_tiles, n_tiles = n_ctx // tm, s // tm
    halo_blocks = tm // 8

    def tile_map(i, j):
        return (i, _scan_order(j, ctx_tiles, n_tiles, reverse), 0)

    def up_map(i, j):
        return (i, jnp.maximum(_scan_order(j, ctx_tiles, n_tiles, reverse) * halo_blocks - 1, 0), 0)

    def un_map(i, j):
        return (i, jnp.minimum((_scan_order(j, ctx_tiles, n_tiles, reverse) + 1) * halo_blocks, s // 8 - 1), 0)

    def full(a):
        return pl.BlockSpec(a.shape, lambda i, j: (0,) * a.ndim)

    n_out = 3 if emit_aux else 1
    kern = functools.partial(_rwkv_kernel, reverse=reverse, ctx_tiles=ctx_tiles, n_tiles=n_tiles,
                             emit_aux=emit_aux)
    outs = pl.pallas_call(
        kern,
        out_shape=[jax.ShapeDtypeStruct((b, s, w), F32)] * n_out,
        grid=(b, n_tiles),
        in_specs=[pl.BlockSpec((1, tm, w4), tile_map),
                  pl.BlockSpec((1, 8, w4), up_map),
                  pl.BlockSpec((1, 8, w4), un_map)] + [full(a) for a in prm],
        out_specs=[pl.BlockSpec((1, tm, w), tile_map)] * n_out,
        scratch_shapes=[pltpu.VMEM((RWKV_HEADS, HEAD_DIM, 2 * HEAD_DIM), F32)],
        compiler_params=_params(("parallel", "arbitrary")),
        name="rwkv_scan_rev" if reverse else "rwkv_scan_fwd",
    )(u, u, u, *prm)
    return outs


def rwkv_direction_params(p, d):
    row = lambda z: z.reshape(1, -1)
    bf = lambda z: z.astype(BF16)
    return (row(p['rwkv_mu']), row(p['rwkv_kk']), row(p['rwkv_ka']), row(p['rwkv_rk']),
            row(p['rwkv_w0'][d]), bf(p['rwkv_w1'][d]), bf(p['rwkv_w2'][d]),
            row(p['rwkv_a0'][d]), bf(p['rwkv_a1'][d]), bf(p['rwkv_a2'][d]), bf(p['rwkv_g1']), bf(p['rwkv_g2']))


def _gelu_tanh(y):
    return 0.5 * y * (1.0 + jnp.tanh(math.sqrt(2.0 / math.pi) * (y + 0.044715 * (y * y * y))))


def _merge_kernel(x_ref, g_ref, mod_ref, wgate_ref, wbr_ref, wout_ref,
                  yaf_ref, yar_ref, bonus_ref, gate_ref, lna_ref,
                  gof_ref, gor_ref, ub_ref, lnb_ref,
                  yc_ref, us5_ref, ysf_ref, ysr_ref, s5d_ref, wglu_ref, bglu_ref, o_ref):
    d = x_ref.shape[2]
    x = x_ref[0]
    h = _norm_mod(x, g_ref[...], mod_ref[0, 0, 0:1, :], mod_ref[0, 0, 1:2, :]).astype(BF16)
    seg = _seg_ones(MIX_W, HEAD_DIM) * (1.0 / HEAD_DIM)

    y = yaf_ref[0] + yar_ref[0]
    mu = _dot(y, seg, HI)
    yc0 = y - mu
    var = _dot(yc0 * yc0, seg, HI)
    ya = (yc0 * lax.rsqrt(var + GN_EPS) * lna_ref[...] + bonus_ref[0]) * gate_ref[0]

    o = gof_ref[0] + gor_ref[0]
    qk = GLA_HEADS * GLA_DK
    r = ub_ref[0][:, 2 * qk + MIX_W:2 * qk + 2 * MIX_W]
    yb = o * lax.rsqrt(_dot(o * o, seg, HI) + EPS) * lnb_ref[...] * (r * _sigmoid(r))

    yd = _gelu_tanh(s5d_ref[...] * us5_ref[0] + ysf_ref[0] + ysr_ref[0])
    yd = yd * _sigmoid(_dot(yd.astype(BF16), wglu_ref[...]) + bglu_ref[...])

    merged = None
    for i, yi in enumerate((ya, yb, yc_ref[0], yd)):
        gate_i = _sigmoid(_dot(h, wgate_ref[:, i * d:(i + 1) * d]))
        term = gate_i * _dot(yi.astype(BF16), wbr_ref[i])
        merged = term if merged is None else merged + term
    o_ref[0] = x + mod_ref[0, 0, 2:3, :] * _dot(merged.astype(BF16), wout_ref[...])


def merge_branches(xs, g, mods, w_gate, w_branch, w_out, branch_inputs, n_ctx):
    b, s, d = xs.shape
    tm = TOKEN_TILE
    ctx_tiles = n_ctx // tm
    (yaf, yar, bonus, gate, lna, gof, gor, ub, lnb, yc, us5, ysf, ysr, s5d, wglu, bglu) = branch_inputs

    def tok(a):
        return pl.BlockSpec((1, tm, a.shape[2]), lambda i, j: (i, j, 0))

    def full(a):
        return pl.BlockSpec(a.shape, lambda i, j: (0,) * a.ndim)

    mod_spec = pl.BlockSpec((1, 1, 6, d), lambda i, j: (i, (j >= ctx_tiles).astype(jnp.int32), 0, 0))
    return pl.pallas_call(
        _merge_kernel,
        out_shape=jax.ShapeDtypeStruct((b, s, d), F32),
        grid=(b, s // tm),
        in_specs=[tok(xs), full(g), mod_spec, full(w_gate), full(w_branch), full(w_out),
                  tok(yaf), tok(yar), tok(bonus), tok(gate), full(lna),
                  tok(gof), tok(gor), tok(ub), full(lnb),
                  tok(yc), tok(us5), tok(ysf), tok(ysr), full(s5d), full(wglu), full(bglu)],
        out_specs=tok(xs),
        compiler_params=_params(("parallel", "parallel")),
        name="merge_branches",
    )(xs, g, mods, w_gate, w_branch, w_out, yaf, yar, bonus, gate, lna, gof, gor, ub, lnb,
      yc, us5, ysf, ysr, s5d, wglu, bglu)


def _router_kernel(x_ref, g_ref, mod_ref, wr_ref, br_ref, h_ref, idx_ref, wt_ref):
    h = _norm_mod(x_ref[0], g_ref[...], mod_ref[0, 0, 3:4, :], mod_ref[0, 0, 4:5, :])
    h_ref[...] = h
    h_hi, h_lo = _split_bf16(h)
    logits = _dot(h_hi, wr_ref[0]) + (_dot(h_lo, wr_ref[0]) + _dot(h_hi, wr_ref[1])) + br_ref[...]
    lane = lax.broadcasted_iota(jnp.int32, logits.shape, 1)
    big = jnp.int32(1 << 20)

    def first_max(z):
        m = jnp.max(z, axis=-1, keepdims=True)
        return m, jnp.min(jnp.where(z == m, lane, big), axis=-1, keepdims=True)

    gl = jnp.where(lane < N_GROUPS, logits, -jnp.inf)
    gmax, gsel = first_max(gl)
    g_w = 1.0 / jnp.sum(jnp.exp(gl - gmax), axis=-1, keepdims=True)
    lo = N_GROUPS + EXPERTS_PER_GROUP * gsel
    el = jnp.where((lane >= lo) & (lane < lo + EXPERTS_PER_GROUP), logits, -jnp.inf)
    v1, i1 = first_max(el)
    v2, i2 = first_max(jnp.where(lane == i1, -jnp.inf, el))
    e2 = jnp.exp(v2 - v1)
    w1 = g_w / (1.0 + e2)
    w2 = g_w * e2 / (1.0 + e2)
    idx_ref[...] = jnp.where(lane == 0, i1 - N_GROUPS, jnp.where(lane == 1, i2 - N_GROUPS, 0))
    wt_ref[...] = jnp.where(lane == 0, w1, jnp.where(lane == 1, w2, 0.0))


def moe_router(xs, g, mods, w_router, b_router, n_ctx):
    b, s, d = xs.shape
    tm = TOKEN_TILE
    ctx_tiles, n_tiles = n_ctx // tm, s // tm
    n = b * s
    mod_spec = pl.BlockSpec((1, 1, 6, d), lambda i, j: (i, (j >= ctx_tiles).astype(jnp.int32), 0, 0))
    flat = lambda i, j: (i * n_tiles + j, 0)
    return pl.pallas_call(
        _router_kernel,
        out_shape=[jax.ShapeDtypeStruct((n, d), F32), jax.ShapeDtypeStruct((n, 128), jnp.int32),
                   jax.ShapeDtypeStruct((n, 128), F32)],
        grid=(b, n_tiles),
        in_specs=[pl.BlockSpec((1, tm, d), lambda i, j: (i, j, 0)),
                  pl.BlockSpec((1, d), lambda i, j: (0, 0)), mod_spec,
                  pl.BlockSpec((2, d, 128), lambda i, j: (0, 0, 0)),
                  pl.BlockSpec((1, 128), lambda i, j: (0, 0))],
        out_specs=[pl.BlockSpec((tm, d), flat), pl.BlockSpec((tm, 128), flat), pl.BlockSpec((tm, 128), flat)],
        compiler_params=_params(("parallel", "parallel")),
        name="moe_router",
    )(xs, g, mods, w_router, b_router)


def _row_copy(src_hbm, src_row, dst, dst_row, sem):
    return pltpu.make_async_copy(src_hbm.at[pl.ds(src_row, 1)], dst.at[pl.ds(dst_row, 1)], sem)


def _expert_kernel(blk_e_ref, tok_ref, tok_next_ref, dst_ref, dst_prev_ref, dst_prev2_ref,
                   h_hbm, wg_ref, wu_ref, wd_ref, out_hbm, xbuf, ybuf, wgb, wub, wdb, gsem, ssem):
    i = pl.program_id(0)
    last = pl.num_programs(0) - 1
    slot = i % 2

    @pl.when((i == 0) | (blk_e_ref[i] != blk_e_ref[jnp.maximum(i - 1, 0)]))
    def _():
        wgb[...] = wg_ref[0].astype(BF16)
        wub[...] = wu_ref[0].astype(BF16)
        wdb[...] = wd_ref[0].astype(BF16)

    def gather(ids_ref, s, start, rows=range(MOE_BLOCK)):
        for r in rows:
            cp = _row_copy(h_hbm, ids_ref[0, 0, r], xbuf.at[s], r, gsem.at[s])
            if start:
                cp.start()
            else:
                cp.wait()

    def scatter(ids_ref, s, start, rows=range(MOE_BLOCK)):
        for r in rows:
            cp = _row_copy(ybuf.at[s], r, out_hbm, ids_ref[0, 0, r], ssem.at[s])
            if start:
                cp.start()
            else:
                cp.wait()

    @pl.when(i == 0)
    def _():
        gather(tok_ref, 0, True)

    gather(tok_ref, slot, False)
    xb = xbuf[slot].astype(BF16)
    n_slices = 4
    hw = wgb.shape[1] // n_slices
    per_slice = MOE_BLOCK // n_slices
    acc = None
    for k in range(n_slices):
        rows = range(k * per_slice, (k + 1) * per_slice)
        gather(tok_next_ref, 1 - slot, True, rows)

        @pl.when(i >= 1)
        def _():
            scatter(dst_prev_ref, 1 - slot, True, rows)

        gate = _dot(xb, wgb[:, k * hw:(k + 1) * hw])
        hid = gate * _sigmoid(gate) * _dot(xb, wub[:, k * hw:(k + 1) * hw])
        part = _dot(hid.astype(BF16), wdb[k * hw:(k + 1) * hw, :])
        acc = part if acc is None else acc + part

    @pl.when(i >= 2)
    def _():
        scatter(dst_prev2_ref, slot, False)

    ybuf[slot] = acc

    @pl.when(i == last)
    def _():
        gather(tok_next_ref, 1 - slot, False)
        scatter(dst_ref, slot, True)

        @pl.when(i >= 1)
        def _():
            scatter(dst_prev_ref, 1 - slot, False)

        scatter(dst_ref, slot, False)


def moe_experts(h2, blk_e, buf_tok, buf_dst, w_gate, w_up, w_down):
    n, d = h2.shape
    n_blk = blk_e.shape[0]
    hid = w_gate.shape[2]

    def ids(shift):
        return pl.BlockSpec((1, 1, MOE_BLOCK), lambda i, be: (jnp.clip(i + shift, 0, n_blk - 1), 0, 0),
                            memory_space=pltpu.SMEM)

    grid_spec = pltpu.PrefetchScalarGridSpec(
        num_scalar_prefetch=1,
        grid=(n_blk,),
        in_specs=[ids(0), ids(1), ids(0), ids(-1), ids(-2),
                  pl.BlockSpec(memory_space=pl.ANY),
                  pl.BlockSpec((1, d, hid), lambda i, be: (be[i], 0, 0)),
                  pl.BlockSpec((1, d, hid), lambda i, be: (be[i], 0, 0)),
                  pl.BlockSpec((1, hid, d), lambda i, be: (be[i], 0, 0))],
        out_specs=pl.BlockSpec(memory_space=pl.ANY),
        scratch_shapes=[pltpu.VMEM((2, MOE_BLOCK, d), F32), pltpu.VMEM((2, MOE_BLOCK, d), F32),
                        pltpu.VMEM((d, hid), BF16), pltpu.VMEM((d, hid), BF16), pltpu.VMEM((hid, d), BF16),
                        pltpu.SemaphoreType.DMA((2,)), pltpu.SemaphoreType.DMA((2,))],
    )
    tok = buf_tok.reshape(n_blk, 1, MOE_BLOCK)
    dst = buf_dst.reshape(n_blk, 1, MOE_BLOCK)
    return pl.pallas_call(
        _expert_kernel,
        out_shape=jax.ShapeDtypeStruct((n_blk * MOE_BLOCK, d), F32),
        grid_spec=grid_spec,
        compiler_params=_params(("arbitrary",)),
        name="moe_experts",
    )(blk_e, tok, tok, dst, dst, dst, h2, w_gate, w_up, w_down)


def _combine_kernel(x_ref, mod_ref, wt_ref, fg_ref, y0_ref, y1_ref, o_ref, *, final_norm):
    wt = wt_ref[...]
    y = wt[:, 0:1] * y0_ref[...] + wt[:, 1:2] * y1_ref[...]
    out = x_ref[0] + mod_ref[0, 0, 5:6, :] * y
    if final_norm:
        out = out * lax.rsqrt(jnp.mean(out * out, axis=-1, keepdims=True) + EPS) * fg_ref[...]
    o_ref[0] = out


def moe_combine(xs, mods, wts, y_rows, final_g, n_ctx, final_norm):
    b, s, d = xs.shape
    tm = TOKEN_TILE
    ctx_tiles, n_tiles = n_ctx // tm, s // tm
    first = ctx_tiles if final_norm else 0
    per_batch = n_tiles - first
    k_stride = b * n_tiles

    def flat(i, j):
        return i * n_tiles + j + first

    return pl.pallas_call(
        functools.partial(_combine_kernel, final_norm=final_norm),
        out_shape=jax.ShapeDtypeStruct((b, per_batch * tm, d), F32),
        grid=(b, per_batch),
        in_specs=[pl.BlockSpec((1, tm, d), lambda i, j: (i, j + first, 0)),
                  pl.BlockSpec((1, 1, 6, d), lambda i, j: (i, (j + first >= ctx_tiles).astype(jnp.int32), 0, 0)),
                  pl.BlockSpec((tm, 128), lambda i, j: (flat(i, j), 0)),
                  pl.BlockSpec((1, d), lambda i, j: (0, 0)),
                  pl.BlockSpec((tm, d), lambda i, j: (flat(i, j), 0)),
                  pl.BlockSpec((tm, d), lambda i, j: (flat(i, j) + k_stride, 0))],
        out_specs=pl.BlockSpec((1, tm, d), lambda i, j: (i, j, 0)),
        compiler_params=_params(("parallel", "parallel")),
        name="moe_combine",
    )(xs, mods, wts, final_g, y_rows, y_rows)


def moe_dispatch_plan(expert_ids):
    n = expert_ids.shape[0]
    flat_e = expert_ids.reshape(-1)
    n_assign = n * TOP_K
    iota = jnp.arange(n_assign, dtype=jnp.int32)
    _, order = lax.sort((flat_e, iota), num_keys=1, is_stable=True)
    one_hot = flat_e[:, None] == jnp.arange(N_EXPERTS, dtype=jnp.int32)[None, :]
    counts = jnp.sum(one_hot.astype(jnp.int32), axis=0)
    start = jnp.cumsum(counts) - counts
    padded = (counts + MOE_BLOCK - 1) // MOE_BLOCK * MOE_BLOCK
    pend = jnp.cumsum(padded)
    pstart = pend - padded
    cap = -(-n_assign // MOE_BLOCK) * MOE_BLOCK + N_EXPERTS * MOE_BLOCK
    n_blk = cap // MOE_BLOCK
    blk_first = jnp.arange(n_blk, dtype=jnp.int32) * MOE_BLOCK
    blk_e = jnp.minimum(jnp.sum((pend[None, :] <= blk_first[:, None]).astype(jnp.int32), axis=1), N_EXPERTS - 1)
    lane = jnp.arange(MOE_BLOCK, dtype=jnp.int32)[None, :]
    in_expert = (blk_first - pstart[blk_e])[:, None] + lane
    is_real = in_expert < counts[blk_e][:, None]
    assign = order[jnp.clip(start[blk_e][:, None] + in_expert, 0, n_assign - 1)]
    buf_tok = jnp.where(is_real, assign // TOP_K, 0)
    pad_row = n_assign + blk_first[:, None] + lane - (start + counts)[blk_e][:, None]
    buf_dst = jnp.where(is_real, (assign % TOP_K) * n + assign // TOP_K, pad_row)
    return blk_e, buf_tok.reshape(-1), buf_dst.reshape(-1)


def hier_moe_layer(xs, g2, mods, p, final_g, n_ctx, final_norm):
    d = xs.shape[2]
    w_router = jnp.zeros((d, 128), F32).at[:, :N_GROUPS].set(p['w_router_g']) \
        .at[:, N_GROUPS:N_GROUPS + N_EXPERTS].set(p['w_router_e'])
    b_router = jnp.zeros((1, 128), F32).at[0, :N_GROUPS].set(p['b_router_g']) \
        .at[0, N_GROUPS:N_GROUPS + N_EXPERTS].set(p['b_router_e'])
    h2, idx, wts = moe_router(xs, g2, mods, jnp.stack(_split_bf16(w_router)), b_router, n_ctx)
    blk_e, buf_tok, buf_dst = moe_dispatch_plan(idx[:, :TOP_K])
    y_rows = moe_experts(h2, blk_e, buf_tok, buf_dst, p['w_exp_gate'], p['w_exp_up'], p['w_exp_down'])
    return moe_combine(xs, mods, wts, y_rows, final_g, n_ctx, final_norm)


def _layer(xs, cvec, p, cos_t, sin_t, final_g, n_ctx, n_batch, final_norm):
    b, s, d = xs.shape
    m = adaln_mods(cvec, p['w_ada'], p['b_ada']).reshape(cvec.shape[0], 6, d)
    mods = jnp.stack([jnp.broadcast_to(m[n_batch], (n_batch, 6, d)), m[:n_batch]], axis=1)
    w_in = p['w_in']
    o1 = RWKV_IN
    o2 = o1 + GLA_IN
    o3 = o2 + ATTN_IN
    o4 = o3 + S5_IN
    hd = HEAD_DIM
    head_order = jnp.concatenate([jnp.arange(h * hd, (h + 1) * hd) for h in (0, 2, 1, 3)])
    w_att = jnp.concatenate([w_in[:, o2:o2 + ATTN_HEADS * hd][:, head_order], w_in[:, o2 + ATTN_HEADS * hd:o3]], axis=1)
    w_mix = jnp.concatenate([w_in[:, :o2], jnp.zeros((d, GLA_IN_PAD - GLA_IN), F32), w_att, w_in[:, o3:o4]],
                            axis=1).astype(BF16)
    w_gate = w_in[:, o4:].astype(BF16)
    w_branch = p['w_branch'].at[2].set(p['w_branch'][2][head_order]).astype(BF16)
    g1 = p['norm1_g'].reshape(1, d)
    ua, ub, uatt, us5 = in_proj(xs, g1, mods, w_mix, cos_t, sin_t, n_ctx)

    yaf, bonus, gate = rwkv_scan(ua, rwkv_direction_params(p, 0), n_ctx, False, True)
    yar, = rwkv_scan(ua, rwkv_direction_params(p, 1), n_ctx, True, False)
    gof = gla_scan(ub, *gla_direction_params(p, 0), n_ctx, False)
    gor = gla_scan(ub, *gla_direction_params(p, 1), n_ctx, True)
    sink = jnp.zeros((1, 128), F32).at[0, :ATTN_HEADS].set(p['attn_sink'])
    yc = window_attention(uatt, sink, n_ctx)
    ysf = s5_scan(us5, s5_direction_params(p, 0), n_ctx, False)
    ysr = s5_scan(us5, s5_direction_params(p, 1), n_ctx, True)

    row = lambda z: z.reshape(1, -1)
    branch_inputs = (yaf, yar, bonus, gate, row(p['rwkv_ln_g']), gof, gor, ub, row(p['gla_ln_g']),
                     yc, us5, ysf, ysr, row(p['s5_d']), p['s5_w_glu'].astype(BF16), row(p['s5_b_glu']))
    xs = merge_branches(xs, g1, mods, w_gate, w_branch, p['w_out'].astype(BF16),
                        branch_inputs, n_ctx)
    return hier_moe_layer(xs, p['norm2_g'].reshape(1, d), mods, p, final_g, n_ctx, final_norm)


_LAYER_PARAMS = ('norm1_g', 'norm2_g', 'w_ada', 'b_ada', 'w_in', 'rwkv_mu', 'rwkv_w0', 'rwkv_w1', 'rwkv_w2',
                 'rwkv_a0', 'rwkv_a1', 'rwkv_a2', 'rwkv_kk', 'rwkv_ka', 'rwkv_rk', 'rwkv_g1', 'rwkv_g2',
                 'rwkv_ln_g', 'gla_a2', 'gla_ab', 'gla_ln_g', 'attn_sink', 's5_lam_re', 's5_lam_im', 's5_log_dt',
                 's5_b_re', 's5_b_im', 's5_c_re', 's5_c_im', 's5_d', 's5_w_glu', 's5_b_glu', 'w_branch', 'w_out',
                 'w_router_g', 'b_router_g', 'w_router_e', 'b_router_e', 'w_exp_gate', 'w_exp_up', 'w_exp_down')


def kernel(x, c, ctx, c_ctx, norm1_g, norm2_g, final_norm_g, w_ada, b_ada, w_in, rwkv_mu, rwkv_w0, rwkv_w1, rwkv_w2, rwkv_a0, rwkv_a1, rwkv_a2, rwkv_kk, rwkv_ka, rwkv_rk, rwkv_g1, rwkv_g2, rwkv_ln_g, gla_a2, gla_ab, gla_ln_g, attn_sink, s5_lam_re, s5_lam_im, s5_log_dt, s5_b_re, s5_b_im, s5_c_re, s5_c_im, s5_d, s5_w_glu, s5_b_glu, w_branch, w_out, w_router_g, b_router_g, w_router_e, b_router_e, w_exp_gate, w_exp_up, w_exp_down):
    stacked = dict(zip(_LAYER_PARAMS, (
        norm1_g, norm2_g, w_ada, b_ada, w_in, rwkv_mu, rwkv_w0, rwkv_w1, rwkv_w2, rwkv_a0, rwkv_a1, rwkv_a2,
        rwkv_kk, rwkv_ka, rwkv_rk, rwkv_g1, rwkv_g2, rwkv_ln_g, gla_a2, gla_ab, gla_ln_g, attn_sink,
        s5_lam_re, s5_lam_im, s5_log_dt, s5_b_re, s5_b_im, s5_c_re, s5_c_im, s5_d, s5_w_glu, s5_b_glu,
        w_branch, w_out, w_router_g, b_router_g, w_router_e, b_router_e, w_exp_gate, w_exp_up, w_exp_down)))
    n_batch, n_lat, d = x.shape
    n_ctx = ctx.shape[1]
    depth = w_in.shape[0]
    xs = jnp.concatenate([ctx, x], axis=1)
    cvec = jnp.zeros((16, d), F32).at[:n_batch].set(c).at[n_batch].set(c_ctx)
    cos_t, sin_t = rope_tables(n_ctx, n_lat)
    final_g = final_norm_g.reshape(1, d)
    for l in range(depth):
        p = {k: v[l] for k, v in stacked.items()}
        xs = _layer(xs, cvec, p, cos_t, sin_t, final_g, n_ctx, n_batch, l == depth - 1)
    return xs
```

```python
import functools
import math

import jax
import jax.numpy as jnp
from jax import lax
from jax.experimental import pallas as pl
from jax.experimental.pallas import tpu as pltpu

F32 = jnp.float32
BF16 = jnp.bfloat16
HI = lax.Precision.HIGHEST

EPS = 1e-6
GN_EPS = 64e-5
ROPE_BASE = 10000.0
NEG_INF = -1e30
GRID_W = 64

MIX_W = 256
N_BRANCH = 4
HEAD_DIM = 64
RWKV_HEADS = MIX_W // HEAD_DIM
GLA_HEADS = 4
GLA_DK = 32
GLA_DV = 64
GLA_RANK = 16
GLA_TAU = 16.0
ATTN_HEADS = 4
ATTN_KV_HEADS = 2
WINDOW = 128
ATTN_BLOCK = 128
S5_GROUPS = 16
S5_GROUP_CH = MIX_W // S5_GROUPS
S5_STATE = 64
N_GROUPS = 4
EXPERTS_PER_GROUP = 8
N_EXPERTS = N_GROUPS * EXPERTS_PER_GROUP
TOP_K = 2
MOE_BLOCK = 128
ROW_TILE = 8

RWKV_IN = 4 * MIX_W
GLA_IN = 2 * GLA_HEADS * GLA_DK + 2 * GLA_HEADS * GLA_DV + 2 * GLA_RANK
GLA_IN_PAD = 896
ATTN_IN = (ATTN_HEADS + 2 * ATTN_KV_HEADS) * HEAD_DIM
S5_IN = MIX_W
MIX_IN_PAD = RWKV_IN + GLA_IN_PAD + ATTN_IN + S5_IN

CHUNK = 64
TOKEN_TILE = 256
VMEM_LIMIT = 56 * 1024 * 1024


def _params(sem):
    return pltpu.CompilerParams(dimension_semantics=sem, vmem_limit_bytes=VMEM_LIMIT)


def _sigmoid(z):
    return 1.0 / (1.0 + jnp.exp(-z))


def _dot(a, b, prec=None):
    return jnp.dot(a, b, precision=prec, preferred_element_type=F32)


def _dot_nt(a, b, prec=None):
    return lax.dot_general(a, b, (((1,), (1,)), ((), ())), precision=prec, preferred_element_type=F32)


def _dot_tn(a, b, prec=None):
    return lax.dot_general(a, b, (((0,), (0,)), ((), ())), precision=prec, preferred_element_type=F32)


def _split_bf16(x):
    hi = x.astype(BF16)
    return hi, (x - hi.astype(F32)).astype(BF16)


def _dot_split(x, m):
    hi, lo = _split_bf16(x)
    return _dot(hi, m) + _dot(lo, m)


def _dot_split_rhs(m, x):
    hi, lo = _split_bf16(x)
    return _dot(m, hi) + _dot(m, lo)


def _seg_ones(width, seg):
    r = lax.broadcasted_iota(jnp.int32, (width, width), 0) // seg
    c = lax.broadcasted_iota(jnp.int32, (width, width), 1) // seg
    return (r == c).astype(F32)


def _order_mask(n, reverse, inclusive):
    t = lax.broadcasted_iota(jnp.int32, (n, n), 0)
    i = lax.broadcasted_iota(jnp.int32, (n, n), 1)
    if reverse:
        return (i >= t) if inclusive else (i > t)
    return (i <= t) if inclusive else (i < t)


def _norm_mod(x, g, shift, scale):
    ms = jnp.mean(x * x, axis=-1, keepdims=True)
    return (x * lax.rsqrt(ms + EPS) * g) * (1.0 + scale) + shift


def _ada_kernel(c_ref, w_ref, b_ref, o_ref):
    cv = c_ref[...]
    o_ref[...] = _dot(cv * _sigmoid(cv), w_ref[...], HI) + b_ref[...]


def adaln_mods(cvec, w_ada, b_ada):
    rows, d = cvec.shape
    n = w_ada.shape[1]
    tn = 1024
    return pl.pallas_call(
        _ada_kernel,
        out_shape=jax.ShapeDtypeStruct((rows, n), F32),
        grid=(n // tn,),
        in_specs=[pl.BlockSpec((rows, d), lambda j: (0, 0)),
                  pl.BlockSpec((d, tn), lambda j: (0, j)),
                  pl.BlockSpec((1, tn), lambda j: (0, j))],
        out_specs=pl.BlockSpec((rows, tn), lambda j: (0, j)),
        compiler_params=_params(("parallel",)),
        name="adaln_mods",
    )(cvec, w_ada, b_ada.reshape(1, n))


def _rope(z, cos, sin):
    lane = lax.broadcasted_iota(jnp.int32, z.shape, 1)
    swapped = jnp.where((lane % HEAD_DIM) < HEAD_DIM // 2,
                        pltpu.roll(z, 128 - HEAD_DIM // 2, axis=1),
                        pltpu.roll(z, HEAD_DIM // 2, axis=1))
    return z * cos + swapped * sin


def _in_proj_kernel(x_ref, g_ref, mod_ref, w_ref, cos_ref, sin_ref, oa_ref, ob_ref, oc_ref, od_ref):
    h = _norm_mod(x_ref[0], g_ref[...], mod_ref[0, 0, 0:1, :], mod_ref[0, 0, 1:2, :])
    u = _dot(h.astype(BF16), w_ref[...])
    o1 = RWKV_IN
    o2 = o1 + GLA_IN_PAD
    o3 = o2 + ATTN_IN
    oa_ref[0] = u[:, :o1]
    ob_ref[0] = u[:, o1:o2]
    od_ref[0] = u[:, o3:]
    cos, sin = cos_ref[...], sin_ref[...]
    scale = HEAD_DIM ** -0.5
    qw = ATTN_HEADS * HEAD_DIM
    kw = ATTN_KV_HEADS * HEAD_DIM
    parts = [_rope(u[:, o2 + c:o2 + c + 128], cos, sin) * scale for c in range(0, qw, 128)]
    parts += [_rope(u[:, o2 + qw + c:o2 + qw + c + 128], cos, sin) for c in range(0, kw, 128)]
    parts.append(u[:, o2 + qw + kw:o3])
    oc_ref[0] = jnp.concatenate(parts, axis=1).astype(BF16)


def in_proj(xs, g, mods, w_mix, cos_t, sin_t, n_ctx):
    b, s, d = xs.shape
    tm = TOKEN_TILE
    ctx_tiles = n_ctx // tm
    widths = (RWKV_IN, GLA_IN_PAD, ATTN_IN, S5_IN)
    dtypes = (F32, F32, BF16, F32)
    return pl.pallas_call(
        _in_proj_kernel,
        out_shape=[jax.ShapeDtypeStruct((b, s, w), t) for w, t in zip(widths, dtypes)],
        grid=(b, s // tm),
        in_specs=[pl.BlockSpec((1, tm, d), lambda i, j: (i, j, 0)),
                  pl.BlockSpec((1, d), lambda i, j: (0, 0)),
                  pl.BlockSpec((1, 1, 6, d), lambda i, j: (i, (j >= ctx_tiles).astype(jnp.int32), 0, 0)),
                  pl.BlockSpec((d, MIX_IN_PAD), lambda i, j: (0, 0)),
                  pl.BlockSpec((tm, 128), lambda i, j: (j, 0)),
                  pl.BlockSpec((tm, 128), lambda i, j: (j, 0))],
        out_specs=[pl.BlockSpec((1, tm, w), lambda i, j: (i, j, 0)) for w in widths],
        compiler_params=_params(("parallel", "parallel")),
        name="in_proj",
    )(xs, g.reshape(1, d), mods, w_mix, cos_t, sin_t)


def _attn_kernel(own_ref, prev_ref, next_ref, ctx_ref, sink_ref, o_ref, *, ctx_blocks, n_lat):
    blk = ATTN_BLOCK
    qw = ATTN_HEADS * HEAD_DIM
    kw = ATTN_KV_HEADS * HEAD_DIM
    n = pl.program_id(1) - ctx_blocks
    own = own_ref[0]
    k_all = jnp.concatenate([prev_ref[0][:, qw:qw + kw], own[:, qw:qw + kw], next_ref[0][:, qw:qw + kw],
                             ctx_ref[0][:, qw:qw + kw]], axis=0)
    v_all = jnp.concatenate([prev_ref[0][:, qw + kw:], own[:, qw + kw:], next_ref[0][:, qw + kw:],
                             ctx_ref[0][:, qw + kw:]], axis=0)
    n_keys = k_all.shape[0]
    qpos = n * blk + lax.broadcasted_iota(jnp.int32, (2 * blk, n_keys), 0) % blk
    kcol = lax.broadcasted_iota(jnp.int32, (2 * blk, n_keys), 1)
    kpos = (n - 1) * blk + kcol
    valid = ((jnp.abs(qpos - kpos) <= WINDOW) & (kpos >= 0) & (kpos < n_lat) & (n >= 0)) | (kcol >= 3 * blk)
    low_half = lax.broadcasted_iota(jnp.int32, (blk, 128), 1) < HEAD_DIM
    top_rows = lax.broadcasted_iota(jnp.int32, (2 * blk, 1), 0) < blk
    zero = jnp.zeros((), BF16)
    outs = []
    for c in range(qw // 128):
        qc = own[:, c * 128:(c + 1) * 128]
        lhs = jnp.concatenate([jnp.where(low_half, qc, zero), jnp.where(low_half, zero, qc)], axis=0)
        s = jnp.where(valid, _dot_nt(lhs, k_all), NEG_INF)
        sink = jnp.where(top_rows, sink_ref[0:1, c:c + 1], sink_ref[0:1, c + 2:c + 3])
        m = jnp.maximum(jnp.max(s, axis=-1, keepdims=True), sink)
        p = jnp.exp(s - m)
        den = jnp.sum(p, axis=-1, keepdims=True) + jnp.exp(sink - m)
        o = _dot(p.astype(BF16), v_all) / den
        outs.append(jnp.where(low_half, o[:blk], o[blk:]))
    o_ref[0] = jnp.concatenate(outs, axis=1)


def window_attention(u_att, sink, n_ctx):
    b, s, w = u_att.shape
    blk = ATTN_BLOCK
    ctx_blocks = n_ctx // blk
    nblk = s // blk
    kern = functools.partial(_attn_kernel, ctx_blocks=ctx_blocks, n_lat=s - n_ctx)
    return pl.pallas_call(
        kern,
        out_shape=jax.ShapeDtypeStruct((b, s, ATTN_HEADS * HEAD_DIM), F32),
        grid=(b, nblk),
        in_specs=[pl.BlockSpec((1, blk, w), lambda i, j: (i, j, 0)),
                  pl.BlockSpec((1, blk, w), lambda i, j: (i, jnp.maximum(j - 1, 0), 0)),
                  pl.BlockSpec((1, blk, w), lambda i, j: (i, jnp.minimum(j + 1, nblk - 1), 0)),
                  pl.BlockSpec((1, n_ctx, w), lambda i, j: (i, 0, 0)),
                  pl.BlockSpec((1, 128), lambda i, j: (0, 0))],
        out_specs=pl.BlockSpec((1, blk, ATTN_HEADS * HEAD_DIM), lambda i, j: (i, j, 0)),
        compiler_params=_params(("parallel", "parallel")),
        name="window_attention",
    )(u_att, u_att, u_att, u_att, sink)


def rope_tables(n_ctx, n_lat):
    rows = n_lat // GRID_W
    row = jnp.repeat(jnp.arange(rows, dtype=F32), GRID_W)
    col = jnp.tile(jnp.arange(GRID_W, dtype=F32), rows)
    n_freq = HEAD_DIM // 4
    inv_freq = ROPE_BASE ** (-jnp.arange(n_freq, dtype=F32) / n_freq)
    ang = jnp.concatenate([row[:, None] * inv_freq, col[:, None] * inv_freq], axis=-1)
    cos, sin = jnp.cos(ang), jnp.sin(ang)
    cos_l = jnp.concatenate([cos, cos, cos, cos], axis=-1)
    sin_l = jnp.concatenate([-sin, sin, -sin, sin], axis=-1)
    cos_t = jnp.concatenate([jnp.ones((n_ctx, 128), F32), cos_l], axis=0)
    sin_t = jnp.concatenate([jnp.zeros((n_ctx, 128), F32), sin_l], axis=0)
    return cos_t, sin_t


def _scan_order(j, ctx_steps, n_steps, reverse):
    if not reverse:
        return j
    return jnp.where(j < ctx_steps, ctx_steps - 1 - j, n_steps - 1 - (j - ctx_steps))


S5_TILE = 128
S5_ROW_PAD = 8


def _s5_kernel(u_ref, bblk_ref, cblk_ref, ar_ref, ai_ref, y_ref, bu_ref, xs_ref, car_ref, cai_ref, *, reverse):
    nb, tm, _ = u_ref.shape
    ns = ar_ref.shape[1]
    stride = tm + S5_ROW_PAD

    @pl.when(pl.program_id(0) == 0)
    def _():
        car_ref[...] = jnp.zeros_like(car_ref)
        cai_ref[...] = jnp.zeros_like(cai_ref)

    nk = ns // 128
    for b in range(nb):
        bu = _dot(u_ref[b].astype(BF16), bblk_ref[...])
        for k in range(2 * nk):
            bu_ref[k, b * stride:b * stride + tm, :] = bu[:, k * 128:(k + 1) * 128]
    ar = [jnp.broadcast_to(ar_ref[:, k * 128:(k + 1) * 128], (nb, 128)) for k in range(nk)]
    ai = [jnp.broadcast_to(ai_ref[:, k * 128:(k + 1) * 128], (nb, 128)) for k in range(nk)]

    def step(i, carry):
        t = (tm - 1 - i) if reverse else i
        rows = pl.ds(t, nb, stride=stride)
        new = []
        for k in range(nk):
            xr, xi = carry[k], carry[nk + k]
            new.append((ar[k] * xr - ai[k] * xi + bu_ref[k, rows, :], ar[k] * xi + ai[k] * xr + bu_ref[nk + k, rows, :]))
        for k in range(nk):
            xs_ref[k, rows, :] = new[k][0]
            xs_ref[nk + k, rows, :] = new[k][1]
        return tuple(z[0] for z in new) + tuple(z[1] for z in new)

    init = tuple(car_ref[:, k * 128:(k + 1) * 128] for k in range(nk)) \
        + tuple(cai_ref[:, k * 128:(k + 1) * 128] for k in range(nk))
    final = lax.fori_loop(0, tm, step, init, unroll=4)
    for k in range(nk):
        car_ref[:, k * 128:(k + 1) * 128] = final[k]
        cai_ref[:, k * 128:(k + 1) * 128] = final[nk + k]
    for b in range(nb):
        xs = jnp.concatenate([xs_ref[k, b * stride:b * stride + tm, :] for k in range(2 * nk)], axis=1)
        y_ref[b] = _dot(xs.astype(BF16), cblk_ref[...])


def s5_scan(u, prm, n_ctx, reverse):
    b, s, w = u.shape
    tm = S5_TILE
    ns = S5_GROUPS * S5_STATE
    ctx_tiles, n_tiles = n_ctx // tm, s // tm
    bblk, cblk, ar, ai = prm

    def tile_map(j):
        return (0, _scan_order(j, ctx_tiles, n_tiles, reverse), 0)

    def const(j):
        return (0, 0)

    rows = b * (tm + S5_ROW_PAD)
    return pl.pallas_call(
        functools.partial(_s5_kernel, reverse=reverse),
        out_shape=jax.ShapeDtypeStruct((b, s, w), F32),
        grid=(n_tiles,),
        in_specs=[pl.BlockSpec((b, tm, w), tile_map),
                  pl.BlockSpec((w, 2 * ns), const), pl.BlockSpec((2 * ns, w), const),
                  pl.BlockSpec((1, ns), const), pl.BlockSpec((1, ns), const)],
        out_specs=pl.BlockSpec((b, tm, w), tile_map),
        scratch_shapes=[pltpu.VMEM((2 * ns // 128, rows, 128), F32), pltpu.VMEM((2 * ns // 128, rows, 128), F32),
                        pltpu.VMEM((b, ns), F32), pltpu.VMEM((b, ns), F32)],
        compiler_params=_params(("arbitrary",)),
        name="s5_scan_rev" if reverse else "s5_scan_fwd",
    )(u, bblk, cblk, ar, ai)


def s5_direction_params(p, d):
    g, ns, ch = S5_GROUPS, S5_STATE, S5_GROUP_CH
    lr, li = p['s5_lam_re'][d], p['s5_lam_im'][d]
    dt = jnp.exp(p['s5_log_dt'][d])[:, None]

    def power(n):
        mag = jnp.exp(n * lr * dt)
        return mag * jnp.cos(n * li * dt), mag * jnp.sin(n * li * dt)

    ar, ai = power(1.0)
    den = lr * lr + li * li
    cr = ((ar - 1.0) * lr + ai * li) / den
    ci = (ai * lr - (ar - 1.0) * li) / den
    bre, bim = p['s5_b_re'], p['s5_b_im']
    bbr = cr[..., None] * bre - ci[..., None] * bim
    bbi = cr[..., None] * bim + ci[..., None] * bre
    eye = jnp.eye(g, dtype=F32)

    def expand_in(z):
        return jnp.einsum('gpc,gh->gchp', z, eye).reshape(g * ch, g * ns)

    def expand_out(z):
        return jnp.einsum('gcp,gh->gphc', z, eye).reshape(g * ns, g * ch)

    bblk = jnp.concatenate([expand_in(bbr), expand_in(bbi)], axis=1).astype(BF16)
    cblk = jnp.concatenate([expand_out(p['s5_c_re'][d]), -expand_out(p['s5_c_im'][d])], axis=0).astype(BF16)
    return bblk, cblk, ar.reshape(1, g * ns), ai.reshape(1, g * ns)


def _log_sigmoid(z):
    return jnp.minimum(z, 0.0) - jnp.log(1.0 + jnp.exp(-jnp.abs(z)))


def _gla_kernel(u_ref, a2_ref, ab_ref, o_ref, st_ref, *, reverse):
    tm = u_ref.shape[1]
    qk = GLA_HEADS * GLA_DK
    vw = GLA_HEADS * GLA_DV

    @pl.when(pl.program_id(1) == 0)
    def _():
        st_ref[...] = jnp.zeros_like(st_ref)

    u = u_ref[0]
    alow = u[:, 2 * qk + 2 * vw:]
    log_a = _log_sigmoid(_dot(alow, a2_ref[...], HI) + ab_ref[...]) * (1.0 / GLA_TAU)
    ti = lax.broadcasted_iota(jnp.int32, (tm, tm), 0)
    tj = lax.broadcasted_iota(jnp.int32, (tm, tm), 1)
    same_chunk = (ti // CHUNK) == (tj // CHUNK)
    before = (tj >= ti) if reverse else (tj <= ti)
    cum = _dot_split_rhs((same_chunk & before).astype(BF16), log_a)
    tot = _dot_split_rhs(same_chunk.astype(BF16), log_a)
    q_dec = u[:, 0:qk] * (GLA_DK ** -0.5) * jnp.exp(cum)
    k = u[:, qk:2 * qk]
    k_inv = (k * jnp.exp(-cum)).astype(BF16)
    k_end = k * jnp.exp(tot - cum)
    decay = jnp.exp(tot)
    v_b = u[:, 2 * qk:2 * qk + vw].astype(BF16)
    head_of_lane = lax.broadcasted_iota(jnp.int32, (tm, qk), 1) // GLA_DK
    q_m = [jnp.where(head_of_lane == h, q_dec, 0.0).astype(BF16) for h in range(GLA_HEADS)]
    ke_m = [jnp.where(head_of_lane == h, k_end, 0.0).astype(BF16) for h in range(GLA_HEADS)]

    incl = _order_mask(CHUNK, reverse, True)
    n_chunks = tm // CHUNK
    units = [(c, h) for c in range(n_chunks) for h in range(GLA_HEADS)]

    def rows_of(c):
        return slice(c * CHUNK, (c + 1) * CHUNK)

    vh = {(c, h): v_b[rows_of(c), h * GLA_DV:(h + 1) * GLA_DV] for c, h in units}
    scores = {(c, h): jnp.where(incl, _dot_nt(q_m[h][rows_of(c)], k_inv[rows_of(c)]), 0.0).astype(BF16)
              for c, h in units}
    intra = {u_: _dot(scores[u_], vh[u_]) for u_ in units}
    vk = {(c, h): _dot_tn(vh[c, h], ke_m[h][rows_of(c)]) for c, h in units}

    heads = range(GLA_HEADS)
    order = range(n_chunks - 1, -1, -1) if reverse else range(n_chunks)
    for c in order:
        st = [st_ref[h] for h in heads]
        o_ref[0, rows_of(c), :] = jnp.concatenate(
            [intra[c, h] + _dot_nt(q_m[h][rows_of(c)], st[h].astype(BF16)) for h in heads], axis=1)
        for h in heads:
            st_ref[h] = st[h] * decay[c * CHUNK:c * CHUNK + 1, :] + vk[c, h]


def gla_scan(u, a2pad, ab, n_ctx, reverse):
    b, s, w = u.shape
    tm = TOKEN_TILE
    ctx_tiles, n_tiles = n_ctx // tm, s // tm
    qk = GLA_HEADS * GLA_DK

    def tile_map(i, j):
        return (i, _scan_order(j, ctx_tiles, n_tiles, reverse), 0)

    return pl.pallas_call(
        functools.partial(_gla_kernel, reverse=reverse),
        out_shape=jax.ShapeDtypeStruct((b, s, GLA_HEADS * GLA_DV), F32),
        grid=(b, n_tiles),
        in_specs=[pl.BlockSpec((1, tm, w), tile_map),
                  pl.BlockSpec((128, qk), lambda i, j: (0, 0)),
                  pl.BlockSpec((1, qk), lambda i, j: (0, 0))],
        out_specs=pl.BlockSpec((1, tm, GLA_HEADS * GLA_DV), tile_map),
        scratch_shapes=[pltpu.VMEM((GLA_HEADS, GLA_DV, GLA_HEADS * GLA_DK), F32)],
        compiler_params=_params(("parallel", "arbitrary")),
        name="gla_scan_rev" if reverse else "gla_scan_fwd",
    )(u, a2pad, ab)


def gla_direction_params(p, d):
    a2pad = jnp.zeros((128, GLA_HEADS * GLA_DK), F32).at[d * GLA_RANK:(d + 1) * GLA_RANK].set(p['gla_a2'][d])
    return a2pad, p['gla_ab'][d].reshape(1, -1)


def _rwkv_tile_terms(u_ref, up_ref, un_ref, pos, shared, wdir, d, reverse, ctx_tiles, n_tiles, aux_refs):
    mu_ref, kkp_ref, ka_ref, rk_ref, g1_ref, g2_ref = shared
    w0_ref, w1_ref, w2_ref, a0_ref, a1_ref, a2_ref = wdir
    tm = u_ref.shape[1]
    w = MIX_W
    first = (pos == 0) | (pos == ctx_tiles)
    last_tile = (pos == ctx_tiles - 1) | (pos == n_tiles - 1)
    prev_row = jnp.where(first, 0.0, up_ref[0, 7:8, :])
    next_row = jnp.where(last_tile, 0.0, un_ref[0, 0:1, :])
    u = u_ref[0]
    row = lax.broadcasted_iota(jnp.int32, u.shape, 0)
    down = jnp.where(row == 0, prev_row, pltpu.roll(u, 1, axis=0))
    up = jnp.where(row == tm - 1, next_row, pltpu.roll(u, tm - 1, axis=0))
    mixed = u + (0.5 * (down + up) - u) * mu_ref[...]
    r, k, v, xa = mixed[:, 0:w], mixed[:, w:2 * w], mixed[:, 2 * w:3 * w], mixed[:, 3 * w:4 * w]
    seg = _seg_ones(w, HEAD_DIM).astype(BF16)
    kk = k * kkp_ref[...]
    kk = kk * lax.rsqrt(_dot_split(kk * kk, seg) + EPS)
    xab = xa.astype(BF16)
    zw = w0_ref[d] + _dot(jnp.tanh(_dot(xab, w1_ref[d])).astype(BF16), w2_ref[d])
    log_w = -math.exp(-0.5) * _sigmoid(zw)
    iclr = _sigmoid(a0_ref[d] + _dot(_dot(xab, a1_ref[d]).astype(BF16), a2_ref[d]))
    k_eff = k * (1.0 + (iclr - 1.0) * ka_ref[...])
    a_vec = -kk
    b_vec = kk * iclr
    if aux_refs is not None:
        bonus_ref, gate_ref = aux_refs
        gate_ref[0] = _dot(_sigmoid(_dot(xab, g1_ref[...])).astype(BF16), g2_ref[...])
        bonus_ref[0] = _dot_split(r * k * rk_ref[...], seg) * v

    ti = lax.broadcasted_iota(jnp.int32, (tm, tm), 0)
    tj = lax.broadcasted_iota(jnp.int32, (tm, tm), 1)
    same_chunk = (ti // CHUNK) == (tj // CHUNK)
    before = (tj >= ti) if reverse else (tj <= ti)
    cum = _dot_split_rhs((same_chunk & before).astype(BF16), log_w)
    tot = _dot_split_rhs(same_chunk.astype(BF16), log_w)
    e_neg = jnp.exp(-cum)
    e_end = jnp.exp(tot - cum)
    lane = lax.broadcasted_iota(jnp.int32, (tm, w), 1)
    even = (lane // HEAD_DIM) % 2 == 0

    def by_parity(z):
        return jnp.where(even, z, 0.0).astype(BF16), jnp.where(even, 0.0, z).astype(BF16)

    return dict(a_par=by_parity(a_vec * jnp.exp(cum - log_w)), r_par=by_parity(r * jnp.exp(cum)),
                be_par=by_parity(b_vec * e_end), ke_par=by_parity(k_eff * e_end),
                b_b=(b_vec * e_neg).astype(BF16), k_b=(k_eff * e_neg).astype(BF16), v_b=v.astype(BF16),
                decay=jnp.exp(tot))


def _rwkv_rows(c):
    return slice(c * CHUNK, (c + 1) * CHUNK)


def _rwkv_lanes(h):
    return slice((h // 2) * 128, (h // 2 + 1) * 128)


def _rwkv_chunk_terms(terms, reverse, n_chunks):
    units = [(c, h) for c in range(n_chunks) for h in range(RWKV_HEADS)]
    t_i = lax.broadcasted_iota(jnp.int32, (CHUNK, CHUNK), 0)
    t_j = lax.broadcasted_iota(jnp.int32, (CHUNK, CHUNK), 1)
    incl_t = (t_i >= t_j) if reverse else (t_i <= t_j)
    strict_t = (t_i > t_j) if reverse else (t_i < t_j)
    eye = (t_i == t_j).astype(F32)
    n_doubling = int(math.log2(CHUNK)) - 1

    def sl(name, c, h):
        return terms[name][h % 2][_rwkv_rows(c), _rwkv_lanes(h)]

    lhs = {(c, h): jnp.concatenate([sl('a_par', c, h), sl('r_par', c, h)], axis=0) for c, h in units}
    rhs = {(c, h): jnp.concatenate([terms['b_b'][_rwkv_rows(c), _rwkv_lanes(h)],
                                    terms['k_b'][_rwkv_rows(c), _rwkv_lanes(h)]], axis=0) for c, h in units}
    m_t = {u_: _dot_nt(rhs[u_], lhs[u_]) for u_ in units}
    a_ab_t = {u_: jnp.where(strict_t, m_t[u_][:CHUNK, :CHUNK], 0.0) for u_ in units}
    a_rb_t = {u_: jnp.where(incl_t, m_t[u_][:CHUNK, CHUNK:], 0.0).astype(BF16) for u_ in units}
    a_ak_t = {u_: jnp.where(strict_t, m_t[u_][CHUNK:, :CHUNK], 0.0).astype(BF16) for u_ in units}
    a_rk_t = {u_: jnp.where(incl_t, m_t[u_][CHUNK:, CHUNK:], 0.0).astype(BF16) for u_ in units}
    inv = {u_: eye + a_ab_t[u_] for u_ in units}
    pw = a_ab_t
    for _ in range(n_doubling):
        pwb = {u_: pw[u_].astype(BF16) for u_ in units}
        pw = {u_: _dot(pwb[u_], pwb[u_]) for u_ in units}
        inv = {u_: inv[u_] + _dot(inv[u_].astype(BF16), pw[u_].astype(BF16)) for u_ in units}
    vh = {(c, h): terms['v_b'][_rwkv_rows(c), h * HEAD_DIM:(h + 1) * HEAD_DIM] for c, h in units}
    return dict(lhs=lhs, inv_t={u_: inv[u_].astype(BF16) for u_ in units}, a_rb_t=a_rb_t,
                akv_t={u_: _dot_tn(vh[u_], a_ak_t[u_]) for u_ in units},
                arkv_t={u_: _dot_tn(vh[u_], a_rk_t[u_]) for u_ in units},
                vk={(c, h): _dot_tn(vh[c, h], sl('ke_par', c, h)) for c, h in units},
                be={(c, h): sl('be_par', c, h) for c, h in units})


def _rwkv_kernel(uf_ref, ufp_ref, ufn_ref, ur_ref, urp_ref, urn_ref, mu_ref, kkp_ref, ka_ref, rk_ref, g1_ref, g2_ref,
                 w0_ref, w1_ref, w2_ref, a0_ref, a1_ref, a2_ref, yf_ref, yr_ref, bonus_ref, gate_ref, st_ref,
                 *, ctx_tiles, n_tiles):
    tm = uf_ref.shape[1]
    n_chunks = tm // CHUNK
    j = pl.program_id(1)

    @pl.when(j == 0)
    def _():
        st_ref[...] = jnp.zeros_like(st_ref)

    shared = (mu_ref, kkp_ref, ka_ref, rk_ref, g1_ref, g2_ref)
    wdir = (w0_ref, w1_ref, w2_ref, a0_ref, a1_ref, a2_ref)
    pos = [_scan_order(j, ctx_tiles, n_tiles, rev) for rev in (False, True)]
    terms = [_rwkv_tile_terms(uf_ref, ufp_ref, ufn_ref, pos[0], shared, wdir, 0, False, ctx_tiles, n_tiles,
                              (bonus_ref, gate_ref)),
             _rwkv_tile_terms(ur_ref, urp_ref, urn_ref, pos[1], shared, wdir, 1, True, ctx_tiles, n_tiles, None)]
    pre = [_rwkv_chunk_terms(terms[d], d == 1, n_chunks) for d in range(2)]
    y_refs = (yf_ref, yr_ref)

    chains = [(d, h) for d in range(2) for h in range(RWKV_HEADS)]
    for step in range(n_chunks):
        chunk = {0: step, 1: n_chunks - 1 - step}
        st = {(d, h): st_ref[d, h] for d, h in chains}
        x0_t = {(d, h): _dot_nt(st[d, h].astype(BF16), pre[d]['lhs'][chunk[d], h]) for d, h in chains}
        u_t = {(d, h): _dot((x0_t[d, h][:, :CHUNK] + pre[d]['akv_t'][chunk[d], h]).astype(BF16),
                            pre[d]['inv_t'][chunk[d], h]).astype(BF16) for d, h in chains}
        for d, h in chains:
            c = chunk[d]
            st_ref[d, h] = (st[d, h] * terms[d]['decay'][c * CHUNK:c * CHUNK + 1, _rwkv_lanes(h)]
                            + _dot(u_t[d, h], pre[d]['be'][c, h]) + pre[d]['vk'][c, h])
        for d in range(2):
            c = chunk[d]
            y_t = [x0_t[d, h][:, CHUNK:] + _dot(u_t[d, h], pre[d]['a_rb_t'][c, h]) + pre[d]['arkv_t'][c, h]
                   for h in range(RWKV_HEADS)]
            y_refs[d][0, _rwkv_rows(c), :] = jnp.concatenate([z.T for z in y_t], axis=1)


def rwkv_scan(u, prm, n_ctx):
    b, s, w4 = u.shape
    tm = TOKEN_TILE
    w = MIX_W
    ctx_tiles, n_tiles = n_ctx // tm, s // tm
    halo_blocks = tm // 8

    def maps(reverse):
        def tile_map(i, j):
            return (i, _scan_order(j, ctx_tiles, n_tiles, reverse), 0)

        def up_map(i, j):
            return (i, jnp.maximum(_scan_order(j, ctx_tiles, n_tiles, reverse) * halo_blocks - 1, 0), 0)

        def un_map(i, j):
            return (i, jnp.minimum((_scan_order(j, ctx_tiles, n_tiles, reverse) + 1) * halo_blocks, s // 8 - 1), 0)

        return tile_map, up_map, un_map

    def full(a):
        return pl.BlockSpec(a.shape, lambda i, j: (0,) * a.ndim)

    fwd, rev = maps(False), maps(True)
    in_specs = []
    for m in (fwd, rev):
        in_specs += [pl.BlockSpec((1, tm, w4), m[0]), pl.BlockSpec((1, 8, w4), m[1]), pl.BlockSpec((1, 8, w4), m[2])]
    return pl.pallas_call(
        functools.partial(_rwkv_kernel, ctx_tiles=ctx_tiles, n_tiles=n_tiles),
        out_shape=[jax.ShapeDtypeStruct((b, s, w), F32)] * 4,
        grid=(b, n_tiles),
        in_specs=in_specs + [full(a) for a in prm],
        out_specs=[pl.BlockSpec((1, tm, w), fwd[0]), pl.BlockSpec((1, tm, w), rev[0]),
                   pl.BlockSpec((1, tm, w), fwd[0]), pl.BlockSpec((1, tm, w), fwd[0])],
        scratch_shapes=[pltpu.VMEM((2, RWKV_HEADS, HEAD_DIM, 2 * HEAD_DIM), F32)],
        compiler_params=_params(("parallel", "arbitrary")),
        name="rwkv_scan",
    )(u, u, u, u, u, u, *prm)


def rwkv_params(p):
    row = lambda z: z.reshape(1, -1)
    rows2 = lambda z: z.reshape(2, 1, -1)
    bf = lambda z: z.astype(BF16)
    return (row(p['rwkv_mu']), row(p['rwkv_kk']), row(p['rwkv_ka']), row(p['rwkv_rk']),
            bf(p['rwkv_g1']), bf(p['rwkv_g2']),
            rows2(p['rwkv_w0']), bf(p['rwkv_w1']), bf(p['rwkv_w2']),
            rows2(p['rwkv_a0']), bf(p['rwkv_a1']), bf(p['rwkv_a2']))


def _gelu_tanh(y):
    return 0.5 * y * (1.0 + jnp.tanh(math.sqrt(2.0 / math.pi) * (y + 0.044715 * (y * y * y))))


def _merge_kernel(x_ref, g_ref, mod_ref, wgate_ref, wbr_ref, wout_ref,
                  yaf_ref, yar_ref, bonus_ref, gate_ref, lna_ref,
                  gof_ref, gor_ref, ub_ref, lnb_ref,
                  yc_ref, us5_ref, ysf_ref, ysr_ref, s5d_ref, wglu_ref, bglu_ref, o_ref):
    d = x_ref.shape[2]
    x = x_ref[0]
    h = _norm_mod(x, g_ref[...], mod_ref[0, 0, 0:1, :], mod_ref[0, 0, 1:2, :]).astype(BF16)
    seg = _seg_ones(MIX_W, HEAD_DIM) * (1.0 / HEAD_DIM)

    y = yaf_ref[0] + yar_ref[0]
    mu = _dot(y, seg, HI)
    yc0 = y - mu
    var = _dot(yc0 * yc0, seg, HI)
    ya = (yc0 * lax.rsqrt(var + GN_EPS) * lna_ref[...] + bonus_ref[0]) * gate_ref[0]

    o = gof_ref[0] + gor_ref[0]
    qk = GLA_HEADS * GLA_DK
    r = ub_ref[0][:, 2 * qk + MIX_W:2 * qk + 2 * MIX_W]
    yb = o * lax.rsqrt(_dot(o * o, seg, HI) + EPS) * lnb_ref[...] * (r * _sigmoid(r))

    yd = _gelu_tanh(s5d_ref[...] * us5_ref[0] + ysf_ref[0] + ysr_ref[0])
    yd = yd * _sigmoid(_dot(yd.astype(BF16), wglu_ref[...]) + bglu_ref[...])

    merged = None
    for i, yi in enumerate((ya, yb, yc_ref[0], yd)):
        gate_i = _sigmoid(_dot(h, wgate_ref[:, i * d:(i + 1) * d]))
        term = gate_i * _dot(yi.astype(BF16), wbr_ref[i])
        merged = term if merged is None else merged + term
    o_ref[0] = x + mod_ref[0, 0, 2:3, :] * _dot(merged.astype(BF16), wout_ref[...])


def merge_branches(xs, g, mods, w_gate, w_branch, w_out, branch_inputs, n_ctx):
    b, s, d = xs.shape
    tm = TOKEN_TILE
    ctx_tiles = n_ctx // tm
    (yaf, yar, bonus, gate, lna, gof, gor, ub, lnb, yc, us5, ysf, ysr, s5d, wglu, bglu) = branch_inputs

    def tok(a):
        return pl.BlockSpec((1, tm, a.shape[2]), lambda i, j: (i, j, 0))

    def full(a):
        return pl.BlockSpec(a.shape, lambda i, j: (0,) * a.ndim)

    mod_spec = pl.BlockSpec((1, 1, 6, d), lambda i, j: (i, (j >= ctx_tiles).astype(jnp.int32), 0, 0))
    return pl.pallas_call(
        _merge_kernel,
        out_shape=jax.ShapeDtypeStruct((b, s, d), F32),
        grid=(b, s // tm),
        in_specs=[tok(xs), full(g), mod_spec, full(w_gate), full(w_branch), full(w_out),
                  tok(yaf), tok(yar), tok(bonus), tok(gate), full(lna),
                  tok(gof), tok(gor), tok(ub), full(lnb),
                  tok(yc), tok(us5), tok(ysf), tok(ysr), full(s5d), full(wglu), full(bglu)],
        out_specs=tok(xs),
        compiler_params=_params(("parallel", "parallel")),
        name="merge_branches",
    )(xs, g, mods, w_gate, w_branch, w_out, yaf, yar, bonus, gate, lna, gof, gor, ub, lnb,
      yc, us5, ysf, ysr, s5d, wglu, bglu)


def _router_kernel(x_ref, g_ref, mod_ref, wr_ref, br_ref, h_ref, idx_ref, wt_ref):
    h = _norm_mod(x_ref[0], g_ref[...], mod_ref[0, 0, 3:4, :], mod_ref[0, 0, 4:5, :])
    for k in range(h.shape[1] // 128):
        h_ref[pl.ds(k, h.shape[0], stride=ROW_TILE), :] = h[:, k * 128:(k + 1) * 128]
    h_hi, h_lo = _split_bf16(h)
    logits = _dot(h_hi, wr_ref[0]) + (_dot(h_lo, wr_ref[0]) + _dot(h_hi, wr_ref[1])) + br_ref[...]
    lane = lax.broadcasted_iota(jnp.int32, logits.shape, 1)
    big = jnp.int32(1 << 20)

    def first_max(z):
        m = jnp.max(z, axis=-1, keepdims=True)
        return m, jnp.min(jnp.where(z == m, lane, big), axis=-1, keepdims=True)

    gl = jnp.where(lane < N_GROUPS, logits, -jnp.inf)
    gmax, gsel = first_max(gl)
    g_w = 1.0 / jnp.sum(jnp.exp(gl - gmax), axis=-1, keepdims=True)
    lo = N_GROUPS + EXPERTS_PER_GROUP * gsel
    el = jnp.where((lane >= lo) & (lane < lo + EXPERTS_PER_GROUP), logits, -jnp.inf)
    v1, i1 = first_max(el)
    v2, i2 = first_max(jnp.where(lane == i1, -jnp.inf, el))
    e2 = jnp.exp(v2 - v1)
    w1 = g_w / (1.0 + e2)
    w2 = g_w * e2 / (1.0 + e2)
    idx_ref[...] = jnp.where(lane == 0, i1 - N_GROUPS, jnp.where(lane == 1, i2 - N_GROUPS, 0))
    wt_ref[...] = jnp.where(lane == 0, w1, jnp.where(lane == 1, w2, 0.0))


def moe_router(xs, g, mods, w_router, b_router, n_ctx):
    b, s, d = xs.shape
    tm = TOKEN_TILE
    ctx_tiles, n_tiles = n_ctx // tm, s // tm
    n = b * s
    mod_spec = pl.BlockSpec((1, 1, 6, d), lambda i, j: (i, (j >= ctx_tiles).astype(jnp.int32), 0, 0))
    flat = lambda i, j: (i * n_tiles + j, 0)
    return pl.pallas_call(
        _router_kernel,
        out_shape=[jax.ShapeDtypeStruct((n * ROW_TILE, d // ROW_TILE), F32),
                   jax.ShapeDtypeStruct((n, 128), jnp.int32), jax.ShapeDtypeStruct((n, 128), F32)],
        grid=(b, n_tiles),
        in_specs=[pl.BlockSpec((1, tm, d), lambda i, j: (i, j, 0)),
                  pl.BlockSpec((1, d), lambda i, j: (0, 0)), mod_spec,
                  pl.BlockSpec((2, d, 128), lambda i, j: (0, 0, 0)),
                  pl.BlockSpec((1, 128), lambda i, j: (0, 0))],
        out_specs=[pl.BlockSpec((tm * ROW_TILE, d // ROW_TILE), flat), pl.BlockSpec((tm, 128), flat),
                   pl.BlockSpec((tm, 128), flat)],
        compiler_params=_params(("parallel", "parallel")),
        name="moe_router",
    )(xs, g, mods, w_router, b_router)


def _row_copy(src, src_row, dst, dst_row, sem):
    return pltpu.make_async_copy(src.at[pl.ds(pl.multiple_of(src_row, ROW_TILE), ROW_TILE)],
                                 dst.at[pl.ds(pl.multiple_of(dst_row, ROW_TILE), ROW_TILE)], sem)


def _load_rows(ref, n_rows):
    return jnp.concatenate([ref[pl.ds(k, n_rows, stride=ROW_TILE), :] for k in range(ROW_TILE)], axis=1)


def _expert_kernel(blk_e_ref, tok_ref, tok_next_ref, dst_ref, dst_prev_ref, dst_prev2_ref,
                   h_hbm, wg_ref, wu_ref, wd_ref, out_hbm, xbuf, ybuf, wgb, wub, wdb, gsem, ssem):
    i = pl.program_id(0)
    last = pl.num_programs(0) - 1
    slot = i % 2

    @pl.when((i == 0) | (blk_e_ref[i] != blk_e_ref[jnp.maximum(i - 1, 0)]))
    def _():
        wgb[...] = wg_ref[0].astype(BF16)
        wub[...] = wu_ref[0].astype(BF16)
        wdb[...] = wd_ref[0].astype(BF16)

    def gather(ids_ref, s, start, rows=range(MOE_BLOCK)):
        for r in rows:
            cp = _row_copy(h_hbm, ids_ref[0, 0, r], xbuf.at[s], r * ROW_TILE, gsem.at[s])
            if start:
                cp.start()
            else:
                cp.wait()

    def scatter(ids_ref, s, start, rows=range(MOE_BLOCK)):
        for r in rows:
            cp = _row_copy(ybuf.at[s], r * ROW_TILE, out_hbm, ids_ref[0, 0, r], ssem.at[s])
            if start:
                cp.start()
            else:
                cp.wait()

    @pl.when(i == 0)
    def _():
        gather(tok_ref, 0, True)

    gather(tok_ref, slot, False)
    xb = _load_rows(xbuf.at[slot], MOE_BLOCK).astype(BF16)
    n_slices = 4
    hw = wgb.shape[1] // n_slices
    per_slice = MOE_BLOCK // n_slices
    acc = None
    for k in range(n_slices):
        rows = range(k * per_slice, (k + 1) * per_slice)
        gather(tok_next_ref, 1 - slot, True, rows)

        @pl.when(i >= 1)
        def _():
            scatter(dst_prev_ref, 1 - slot, True, rows)

        gate = _dot(xb, wgb[:, k * hw:(k + 1) * hw])
        hid = gate * _sigmoid(gate) * _dot(xb, wub[:, k * hw:(k + 1) * hw])
        part = _dot(hid.astype(BF16), wdb[k * hw:(k + 1) * hw, :])
        acc = part if acc is None else acc + part

    @pl.when(i >= 2)
    def _():
        scatter(dst_prev2_ref, slot, False)

    for k in range(ROW_TILE):
        ybuf[slot, pl.ds(k, MOE_BLOCK, stride=ROW_TILE), :] = acc[:, k * 128:(k + 1) * 128]

    @pl.when(i == last)
    def _():
        gather(tok_next_ref, 1 - slot, False)
        scatter(dst_ref, slot, True)

        @pl.when(i >= 1)
        def _():
            scatter(dst_prev_ref, 1 - slot, False)

        scatter(dst_ref, slot, False)


def moe_experts(h2, blk_e, buf_tok, buf_dst, w_gate, w_up, w_down):
    d = w_gate.shape[1]
    n_blk = blk_e.shape[0]
    hid = w_gate.shape[2]
    tile_rows = MOE_BLOCK * ROW_TILE

    def ids(shift):
        return pl.BlockSpec((1, 1, MOE_BLOCK), lambda i, be: (jnp.clip(i + shift, 0, n_blk - 1), 0, 0),
                            memory_space=pltpu.SMEM)

    grid_spec = pltpu.PrefetchScalarGridSpec(
        num_scalar_prefetch=1,
        grid=(n_blk,),
        in_specs=[ids(0), ids(1), ids(0), ids(-1), ids(-2),
                  pl.BlockSpec(memory_space=pl.ANY),
                  pl.BlockSpec((1, d, hid), lambda i, be: (be[i], 0, 0)),
                  pl.BlockSpec((1, d, hid), lambda i, be: (be[i], 0, 0)),
                  pl.BlockSpec((1, hid, d), lambda i, be: (be[i], 0, 0))],
        out_specs=pl.BlockSpec(memory_space=pl.ANY),
        scratch_shapes=[pltpu.VMEM((2, tile_rows, d // ROW_TILE), F32), pltpu.VMEM((2, tile_rows, d // ROW_TILE), F32),
                        pltpu.VMEM((d, hid), BF16), pltpu.VMEM((d, hid), BF16), pltpu.VMEM((hid, d), BF16),
                        pltpu.SemaphoreType.DMA((2,)), pltpu.SemaphoreType.DMA((2,))],
    )
    tok = buf_tok.reshape(n_blk, 1, MOE_BLOCK)
    dst = buf_dst.reshape(n_blk, 1, MOE_BLOCK)
    return pl.pallas_call(
        _expert_kernel,
        out_shape=jax.ShapeDtypeStruct((n_blk * tile_rows, d // ROW_TILE), F32),
        grid_spec=grid_spec,
        compiler_params=_params(("arbitrary",)),
        name="moe_experts",
    )(blk_e, tok, tok, dst, dst, dst, h2, w_gate, w_up, w_down)


def _combine_kernel(x_ref, mod_ref, wt_ref, fg_ref, y0_ref, y1_ref, o_ref, *, final_norm):
    wt = wt_ref[...]
    tm = x_ref.shape[1]
    y = wt[:, 0:1] * _load_rows(y0_ref, tm) + wt[:, 1:2] * _load_rows(y1_ref, tm)
    out = x_ref[0] + mod_ref[0, 0, 5:6, :] * y
    if final_norm:
        out = out * lax.rsqrt(jnp.mean(out * out, axis=-1, keepdims=True) + EPS) * fg_ref[...]
    o_ref[0] = out


def moe_combine(xs, mods, wts, y_rows, final_g, n_ctx, final_norm):
    b, s, d = xs.shape
    tm = TOKEN_TILE
    ctx_tiles, n_tiles = n_ctx // tm, s // tm
    first = ctx_tiles if final_norm else 0
    per_batch = n_tiles - first
    k_stride = b * n_tiles

    def flat(i, j):
        return i * n_tiles + j + first

    return pl.pallas_call(
        functools.partial(_combine_kernel, final_norm=final_norm),
        out_shape=jax.ShapeDtypeStruct((b, per_batch * tm, d), F32),
        grid=(b, per_batch),
        in_specs=[pl.BlockSpec((1, tm, d), lambda i, j: (i, j + first, 0)),
                  pl.BlockSpec((1, 1, 6, d), lambda i, j: (i, (j + first >= ctx_tiles).astype(jnp.int32), 0, 0)),
                  pl.BlockSpec((tm, 128), lambda i, j: (flat(i, j), 0)),
                  pl.BlockSpec((1, d), lambda i, j: (0, 0)),
                  pl.BlockSpec((tm * ROW_TILE, d // ROW_TILE), lambda i, j: (flat(i, j), 0)),
                  pl.BlockSpec((tm * ROW_TILE, d // ROW_TILE), lambda i, j: (flat(i, j) + k_stride, 0))],
        out_specs=pl.BlockSpec((1, tm, d), lambda i, j: (i, j, 0)),
        compiler_params=_params(("parallel", "parallel")),
        name="moe_combine",
    )(xs, mods, wts, final_g, y_rows, y_rows)


def moe_dispatch_plan(expert_ids):
    n = expert_ids.shape[0]
    flat_e = expert_ids.reshape(-1)
    n_assign = n * TOP_K
    iota = jnp.arange(n_assign, dtype=jnp.int32)
    _, order = lax.sort((flat_e, iota), num_keys=1, is_stable=True)
    one_hot = flat_e[:, None] == jnp.arange(N_EXPERTS, dtype=jnp.int32)[None, :]
    counts = jnp.sum(one_hot.astype(jnp.int32), axis=0)
    start = jnp.cumsum(counts) - counts
    padded = (counts + MOE_BLOCK - 1) // MOE_BLOCK * MOE_BLOCK
    pend = jnp.cumsum(padded)
    pstart = pend - padded
    cap = -(-n_assign // MOE_BLOCK) * MOE_BLOCK + N_EXPERTS * MOE_BLOCK
    n_blk = cap // MOE_BLOCK
    blk_first = jnp.arange(n_blk, dtype=jnp.int32) * MOE_BLOCK
    blk_e = jnp.minimum(jnp.sum((pend[None, :] <= blk_first[:, None]).astype(jnp.int32), axis=1), N_EXPERTS - 1)
    lane = jnp.arange(MOE_BLOCK, dtype=jnp.int32)[None, :]
    in_expert = (blk_first - pstart[blk_e])[:, None] + lane
    is_real = in_expert < counts[blk_e][:, None]
    assign = order[jnp.clip(start[blk_e][:, None] + in_expert, 0, n_assign - 1)]
    buf_tok = jnp.where(is_real, assign // TOP_K, 0)
    pad_row = n_assign + blk_first[:, None] + lane - (start + counts)[blk_e][:, None]
    buf_dst = jnp.where(is_real, (assign % TOP_K) * n + assign // TOP_K, pad_row)
    return blk_e, buf_tok.reshape(-1) * ROW_TILE, buf_dst.reshape(-1) * ROW_TILE


def hier_moe_layer(xs, g2, mods, p, final_g, n_ctx, final_norm):
    d = xs.shape[2]
    w_router = jnp.zeros((d, 128), F32).at[:, :N_GROUPS].set(p['w_router_g']) \
        .at[:, N_GROUPS:N_GROUPS + N_EXPERTS].set(p['w_router_e'])
    b_router = jnp.zeros((1, 128), F32).at[0, :N_GROUPS].set(p['b_router_g']) \
        .at[0, N_GROUPS:N_GROUPS + N_EXPERTS].set(p['b_router_e'])
    h2, idx, wts = moe_router(xs, g2, mods, jnp.stack(_split_bf16(w_router)), b_router, n_ctx)
    blk_e, buf_tok, buf_dst = moe_dispatch_plan(idx[:, :TOP_K])
    y_rows = moe_experts(h2, blk_e, buf_tok, buf_dst, p['w_exp_gate'], p['w_exp_up'], p['w_exp_down'])
    return moe_combine(xs, mods, wts, y_rows, final_g, n_ctx, final_norm)


def _layer(xs, cvec, p, cos_t, sin_t, final_g, n_ctx, n_batch, final_norm):
    b, s, d = xs.shape
    m = adaln_mods(cvec, p['w_ada'], p['b_ada']).reshape(cvec.shape[0], 6, d)
    mods = jnp.stack([jnp.broadcast_to(m[n_batch], (n_batch, 6, d)), m[:n_batch]], axis=1)
    w_in = p['w_in']
    o1 = RWKV_IN
    o2 = o1 + GLA_IN
    o3 = o2 + ATTN_IN
    o4 = o3 + S5_IN
    hd = HEAD_DIM
    head_order = jnp.concatenate([jnp.arange(h * hd, (h + 1) * hd) for h in (0, 2, 1, 3)])
    w_att = jnp.concatenate([w_in[:, o2:o2 + ATTN_HEADS * hd][:, head_order], w_in[:, o2 + ATTN_HEADS * hd:o3]], axis=1)
    w_mix = jnp.concatenate([w_in[:, :o2], jnp.zeros((d, GLA_IN_PAD - GLA_IN), F32), w_att, w_in[:, o3:o4]],
                            axis=1).astype(BF16)
    w_gate = w_in[:, o4:].astype(BF16)
    w_branch = p['w_branch'].at[2].set(p['w_branch'][2][head_order]).astype(BF16)
    g1 = p['norm1_g'].reshape(1, d)
    ua, ub, uatt, us5 = in_proj(xs, g1, mods, w_mix, cos_t, sin_t, n_ctx)

    yaf, yar, bonus, gate = rwkv_scan(ua, rwkv_params(p), n_ctx)
    gof = gla_scan(ub, *gla_direction_params(p, 0), n_ctx, False)
    gor = gla_scan(ub, *gla_direction_params(p, 1), n_ctx, True)
    sink = jnp.zeros((1, 128), F32).at[0, :ATTN_HEADS].set(p['attn_sink'])
    yc = window_attention(uatt, sink, n_ctx)
    ysf = s5_scan(us5, s5_direction_params(p, 0), n_ctx, False)
    ysr = s5_scan(us5, s5_direction_params(p, 1), n_ctx, True)

    row = lambda z: z.reshape(1, -1)
    branch_inputs = (yaf, yar, bonus, gate, row(p['rwkv_ln_g']), gof, gor, ub, row(p['gla_ln_g']),
                     yc, us5, ysf, ysr, row(p['s5_d']), p['s5_w_glu'].astype(BF16), row(p['s5_b_glu']))
    xs = merge_branches(xs, g1, mods, w_gate, w_branch, p['w_out'].astype(BF16),
                        branch_inputs, n_ctx)
    return hier_moe_layer(xs, p['norm2_g'].reshape(1, d), mods, p, final_g, n_ctx, final_norm)


_LAYER_PARAMS = ('norm1_g', 'norm2_g', 'w_ada', 'b_ada', 'w_in', 'rwkv_mu', 'rwkv_w0', 'rwkv_w1', 'rwkv_w2',
                 'rwkv_a0', 'rwkv_a1', 'rwkv_a2', 'rwkv_kk', 'rwkv_ka', 'rwkv_rk', 'rwkv_g1', 'rwkv_g2',
                 'rwkv_ln_g', 'gla_a2', 'gla_ab', 'gla_ln_g', 'attn_sink', 's5_lam_re', 's5_lam_im', 's5_log_dt',
                 's5_b_re', 's5_b_im', 's5_c_re', 's5_c_im', 's5_d', 's5_w_glu', 's5_b_glu', 'w_branch', 'w_out',
                 'w_router_g', 'b_router_g', 'w_router_e', 'b_router_e', 'w_exp_gate', 'w_exp_up', 'w_exp_down')


def kernel(x, c, ctx, c_ctx, norm1_g, norm2_g, final_norm_g, w_ada, b_ada, w_in, rwkv_mu, rwkv_w0, rwkv_w1, rwkv_w2, rwkv_a0, rwkv_a1, rwkv_a2, rwkv_kk, rwkv_ka, rwkv_rk, rwkv_g1, rwkv_g2, rwkv_ln_g, gla_a2, gla_ab, gla_ln_g, attn_sink, s5_lam_re, s5_lam_im, s5_log_dt, s5_b_re, s5_b_im, s5_c_re, s5_c_im, s5_d, s5_w_glu, s5_b_glu, w_branch, w_out, w_router_g, b_router_g, w_router_e, b_router_e, w_exp_gate, w_exp_up, w_exp_down):
    stacked = dict(zip(_LAYER_PARAMS, (
        norm1_g, norm2_g, w_ada, b_ada, w_in, rwkv_mu, rwkv_w0, rwkv_w1, rwkv_w2, rwkv_a0, rwkv_a1, rwkv_a2,
        rwkv_kk, rwkv_ka, rwkv_rk, rwkv_g1, rwkv_g2, rwkv_ln_g, gla_a2, gla_ab, gla_ln_g, attn_sink,
        s5_lam_re, s5_lam_im, s5_log_dt, s5_b_re, s5_b_im, s5_c_re, s5_c_im, s5_d, s5_w_glu, s5_b_glu,
        w_branch, w_out, w_router_g, b_router_g, w_router_e, b_router_e, w_exp_gate, w_exp_up, w_exp_down)))
    n_batch, n_lat, d = x.shape
    n_ctx = ctx.shape[1]
    depth = w_in.shape[0]
    xs = jnp.concatenate([ctx, x], axis=1)
    cvec = jnp.zeros((16, d), F32).at[:n_batch].set(c).at[n_batch].set(c_ctx)
    cos_t, sin_t = rope_tables(n_ctx, n_lat)
    final_g = final_norm_g.reshape(1, d)
    for l in range(depth):
        p = {k: v[l] for k, v in stacked.items()}
        xs = _layer(xs, cvec, p, cos_t, sin_t, final_g, n_ctx, n_batch, l == depth - 1)
    return xs
```

```python
import functools
import math

import jax
import jax.numpy as jnp
from jax import lax
from jax.experimental import pallas as pl
from jax.experimental.pallas import tpu as pltpu

F32 = jnp.float32
BF16 = jnp.bfloat16
HI = lax.Precision.HIGHEST

EPS = 1e-6
GN_EPS = 64e-5
ROPE_BASE = 10000.0
NEG_INF = -1e30
GRID_W = 64

MIX_W = 256
N_BRANCH = 4
HEAD_DIM = 64
RWKV_HEADS = MIX_W // HEAD_DIM
GLA_HEADS = 4
GLA_DK = 32
GLA_DV = 64
GLA_RANK = 16
GLA_TAU = 16.0
ATTN_HEADS = 4
ATTN_KV_HEADS = 2
WINDOW = 128
ATTN_BLOCK = 128
S5_GROUPS = 16
S5_GROUP_CH = MIX_W // S5_GROUPS
S5_STATE = 64
N_GROUPS = 4
EXPERTS_PER_GROUP = 8
N_EXPERTS = N_GROUPS * EXPERTS_PER_GROUP
TOP_K = 2
MOE_BLOCK = 128
ROW_TILE = 8

RWKV_IN = 4 * MIX_W
GLA_IN = 2 * GLA_HEADS * GLA_DK + 2 * GLA_HEADS * GLA_DV + 2 * GLA_RANK
GLA_IN_PAD = 896
ATTN_IN = (ATTN_HEADS + 2 * ATTN_KV_HEADS) * HEAD_DIM
S5_IN = MIX_W
MIX_IN_PAD = RWKV_IN + GLA_IN_PAD + ATTN_IN + S5_IN

CHUNK = 64
TOKEN_TILE = 256
VMEM_LIMIT = 56 * 1024 * 1024


def _params(sem):
    return pltpu.CompilerParams(dimension_semantics=sem, vmem_limit_bytes=VMEM_LIMIT)


def _sigmoid(z):
    return 1.0 / (1.0 + jnp.exp(-z))


def _dot(a, b, prec=None):
    return jnp.dot(a, b, precision=prec, preferred_element_type=F32)


def _dot_nt(a, b, prec=None):
    return lax.dot_general(a, b, (((1,), (1,)), ((), ())), precision=prec, preferred_element_type=F32)


def _dot_tn(a, b, prec=None):
    return lax.dot_general(a, b, (((0,), (0,)), ((), ())), precision=prec, preferred_element_type=F32)


def _split_bf16(x):
    hi = x.astype(BF16)
    return hi, (x - hi.astype(F32)).astype(BF16)


def _dot_split(x, m):
    hi, lo = _split_bf16(x)
    return _dot(hi, m) + _dot(lo, m)


def _dot_split_rhs(m, x):
    hi, lo = _split_bf16(x)
    return _dot(m, hi) + _dot(m, lo)


def _seg_ones(width, seg):
    r = lax.broadcasted_iota(jnp.int32, (width, width), 0) // seg
    c = lax.broadcasted_iota(jnp.int32, (width, width), 1) // seg
    return (r == c).astype(F32)


def _order_mask(n, reverse, inclusive):
    t = lax.broadcasted_iota(jnp.int32, (n, n), 0)
    i = lax.broadcasted_iota(jnp.int32, (n, n), 1)
    if reverse:
        return (i >= t) if inclusive else (i > t)
    return (i <= t) if inclusive else (i < t)


def _norm_mod(x, g, shift, scale):
    ms = jnp.mean(x * x, axis=-1, keepdims=True)
    return (x * lax.rsqrt(ms + EPS) * g) * (1.0 + scale) + shift


def _ada_kernel(c_ref, w_ref, b_ref, o_ref):
    cv = c_ref[...]
    o_ref[...] = _dot(cv * _sigmoid(cv), w_ref[...], HI) + b_ref[...]


def adaln_mods(cvec, w_ada, b_ada):
    rows, d = cvec.shape
    n = w_ada.shape[1]
    tn = 1024
    return pl.pallas_call(
        _ada_kernel,
        out_shape=jax.ShapeDtypeStruct((rows, n), F32),
        grid=(n // tn,),
        in_specs=[pl.BlockSpec((rows, d), lambda j: (0, 0)),
                  pl.BlockSpec((d, tn), lambda j: (0, j)),
                  pl.BlockSpec((1, tn), lambda j: (0, j))],
        out_specs=pl.BlockSpec((rows, tn), lambda j: (0, j)),
        compiler_params=_params(("parallel",)),
        name="adaln_mods",
    )(cvec, w_ada, b_ada.reshape(1, n))


def _rope(z, cos, sin):
    lane = lax.broadcasted_iota(jnp.int32, z.shape, 1)
    swapped = jnp.where((lane % HEAD_DIM) < HEAD_DIM // 2,
                        pltpu.roll(z, 128 - HEAD_DIM // 2, axis=1),
                        pltpu.roll(z, HEAD_DIM // 2, axis=1))
    return z * cos + swapped * sin


def _in_proj_kernel(x_ref, g_ref, mod_ref, w_ref, cos_ref, sin_ref, oa_ref, ob_ref, oc_ref, od_ref):
    h = _norm_mod(x_ref[0], g_ref[...], mod_ref[0, 0, 0:1, :], mod_ref[0, 0, 1:2, :])
    u = _dot(h.astype(BF16), w_ref[...])
    o1 = RWKV_IN
    o2 = o1 + GLA_IN_PAD
    o3 = o2 + ATTN_IN
    oa_ref[0] = u[:, :o1]
    ob_ref[0] = u[:, o1:o2]
    od_ref[0] = u[:, o3:]
    cos, sin = cos_ref[...], sin_ref[...]
    scale = HEAD_DIM ** -0.5
    qw = ATTN_HEADS * HEAD_DIM
    kw = ATTN_KV_HEADS * HEAD_DIM
    parts = [_rope(u[:, o2 + c:o2 + c + 128], cos, sin) * scale for c in range(0, qw, 128)]
    parts += [_rope(u[:, o2 + qw + c:o2 + qw + c + 128], cos, sin) for c in range(0, kw, 128)]
    parts.append(u[:, o2 + qw + kw:o3])
    oc_ref[0] = jnp.concatenate(parts, axis=1).astype(BF16)


def in_proj(xs, g, mods, w_mix, cos_t, sin_t, n_ctx):
    b, s, d = xs.shape
    tm = TOKEN_TILE
    ctx_tiles = n_ctx // tm
    widths = (RWKV_IN, GLA_IN_PAD, ATTN_IN, S5_IN)
    dtypes = (F32, F32, BF16, F32)
    return pl.pallas_call(
        _in_proj_kernel,
        out_shape=[jax.ShapeDtypeStruct((b, s, w), t) for w, t in zip(widths, dtypes)],
        grid=(b, s // tm),
        in_specs=[pl.BlockSpec((1, tm, d), lambda i, j: (i, j, 0)),
                  pl.BlockSpec((1, d), lambda i, j: (0, 0)),
                  pl.BlockSpec((1, 1, 6, d), lambda i, j: (i, (j >= ctx_tiles).astype(jnp.int32), 0, 0)),
                  pl.BlockSpec((d, MIX_IN_PAD), lambda i, j: (0, 0)),
                  pl.BlockSpec((tm, 128), lambda i, j: (j, 0)),
                  pl.BlockSpec((tm, 128), lambda i, j: (j, 0))],
        out_specs=[pl.BlockSpec((1, tm, w), lambda i, j: (i, j, 0)) for w in widths],
        compiler_params=_params(("parallel", "parallel")),
        name="in_proj",
    )(xs, g.reshape(1, d), mods, w_mix, cos_t, sin_t)


def _attn_kernel(own_ref, prev_ref, next_ref, ctx_ref, sink_ref, band_ref, o_ref, *, ctx_blocks, n_lat):
    blk = ATTN_BLOCK
    qw = ATTN_HEADS * HEAD_DIM
    kw = ATTN_KV_HEADS * HEAD_DIM
    n = pl.program_id(1) - ctx_blocks
    own = own_ref[0]
    k_all = jnp.concatenate([prev_ref[0][:, qw:qw + kw], own[:, qw:qw + kw], next_ref[0][:, qw:qw + kw],
                             ctx_ref[0][:, qw:qw + kw]], axis=0)
    v_all = jnp.concatenate([prev_ref[0][:, qw + kw:], own[:, qw + kw:], next_ref[0][:, qw + kw:],
                             ctx_ref[0][:, qw + kw:]], axis=0)
    n_keys = k_all.shape[0]
    key_blk = lax.broadcasted_iota(jnp.int32, (1, n_keys), 1) // blk
    off = [jnp.where(ok, 0.0, NEG_INF) for ok in (n >= 1, n >= 0, (n >= 0) & ((n + 1) * blk < n_lat))]
    bias = band_ref[...] + jnp.where(key_blk == 0, off[0],
                                     jnp.where(key_blk == 1, off[1], jnp.where(key_blk == 2, off[2], 0.0)))
    low_half = lax.broadcasted_iota(jnp.int32, (blk, 128), 1) < HEAD_DIM
    top_rows = lax.broadcasted_iota(jnp.int32, (2 * blk, 1), 0) < blk
    zero = jnp.zeros((), BF16)
    outs = []
    for c in range(qw // 128):
        qc = own[:, c * 128:(c + 1) * 128]
        lhs = jnp.concatenate([jnp.where(low_half, qc, zero), jnp.where(low_half, zero, qc)], axis=0)
        s = _dot_nt(lhs, k_all) + bias
        sink = jnp.where(top_rows, sink_ref[0:1, c:c + 1], sink_ref[0:1, c + 2:c + 3])
        m = jnp.maximum(jnp.max(s, axis=-1, keepdims=True), sink)
        p = jnp.exp(s - m)
        den = jnp.sum(p, axis=-1, keepdims=True) + jnp.exp(sink - m)
        o = _dot(p.astype(BF16), v_all) / den
        outs.append(jnp.where(low_half, o[:blk], o[blk:]))
    o_ref[0] = jnp.concatenate(outs, axis=1)


def window_attention(u_att, sink, n_ctx):
    b, s, w = u_att.shape
    blk = ATTN_BLOCK
    ctx_blocks = n_ctx // blk
    nblk = s // blk
    kern = functools.partial(_attn_kernel, ctx_blocks=ctx_blocks, n_lat=s - n_ctx)
    q_in = jnp.arange(2 * blk)[:, None] % blk
    k_col = jnp.arange(3 * blk + n_ctx)[None, :]
    band = jnp.where((k_col >= 3 * blk) | (jnp.abs(q_in - (k_col - blk)) <= WINDOW), 0.0, NEG_INF).astype(F32)
    return pl.pallas_call(
        kern,
        out_shape=jax.ShapeDtypeStruct((b, s, ATTN_HEADS * HEAD_DIM), F32),
        grid=(b, nblk),
        in_specs=[pl.BlockSpec((1, blk, w), lambda i, j: (i, j, 0)),
                  pl.BlockSpec((1, blk, w), lambda i, j: (i, jnp.maximum(j - 1, 0), 0)),
                  pl.BlockSpec((1, blk, w), lambda i, j: (i, jnp.minimum(j + 1, nblk - 1), 0)),
                  pl.BlockSpec((1, n_ctx, w), lambda i, j: (i, 0, 0)),
                  pl.BlockSpec((1, 128), lambda i, j: (0, 0)),
                  pl.BlockSpec(band.shape, lambda i, j: (0, 0))],
        out_specs=pl.BlockSpec((1, blk, ATTN_HEADS * HEAD_DIM), lambda i, j: (i, j, 0)),
        compiler_params=_params(("parallel", "parallel")),
        name="window_attention",
    )(u_att, u_att, u_att, u_att, sink, band)


def rope_tables(n_ctx, n_lat):
    rows = n_lat // GRID_W
    row = jnp.repeat(jnp.arange(rows, dtype=F32), GRID_W)
    col = jnp.tile(jnp.arange(GRID_W, dtype=F32), rows)
    n_freq = HEAD_DIM // 4
    inv_freq = ROPE_BASE ** (-jnp.arange(n_freq, dtype=F32) / n_freq)
    ang = jnp.concatenate([row[:, None] * inv_freq, col[:, None] * inv_freq], axis=-1)
    cos, sin = jnp.cos(ang), jnp.sin(ang)
    cos_l = jnp.concatenate([cos, cos, cos, cos], axis=-1)
    sin_l = jnp.concatenate([-sin, sin, -sin, sin], axis=-1)
    cos_t = jnp.concatenate([jnp.ones((n_ctx, 128), F32), cos_l], axis=0)
    sin_t = jnp.concatenate([jnp.zeros((n_ctx, 128), F32), sin_l], axis=0)
    return cos_t, sin_t


def _scan_order(j, ctx_steps, n_steps, reverse):
    if not reverse:
        return j
    return jnp.where(j < ctx_steps, ctx_steps - 1 - j, n_steps - 1 - (j - ctx_steps))


S5_TILE = 128
S5_ROW_PAD = 8


def _s5_kernel(u_ref, bblk_ref, cblk_ref, ar_ref, ai_ref, y_ref, bu_ref, xs_ref, car_ref, cai_ref, *, reverse):
    nb, tm, _ = u_ref.shape
    ns = ar_ref.shape[1]
    stride = tm + S5_ROW_PAD

    @pl.when(pl.program_id(0) == 0)
    def _():
        car_ref[...] = jnp.zeros_like(car_ref)
        cai_ref[...] = jnp.zeros_like(cai_ref)

    nk = ns // 128
    for b in range(nb):
        bu = _dot(u_ref[b].astype(BF16), bblk_ref[...])
        for k in range(2 * nk):
            bu_ref[k, b * stride:b * stride + tm, :] = bu[:, k * 128:(k + 1) * 128]
    ar = [jnp.broadcast_to(ar_ref[:, k * 128:(k + 1) * 128], (nb, 128)) for k in range(nk)]
    ai = [jnp.broadcast_to(ai_ref[:, k * 128:(k + 1) * 128], (nb, 128)) for k in range(nk)]

    def step(i, carry):
        t = (tm - 1 - i) if reverse else i
        rows = pl.ds(t, nb, stride=stride)
        new = []
        for k in range(nk):
            xr, xi = carry[k], carry[nk + k]
            new.append((ar[k] * xr - ai[k] * xi + bu_ref[k, rows, :], ar[k] * xi + ai[k] * xr + bu_ref[nk + k, rows, :]))
        for k in range(nk):
            xs_ref[k, rows, :] = new[k][0]
            xs_ref[nk + k, rows, :] = new[k][1]
        return tuple(z[0] for z in new) + tuple(z[1] for z in new)

    init = tuple(car_ref[:, k * 128:(k + 1) * 128] for k in range(nk)) \
        + tuple(cai_ref[:, k * 128:(k + 1) * 128] for k in range(nk))
    final = lax.fori_loop(0, tm, step, init, unroll=4)
    for k in range(nk):
        car_ref[:, k * 128:(k + 1) * 128] = final[k]
        cai_ref[:, k * 128:(k + 1) * 128] = final[nk + k]
    for b in range(nb):
        xs = jnp.concatenate([xs_ref[k, b * stride:b * stride + tm, :] for k in range(2 * nk)], axis=1)
        y_ref[b] = _dot(xs.astype(BF16), cblk_ref[...])


def s5_scan(u, prm, n_ctx, reverse):
    b, s, w = u.shape
    tm = S5_TILE
    ns = S5_GROUPS * S5_STATE
    ctx_tiles, n_tiles = n_ctx // tm, s // tm
    bblk, cblk, ar, ai = prm

    def tile_map(j):
        return (0, _scan_order(j, ctx_tiles, n_tiles, reverse), 0)

    def const(j):
        return (0, 0)

    rows = b * (tm + S5_ROW_PAD)
    return pl.pallas_call(
        functools.partial(_s5_kernel, reverse=reverse),
        out_shape=jax.ShapeDtypeStruct((b, s, w), F32),
        grid=(n_tiles,),
        in_specs=[pl.BlockSpec((b, tm, w), tile_map),
                  pl.BlockSpec((w, 2 * ns), const), pl.BlockSpec((2 * ns, w), const),
                  pl.BlockSpec((1, ns), const), pl.BlockSpec((1, ns), const)],
        out_specs=pl.BlockSpec((b, tm, w), tile_map),
        scratch_shapes=[pltpu.VMEM((2 * ns // 128, rows, 128), F32), pltpu.VMEM((2 * ns // 128, rows, 128), F32),
                        pltpu.VMEM((b, ns), F32), pltpu.VMEM((b, ns), F32)],
        compiler_params=_params(("arbitrary",)),
        name="s5_scan_rev" if reverse else "s5_scan_fwd",
    )(u, bblk, cblk, ar, ai)


def s5_direction_params(p, d):
    g, ns, ch = S5_GROUPS, S5_STATE, S5_GROUP_CH
    lr, li = p['s5_lam_re'][d], p['s5_lam_im'][d]
    dt = jnp.exp(p['s5_log_dt'][d])[:, None]

    def power(n):
        mag = jnp.exp(n * lr * dt)
        return mag * jnp.cos(n * li * dt), mag * jnp.sin(n * li * dt)

    ar, ai = power(1.0)
    den = lr * lr + li * li
    cr = ((ar - 1.0) * lr + ai * li) / den
    ci = (ai * lr - (ar - 1.0) * li) / den
    bre, bim = p['s5_b_re'], p['s5_b_im']
    bbr = cr[..., None] * bre - ci[..., None] * bim
    bbi = cr[..., None] * bim + ci[..., None] * bre
    eye = jnp.eye(g, dtype=F32)

    def expand_in(z):
        return jnp.einsum('gpc,gh->gchp', z, eye).reshape(g * ch, g * ns)

    def expand_out(z):
        return jnp.einsum('gcp,gh->gphc', z, eye).reshape(g * ns, g * ch)

    bblk = jnp.concatenate([expand_in(bbr), expand_in(bbi)], axis=1).astype(BF16)
    cblk = jnp.concatenate([expand_out(p['s5_c_re'][d]), -expand_out(p['s5_c_im'][d])], axis=0).astype(BF16)
    return bblk, cblk, ar.reshape(1, g * ns), ai.reshape(1, g * ns)


def _log_sigmoid(z):
    return jnp.minimum(z, 0.0) - jnp.log(1.0 + jnp.exp(-jnp.abs(z)))


def _gla_kernel(u_ref, a2_ref, ab_ref, o_ref, st_ref, *, reverse):
    tm = u_ref.shape[1]
    qk = GLA_HEADS * GLA_DK
    vw = GLA_HEADS * GLA_DV

    @pl.when(pl.program_id(1) == 0)
    def _():
        st_ref[...] = jnp.zeros_like(st_ref)

    u = u_ref[0]
    alow = u[:, 2 * qk + 2 * vw:]
    log_a = _log_sigmoid(_dot(alow, a2_ref[...], HI) + ab_ref[...]) * (1.0 / GLA_TAU)
    ti = lax.broadcasted_iota(jnp.int32, (tm, tm), 0)
    tj = lax.broadcasted_iota(jnp.int32, (tm, tm), 1)
    same_chunk = (ti // CHUNK) == (tj // CHUNK)
    before = (tj >= ti) if reverse else (tj <= ti)
    cum = _dot_split_rhs((same_chunk & before).astype(BF16), log_a)
    tot = _dot_split_rhs(same_chunk.astype(BF16), log_a)
    q_dec = u[:, 0:qk] * (GLA_DK ** -0.5) * jnp.exp(cum)
    k = u[:, qk:2 * qk]
    k_inv = (k * jnp.exp(-cum)).astype(BF16)
    k_end = k * jnp.exp(tot - cum)
    decay = jnp.exp(tot)
    v_b = u[:, 2 * qk:2 * qk + vw].astype(BF16)
    head_of_lane = lax.broadcasted_iota(jnp.int32, (tm, qk), 1) // GLA_DK
    q_m = [jnp.where(head_of_lane == h, q_dec, 0.0).astype(BF16) for h in range(GLA_HEADS)]
    ke_m = [jnp.where(head_of_lane == h, k_end, 0.0).astype(BF16) for h in range(GLA_HEADS)]

    incl = _order_mask(CHUNK, reverse, True)
    n_chunks = tm // CHUNK
    units = [(c, h) for c in range(n_chunks) for h in range(GLA_HEADS)]

    def rows_of(c):
        return slice(c * CHUNK, (c + 1) * CHUNK)

    vh = {(c, h): v_b[rows_of(c), h * GLA_DV:(h + 1) * GLA_DV] for c, h in units}
    scores = {(c, h): jnp.where(incl, _dot_nt(q_m[h][rows_of(c)], k_inv[rows_of(c)]), 0.0).astype(BF16)
              for c, h in units}
    intra = {u_: _dot(scores[u_], vh[u_]) for u_ in units}
    vk = {(c, h): _dot_tn(vh[c, h], ke_m[h][rows_of(c)]) for c, h in units}

    heads = range(GLA_HEADS)
    order = range(n_chunks - 1, -1, -1) if reverse else range(n_chunks)
    for c in order:
        st = [st_ref[h] for h in heads]
        o_ref[0, rows_of(c), :] = jnp.concatenate(
            [intra[c, h] + _dot_nt(q_m[h][rows_of(c)], st[h].astype(BF16)) for h in heads], axis=1)
        for h in heads:
            st_ref[h] = st[h] * decay[c * CHUNK:c * CHUNK + 1, :] + vk[c, h]


def gla_scan(u, a2pad, ab, n_ctx, reverse):
    b, s, w = u.shape
    tm = TOKEN_TILE
    ctx_tiles, n_tiles = n_ctx // tm, s // tm
    qk = GLA_HEADS * GLA_DK

    def tile_map(i, j):
        return (i, _scan_order(j, ctx_tiles, n_tiles, reverse), 0)

    return pl.pallas_call(
        functools.partial(_gla_kernel, reverse=reverse),
        out_shape=jax.ShapeDtypeStruct((b, s, GLA_HEADS * GLA_DV), F32),
        grid=(b, n_tiles),
        in_specs=[pl.BlockSpec((1, tm, w), tile_map),
                  pl.BlockSpec((128, qk), lambda i, j: (0, 0)),
                  pl.BlockSpec((1, qk), lambda i, j: (0, 0))],
        out_specs=pl.BlockSpec((1, tm, GLA_HEADS * GLA_DV), tile_map),
        scratch_shapes=[pltpu.VMEM((GLA_HEADS, GLA_DV, GLA_HEADS * GLA_DK), F32)],
        compiler_params=_params(("parallel", "arbitrary")),
        name="gla_scan_rev" if reverse else "gla_scan_fwd",
    )(u, a2pad, ab)


def gla_direction_params(p, d):
    a2pad = jnp.zeros((128, GLA_HEADS * GLA_DK), F32).at[d * GLA_RANK:(d + 1) * GLA_RANK].set(p['gla_a2'][d])
    return a2pad, p['gla_ab'][d].reshape(1, -1)


def _rwkv_tile_terms(u_ref, up_ref, un_ref, pos, shared, wdir, d, reverse, ctx_tiles, n_tiles, aux_refs):
    mu_ref, kkp_ref, ka_ref, rk_ref, g1_ref, g2_ref = shared
    w0_ref, w1_ref, w2_ref, a0_ref, a1_ref, a2_ref = wdir
    tm = u_ref.shape[1]
    w = MIX_W
    first = (pos == 0) | (pos == ctx_tiles)
    last_tile = (pos == ctx_tiles - 1) | (pos == n_tiles - 1)
    prev_row = jnp.where(first, 0.0, up_ref[0, 7:8, :])
    next_row = jnp.where(last_tile, 0.0, un_ref[0, 0:1, :])
    u = u_ref[0]
    row = lax.broadcasted_iota(jnp.int32, u.shape, 0)
    down = jnp.where(row == 0, prev_row, pltpu.roll(u, 1, axis=0))
    up = jnp.where(row == tm - 1, next_row, pltpu.roll(u, tm - 1, axis=0))
    mixed = u + (0.5 * (down + up) - u) * mu_ref[...]
    r, k, v, xa = mixed[:, 0:w], mixed[:, w:2 * w], mixed[:, 2 * w:3 * w], mixed[:, 3 * w:4 * w]
    seg = _seg_ones(w, HEAD_DIM).astype(BF16)
    kk = k * kkp_ref[...]
    kk = kk * lax.rsqrt(_dot_split(kk * kk, seg) + EPS)
    xab = xa.astype(BF16)
    zw = w0_ref[d] + _dot(jnp.tanh(_dot(xab, w1_ref[d])).astype(BF16), w2_ref[d])
    log_w = -math.exp(-0.5) * _sigmoid(zw)
    iclr = _sigmoid(a0_ref[d] + _dot(_dot(xab, a1_ref[d]).astype(BF16), a2_ref[d]))
    k_eff = k * (1.0 + (iclr - 1.0) * ka_ref[...])
    a_vec = -kk
    b_vec = kk * iclr
    if aux_refs is not None:
        bonus_ref, gate_ref = aux_refs
        gate_ref[0] = _dot(_sigmoid(_dot(xab, g1_ref[...])).astype(BF16), g2_ref[...])
        bonus_ref[0] = _dot_split(r * k * rk_ref[...], seg) * v

    ti = lax.broadcasted_iota(jnp.int32, (tm, tm), 0)
    tj = lax.broadcasted_iota(jnp.int32, (tm, tm), 1)
    same_chunk = (ti // CHUNK) == (tj // CHUNK)
    before = (tj >= ti) if reverse else (tj <= ti)
    cum = _dot_split_rhs((same_chunk & before).astype(BF16), log_w)
    tot = _dot_split_rhs(same_chunk.astype(BF16), log_w)
    e_neg = jnp.exp(-cum)
    e_end = jnp.exp(tot - cum)
    lane = lax.broadcasted_iota(jnp.int32, (tm, w), 1)
    even = (lane // HEAD_DIM) % 2 == 0

    def by_parity(z):
        return jnp.where(even, z, 0.0).astype(BF16), jnp.where(even, 0.0, z).astype(BF16)

    return dict(a_par=by_parity(a_vec * jnp.exp(cum - log_w)), r_par=by_parity(r * jnp.exp(cum)),
                be_par=by_parity(b_vec * e_end), ke_par=by_parity(k_eff * e_end),
                b_b=(b_vec * e_neg).astype(BF16), k_b=(k_eff * e_neg).astype(BF16), v_b=v.astype(BF16),
                decay=jnp.exp(tot))


def _rwkv_rows(c):
    return slice(c * CHUNK, (c + 1) * CHUNK)


def _rwkv_lanes(h):
    return slice((h // 2) * 128, (h // 2 + 1) * 128)


def _rwkv_chunk_terms(terms, reverse, n_chunks):
    units = [(c, h) for c in range(n_chunks) for h in range(RWKV_HEADS)]
    t_i = lax.broadcasted_iota(jnp.int32, (CHUNK, CHUNK), 0)
    t_j = lax.broadcasted_iota(jnp.int32, (CHUNK, CHUNK), 1)
    incl_t = (t_i >= t_j) if reverse else (t_i <= t_j)
    strict_t = (t_i > t_j) if reverse else (t_i < t_j)
    eye = (t_i == t_j).astype(F32)
    n_doubling = int(math.log2(CHUNK)) - 1

    def sl(name, c, h):
        return terms[name][h % 2][_rwkv_rows(c), _rwkv_lanes(h)]

    lhs = {(c, h): jnp.concatenate([sl('a_par', c, h), sl('r_par', c, h)], axis=0) for c, h in units}
    rhs = {(c, h): jnp.concatenate([terms['b_b'][_rwkv_rows(c), _rwkv_lanes(h)],
                                    terms['k_b'][_rwkv_rows(c), _rwkv_lanes(h)]], axis=0) for c, h in units}
    m_t = {u_: _dot_nt(rhs[u_], lhs[u_]) for u_ in units}
    a_ab_t = {u_: jnp.where(strict_t, m_t[u_][:CHUNK, :CHUNK], 0.0) for u_ in units}
    a_rb_t = {u_: jnp.where(incl_t, m_t[u_][:CHUNK, CHUNK:], 0.0).astype(BF16) for u_ in units}
    a_ak_t = {u_: jnp.where(strict_t, m_t[u_][CHUNK:, :CHUNK], 0.0).astype(BF16) for u_ in units}
    a_rk_t = {u_: jnp.where(incl_t, m_t[u_][CHUNK:, CHUNK:], 0.0).astype(BF16) for u_ in units}
    inv = {u_: eye + a_ab_t[u_] for u_ in units}
    pw = a_ab_t
    for _ in range(n_doubling):
        pwb = {u_: pw[u_].astype(BF16) for u_ in units}
        pw = {u_: _dot(pwb[u_], pwb[u_]) for u_ in units}
        inv = {u_: inv[u_] + _dot(inv[u_].astype(BF16), pw[u_].astype(BF16)) for u_ in units}
    vh = {(c, h): terms['v_b'][_rwkv_rows(c), h * HEAD_DIM:(h + 1) * HEAD_DIM] for c, h in units}
    return dict(lhs=lhs, inv_t={u_: inv[u_].astype(BF16) for u_ in units}, a_rb_t=a_rb_t,
                akv_t={u_: _dot_tn(vh[u_], a_ak_t[u_]) for u_ in units},
                arkv_t={u_: _dot_tn(vh[u_], a_rk_t[u_]) for u_ in units},
                vk={(c, h): _dot_tn(vh[c, h], sl('ke_par', c, h)) for c, h in units},
                be={(c, h): sl('be_par', c, h) for c, h in units})


def _rwkv_kernel(uf_ref, ufp_ref, ufn_ref, ur_ref, urp_ref, urn_ref, mu_ref, kkp_ref, ka_ref, rk_ref, g1_ref, g2_ref,
                 w0_ref, w1_ref, w2_ref, a0_ref, a1_ref, a2_ref, yf_ref, yr_ref, bonus_ref, gate_ref, st_ref,
                 *, ctx_tiles, n_tiles):
    tm = uf_ref.shape[1]
    n_chunks = tm // CHUNK
    j = pl.program_id(1)

    @pl.when(j == 0)
    def _():
        st_ref[...] = jnp.zeros_like(st_ref)

    shared = (mu_ref, kkp_ref, ka_ref, rk_ref, g1_ref, g2_ref)
    wdir = (w0_ref, w1_ref, w2_ref, a0_ref, a1_ref, a2_ref)
    pos = [_scan_order(j, ctx_tiles, n_tiles, rev) for rev in (False, True)]
    terms = [_rwkv_tile_terms(uf_ref, ufp_ref, ufn_ref, pos[0], shared, wdir, 0, False, ctx_tiles, n_tiles,
                              (bonus_ref, gate_ref)),
             _rwkv_tile_terms(ur_ref, urp_ref, urn_ref, pos[1], shared, wdir, 1, True, ctx_tiles, n_tiles, None)]
    pre = [_rwkv_chunk_terms(terms[d], d == 1, n_chunks) for d in range(2)]
    y_refs = (yf_ref, yr_ref)

    chains = [(d, h) for d in range(2) for h in range(RWKV_HEADS)]
    for step in range(n_chunks):
        chunk = {0: step, 1: n_chunks - 1 - step}
        st = {(d, h): st_ref[d, h] for d, h in chains}
        x0_t = {(d, h): _dot_nt(st[d, h].astype(BF16), pre[d]['lhs'][chunk[d], h]) for d, h in chains}
        u_t = {(d, h): _dot((x0_t[d, h][:, :CHUNK] + pre[d]['akv_t'][chunk[d], h]).astype(BF16),
                            pre[d]['inv_t'][chunk[d], h]).astype(BF16) for d, h in chains}
        for d, h in chains:
            c = chunk[d]
            st_ref[d, h] = (st[d, h] * terms[d]['decay'][c * CHUNK:c * CHUNK + 1, _rwkv_lanes(h)]
                            + _dot(u_t[d, h], pre[d]['be'][c, h]) + pre[d]['vk'][c, h])
        for d in range(2):
            c = chunk[d]
            y_t = [x0_t[d, h][:, CHUNK:] + _dot(u_t[d, h], pre[d]['a_rb_t'][c, h]) + pre[d]['arkv_t'][c, h]
                   for h in range(RWKV_HEADS)]
            y_refs[d][0, _rwkv_rows(c), :] = jnp.concatenate([z.T for z in y_t], axis=1)


def rwkv_scan(u, prm, n_ctx):
    b, s, w4 = u.shape
    tm = TOKEN_TILE
    w = MIX_W
    ctx_tiles, n_tiles = n_ctx // tm, s // tm
    halo_blocks = tm // 8

    def maps(reverse):
        def tile_map(i, j):
            return (i, _scan_order(j, ctx_tiles, n_tiles, reverse), 0)

        def up_map(i, j):
            return (i, jnp.maximum(_scan_order(j, ctx_tiles, n_tiles, reverse) * halo_blocks - 1, 0), 0)

        def un_map(i, j):
            return (i, jnp.minimum((_scan_order(j, ctx_tiles, n_tiles, reverse) + 1) * halo_blocks, s // 8 - 1), 0)

        return tile_map, up_map, un_map

    def full(a):
        return pl.BlockSpec(a.shape, lambda i, j: (0,) * a.ndim)

    fwd, rev = maps(False), maps(True)
    in_specs = []
    for m in (fwd, rev):
        in_specs += [pl.BlockSpec((1, tm, w4), m[0]), pl.BlockSpec((1, 8, w4), m[1]), pl.BlockSpec((1, 8, w4), m[2])]
    return pl.pallas_call(
        functools.partial(_rwkv_kernel, ctx_tiles=ctx_tiles, n_tiles=n_tiles),
        out_shape=[jax.ShapeDtypeStruct((b, s, w), F32)] * 4,
        grid=(b, n_tiles),
        in_specs=in_specs + [full(a) for a in prm],
        out_specs=[pl.BlockSpec((1, tm, w), fwd[0]), pl.BlockSpec((1, tm, w), rev[0]),
                   pl.BlockSpec((1, tm, w), fwd[0]), pl.BlockSpec((1, tm, w), fwd[0])],
        scratch_shapes=[pltpu.VMEM((2, RWKV_HEADS, HEAD_DIM, 2 * HEAD_DIM), F32)],
        compiler_params=_params(("parallel", "arbitrary")),
        name="rwkv_scan",
    )(u, u, u, u, u, u, *prm)


def rwkv_params(p):
    row = lambda z: z.reshape(1, -1)
    rows2 = lambda z: z.reshape(2, 1, -1)
    bf = lambda z: z.astype(BF16)
    return (row(p['rwkv_mu']), row(p['rwkv_kk']), row(p['rwkv_ka']), row(p['rwkv_rk']),
            bf(p['rwkv_g1']), bf(p['rwkv_g2']),
            rows2(p['rwkv_w0']), bf(p['rwkv_w1']), bf(p['rwkv_w2']),
            rows2(p['rwkv_a0']), bf(p['rwkv_a1']), bf(p['rwkv_a2']))


def _gelu_tanh(y):
    return 0.5 * y * (1.0 + jnp.tanh(math.sqrt(2.0 / math.pi) * (y + 0.044715 * (y * y * y))))


def _merge_kernel(x_ref, g_ref, mod_ref, wgate_ref, wbr_ref, wout_ref,
                  yaf_ref, yar_ref, bonus_ref, gate_ref, lna_ref,
                  gof_ref, gor_ref, ub_ref, lnb_ref,
                  yc_ref, us5_ref, ysf_ref, ysr_ref, s5d_ref, wglu_ref, bglu_ref, o_ref):
    d = x_ref.shape[2]
    x = x_ref[0]
    h = _norm_mod(x, g_ref[...], mod_ref[0, 0, 0:1, :], mod_ref[0, 0, 1:2, :]).astype(BF16)
    seg = _seg_ones(MIX_W, HEAD_DIM).astype(BF16)

    def head_mean(z):
        return _dot_split(z, seg) * (1.0 / HEAD_DIM)

    y = yaf_ref[0] + yar_ref[0]
    yc0 = y - head_mean(y)
    ya = (yc0 * lax.rsqrt(head_mean(yc0 * yc0) + GN_EPS) * lna_ref[...] + bonus_ref[0]) * gate_ref[0]

    o = gof_ref[0] + gor_ref[0]
    qk = GLA_HEADS * GLA_DK
    r = ub_ref[0][:, 2 * qk + MIX_W:2 * qk + 2 * MIX_W]
    yb = o * lax.rsqrt(head_mean(o * o) + EPS) * lnb_ref[...] * (r * _sigmoid(r))

    yd = _gelu_tanh(s5d_ref[...] * us5_ref[0] + ysf_ref[0] + ysr_ref[0])
    yd = yd * _sigmoid(_dot(yd.astype(BF16), wglu_ref[...]) + bglu_ref[...])

    merged = None
    for i, yi in enumerate((ya, yb, yc_ref[0], yd)):
        gate_i = _sigmoid(_dot(h, wgate_ref[:, i * d:(i + 1) * d]))
        term = gate_i * _dot(yi.astype(BF16), wbr_ref[i])
        merged = term if merged is None else merged + term
    o_ref[0] = x + mod_ref[0, 0, 2:3, :] * _dot(merged.astype(BF16), wout_ref[...])


def merge_branches(xs, g, mods, w_gate, w_branch, w_out, branch_inputs, n_ctx):
    b, s, d = xs.shape
    tm = TOKEN_TILE
    ctx_tiles = n_ctx // tm
    (yaf, yar, bonus, gate, lna, gof, gor, ub, lnb, yc, us5, ysf, ysr, s5d, wglu, bglu) = branch_inputs

    def tok(a):
        return pl.BlockSpec((1, tm, a.shape[2]), lambda i, j: (i, j, 0))

    def full(a):
        return pl.BlockSpec(a.shape, lambda i, j: (0,) * a.ndim)

    mod_spec = pl.BlockSpec((1, 1, 6, d), lambda i, j: (i, (j >= ctx_tiles).astype(jnp.int32), 0, 0))
    return pl.pallas_call(
        _merge_kernel,
        out_shape=jax.ShapeDtypeStruct((b, s, d), F32),
        grid=(b, s // tm),
        in_specs=[tok(xs), full(g), mod_spec, full(w_gate), full(w_branch), full(w_out),
                  tok(yaf), tok(yar), tok(bonus), tok(gate), full(lna),
                  tok(gof), tok(gor), tok(ub), full(lnb),
                  tok(yc), tok(us5), tok(ysf), tok(ysr), full(s5d), full(wglu), full(bglu)],
        out_specs=tok(xs),
        compiler_params=_params(("parallel", "parallel")),
        name="merge_branches",
    )(xs, g, mods, w_gate, w_branch, w_out, yaf, yar, bonus, gate, lna, gof, gor, ub, lnb,
      yc, us5, ysf, ysr, s5d, wglu, bglu)


def _router_kernel(x_ref, g_ref, mod_ref, wr_ref, br_ref, h_ref, idx_ref, wt_ref):
    h = _norm_mod(x_ref[0], g_ref[...], mod_ref[0, 0, 3:4, :], mod_ref[0, 0, 4:5, :])
    for k in range(h.shape[1] // 128):
        h_ref[pl.ds(k, h.shape[0], stride=ROW_TILE), :] = h[:, k * 128:(k + 1) * 128]
    h_hi, h_lo = _split_bf16(h)
    logits = _dot(h_hi, wr_ref[0]) + (_dot(h_lo, wr_ref[0]) + _dot(h_hi, wr_ref[1])) + br_ref[...]
    lane = lax.broadcasted_iota(jnp.int32, logits.shape, 1)
    big = jnp.int32(1 << 20)

    def first_max(z):
        m = jnp.max(z, axis=-1, keepdims=True)
        return m, jnp.min(jnp.where(z == m, lane, big), axis=-1, keepdims=True)

    gl = jnp.where(lane < N_GROUPS, logits, -jnp.inf)
    gmax, gsel = first_max(gl)
    g_w = 1.0 / jnp.sum(jnp.exp(gl - gmax), axis=-1, keepdims=True)
    lo = N_GROUPS + EXPERTS_PER_GROUP * gsel
    el = jnp.where((lane >= lo) & (lane < lo + EXPERTS_PER_GROUP), logits, -jnp.inf)
    v1, i1 = first_max(el)
    v2, i2 = first_max(jnp.where(lane == i1, -jnp.inf, el))
    e2 = jnp.exp(v2 - v1)
    w1 = g_w / (1.0 + e2)
    w2 = g_w * e2 / (1.0 + e2)
    idx_ref[...] = jnp.where(lane == 0, i1 - N_GROUPS, jnp.where(lane == 1, i2 - N_GROUPS, 0))
    wt_ref[...] = jnp.where(lane == 0, w1, jnp.where(lane == 1, w2, 0.0))


def moe_router(xs, g, mods, w_router, b_router, n_ctx):
    b, s, d = xs.shape
    tm = TOKEN_TILE
    ctx_tiles, n_tiles = n_ctx // tm, s // tm
    n = b * s
    mod_spec = pl.BlockSpec((1, 1, 6, d), lambda i, j: (i, (j >= ctx_tiles).astype(jnp.int32), 0, 0))
    flat = lambda i, j: (i * n_tiles + j, 0)
    return pl.pallas_call(
        _router_kernel,
        out_shape=[jax.ShapeDtypeStruct((n * ROW_TILE, d // ROW_TILE), F32),
                   jax.ShapeDtypeStruct((n, 128), jnp.int32), jax.ShapeDtypeStruct((n, 128), F32)],
        grid=(b, n_tiles),
        in_specs=[pl.BlockSpec((1, tm, d), lambda i, j: (i, j, 0)),
                  pl.BlockSpec((1, d), lambda i, j: (0, 0)), mod_spec,
                  pl.BlockSpec((2, d, 128), lambda i, j: (0, 0, 0)),
                  pl.BlockSpec((1, 128), lambda i, j: (0, 0))],
        out_specs=[pl.BlockSpec((tm * ROW_TILE, d // ROW_TILE), flat), pl.BlockSpec((tm, 128), flat),
                   pl.BlockSpec((tm, 128), flat)],
        compiler_params=_params(("parallel", "parallel")),
        name="moe_router",
    )(xs, g, mods, w_router, b_router)


def _row_copy(src, src_row, dst, dst_row, sem):
    return pltpu.make_async_copy(src.at[pl.ds(pl.multiple_of(src_row, ROW_TILE), ROW_TILE)],
                                 dst.at[pl.ds(pl.multiple_of(dst_row, ROW_TILE), ROW_TILE)], sem)


def _load_rows(ref, n_rows):
    return jnp.concatenate([ref[pl.ds(k, n_rows, stride=ROW_TILE), :] for k in range(ROW_TILE)], axis=1)


def _expert_kernel(blk_e_ref, tok_ref, tok_next_ref, dst_ref, dst_prev_ref, dst_prev2_ref,
                   h_hbm, wg_ref, wu_ref, wd_ref, out_hbm, xbuf, ybuf, wgb, wub, wdb, gsem, ssem):
    i = pl.program_id(0)
    last = pl.num_programs(0) - 1
    slot = i % 2

    @pl.when((i == 0) | (blk_e_ref[i] != blk_e_ref[jnp.maximum(i - 1, 0)]))
    def _():
        wgb[...] = wg_ref[0].astype(BF16)
        wub[...] = wu_ref[0].astype(BF16)
        wdb[...] = wd_ref[0].astype(BF16)

    def gather(ids_ref, s, start, rows=range(MOE_BLOCK)):
        for r in rows:
            cp = _row_copy(h_hbm, ids_ref[0, 0, r], xbuf.at[s], r * ROW_TILE, gsem.at[s])
            if start:
                cp.start(priority=r % 2)
            else:
                cp.wait()

    def scatter(ids_ref, s, start, rows=range(MOE_BLOCK)):
        for r in rows:
            cp = _row_copy(ybuf.at[s], r * ROW_TILE, out_hbm, ids_ref[0, 0, r], ssem.at[s])
            if start:
                cp.start(priority=r % 2)
            else:
                cp.wait()

    @pl.when(i == 0)
    def _():
        gather(tok_ref, 0, True)

    gather(tok_ref, slot, False)
    xb = _load_rows(xbuf.at[slot], MOE_BLOCK).astype(BF16)
    n_slices = 4
    hw = wgb.shape[1] // n_slices
    per_slice = MOE_BLOCK // n_slices
    acc = None
    for k in range(n_slices):
        rows = range(k * per_slice, (k + 1) * per_slice)
        gather(tok_next_ref, 1 - slot, True, rows)

        @pl.when(i >= 1)
        def _():
            scatter(dst_prev_ref, 1 - slot, True, rows)

        gate = _dot(xb, wgb[:, k * hw:(k + 1) * hw])
        hid = gate * _sigmoid(gate) * _dot(xb, wub[:, k * hw:(k + 1) * hw])
        part = _dot(hid.astype(BF16), wdb[k * hw:(k + 1) * hw, :])
        acc = part if acc is None else acc + part

    @pl.when(i >= 2)
    def _():
        scatter(dst_prev2_ref, slot, False)

    for k in range(ROW_TILE):
        ybuf[slot, pl.ds(k, MOE_BLOCK, stride=ROW_TILE), :] = acc[:, k * 128:(k + 1) * 128]

    @pl.when(i == last)
    def _():
        gather(tok_next_ref, 1 - slot, False)
        scatter(dst_ref, slot, True)

        @pl.when(i >= 1)
        def _():
            scatter(dst_prev_ref, 1 - slot, False)

        scatter(dst_ref, slot, False)


def moe_experts(h2, blk_e, buf_tok, buf_dst, w_gate, w_up, w_down):
    d = w_gate.shape[1]
    n_blk = blk_e.shape[0]
    hid = w_gate.shape[2]
    tile_rows = MOE_BLOCK * ROW_TILE

    def ids(shift):
        return pl.BlockSpec((1, 1, MOE_BLOCK), lambda i, be: (jnp.clip(i + shift, 0, n_blk - 1), 0, 0),
                            memory_space=pltpu.SMEM)

    grid_spec = pltpu.PrefetchScalarGridSpec(
        num_scalar_prefetch=1,
        grid=(n_blk,),
        in_specs=[ids(0), ids(1), ids(0), ids(-1), ids(-2),
                  pl.BlockSpec(memory_space=pl.ANY),
                  pl.BlockSpec((1, d, hid), lambda i, be: (be[i], 0, 0)),
                  pl.BlockSpec((1, d, hid), lambda i, be: (be[i], 0, 0)),
                  pl.BlockSpec((1, hid, d), lambda i, be: (be[i], 0, 0))],
        out_specs=pl.BlockSpec(memory_space=pl.ANY),
        scratch_shapes=[pltpu.VMEM((2, tile_rows, d // ROW_TILE), F32), pltpu.VMEM((2, tile_rows, d // ROW_TILE), F32),
                        pltpu.VMEM((d, hid), BF16), pltpu.VMEM((d, hid), BF16), pltpu.VMEM((hid, d), BF16),
                        pltpu.SemaphoreType.DMA((2,)), pltpu.SemaphoreType.DMA((2,))],
    )
    tok = buf_tok.reshape(n_blk, 1, MOE_BLOCK)
    dst = buf_dst.reshape(n_blk, 1, MOE_BLOCK)
    return pl.pallas_call(
        _expert_kernel,
        out_shape=jax.ShapeDtypeStruct((n_blk * tile_rows, d // ROW_TILE), F32),
        grid_spec=grid_spec,
        compiler_params=_params(("arbitrary",)),
        name="moe_experts",
    )(blk_e, tok, tok, dst, dst, dst, h2, w_gate, w_up, w_down)


def _combine_kernel(x_ref, mod_ref, wt_ref, fg_ref, y0_ref, y1_ref, o_ref, *, final_norm):
    wt = wt_ref[...]
    tm = x_ref.shape[1]
    y = wt[:, 0:1] * _load_rows(y0_ref, tm) + wt[:, 1:2] * _load_rows(y1_ref, tm)
    out = x_ref[0] + mod_ref[0, 0, 5:6, :] * y
    if final_norm:
        out = out * lax.rsqrt(jnp.mean(out * out, axis=-1, keepdims=True) + EPS) * fg_ref[...]
    o_ref[0] = out


def moe_combine(xs, mods, wts, y_rows, final_g, n_ctx, final_norm):
    b, s, d = xs.shape
    tm = TOKEN_TILE
    ctx_tiles, n_tiles = n_ctx // tm, s // tm
    first = ctx_tiles if final_norm else 0
    per_batch = n_tiles - first
    k_stride = b * n_tiles

    def flat(i, j):
        return i * n_tiles + j + first

    return pl.pallas_call(
        functools.partial(_combine_kernel, final_norm=final_norm),
        out_shape=jax.ShapeDtypeStruct((b, per_batch * tm, d), F32),
        grid=(b, per_batch),
        in_specs=[pl.BlockSpec((1, tm, d), lambda i, j: (i, j + first, 0)),
                  pl.BlockSpec((1, 1, 6, d), lambda i, j: (i, (j + first >= ctx_tiles).astype(jnp.int32), 0, 0)),
                  pl.BlockSpec((tm, 128), lambda i, j: (flat(i, j), 0)),
                  pl.BlockSpec((1, d), lambda i, j: (0, 0)),
                  pl.BlockSpec((tm * ROW_TILE, d // ROW_TILE), lambda i, j: (flat(i, j), 0)),
                  pl.BlockSpec((tm * ROW_TILE, d // ROW_TILE), lambda i, j: (flat(i, j) + k_stride, 0))],
        out_specs=pl.BlockSpec((1, tm, d), lambda i, j: (i, j, 0)),
        compiler_params=_params(("parallel", "parallel")),
        name="moe_combine",
    )(xs, mods, wts, final_g, y_rows, y_rows)


def moe_dispatch_plan(expert_ids):
    n = expert_ids.shape[0]
    flat_e = expert_ids.reshape(-1)
    n_assign = n * TOP_K
    iota = jnp.arange(n_assign, dtype=jnp.int32)
    _, order = lax.sort((flat_e, iota), num_keys=1, is_stable=True)
    one_hot = flat_e[:, None] == jnp.arange(N_EXPERTS, dtype=jnp.int32)[None, :]
    counts = jnp.sum(one_hot.astype(jnp.int32), axis=0)
    start = jnp.cumsum(counts) - counts
    padded = (counts + MOE_BLOCK - 1) // MOE_BLOCK * MOE_BLOCK
    pend = jnp.cumsum(padded)
    pstart = pend - padded
    cap = -(-n_assign // MOE_BLOCK) * MOE_BLOCK + N_EXPERTS * MOE_BLOCK
    n_blk = cap // MOE_BLOCK
    blk_first = jnp.arange(n_blk, dtype=jnp.int32) * MOE_BLOCK
    blk_e = jnp.minimum(jnp.sum((pend[None, :] <= blk_first[:, None]).astype(jnp.int32), axis=1), N_EXPERTS - 1)
    lane = jnp.arange(MOE_BLOCK, dtype=jnp.int32)[None, :]
    in_expert = (blk_first - pstart[blk_e])[:, None] + lane
    is_real = in_expert < counts[blk_e][:, None]
    assign = order[jnp.clip(start[blk_e][:, None] + in_expert, 0, n_assign - 1)]
    buf_tok = jnp.where(is_real, assign // TOP_K, 0)
    pad_row = n_assign + blk_first[:, None] + lane - (start + counts)[blk_e][:, None]
    buf_dst = jnp.where(is_real, (assign % TOP_K) * n + assign // TOP_K, pad_row)
    return blk_e, buf_tok.reshape(-1) * ROW_TILE, buf_dst.reshape(-1) * ROW_TILE


def hier_moe_layer(xs, g2, mods, p, expert_weights, first_expert, final_g, n_ctx, final_norm):
    d = xs.shape[2]
    w_router = jnp.zeros((d, 128), F32).at[:, :N_GROUPS].set(p['w_router_g']) \
        .at[:, N_GROUPS:N_GROUPS + N_EXPERTS].set(p['w_router_e'])
    b_router = jnp.zeros((1, 128), F32).at[0, :N_GROUPS].set(p['b_router_g']) \
        .at[0, N_GROUPS:N_GROUPS + N_EXPERTS].set(p['b_router_e'])
    h2, idx, wts = moe_router(xs, g2, mods, jnp.stack(_split_bf16(w_router)), b_router, n_ctx)
    blk_e, buf_tok, buf_dst = moe_dispatch_plan(idx[:, :TOP_K])
    y_rows = moe_experts(h2, blk_e + first_expert, buf_tok, buf_dst, *expert_weights)
    return moe_combine(xs, mods, wts, y_rows, final_g, n_ctx, final_norm)


def _layer(xs, cvec, p, expert_weights, first_expert, cos_t, sin_t, final_g, n_ctx, n_batch, final_norm):
    b, s, d = xs.shape
    m = adaln_mods(cvec, p['w_ada'], p['b_ada']).reshape(cvec.shape[0], 6, d)
    mods = jnp.stack([jnp.broadcast_to(m[n_batch], (n_batch, 6, d)), m[:n_batch]], axis=1)
    w_in = p['w_in']
    o1 = RWKV_IN
    o2 = o1 + GLA_IN
    o3 = o2 + ATTN_IN
    o4 = o3 + S5_IN
    hd = HEAD_DIM
    head_order = jnp.concatenate([jnp.arange(h * hd, (h + 1) * hd) for h in (0, 2, 1, 3)])
    w_att = jnp.concatenate([w_in[:, o2:o2 + ATTN_HEADS * hd][:, head_order], w_in[:, o2 + ATTN_HEADS * hd:o3]], axis=1)
    w_mix = jnp.concatenate([w_in[:, :o2], jnp.zeros((d, GLA_IN_PAD - GLA_IN), F32), w_att, w_in[:, o3:o4]],
                            axis=1).astype(BF16)
    w_gate = w_in[:, o4:].astype(BF16)
    w_branch = p['w_branch'].at[2].set(p['w_branch'][2][head_order]).astype(BF16)
    g1 = p['norm1_g'].reshape(1, d)
    ua, ub, uatt, us5 = in_proj(xs, g1, mods, w_mix, cos_t, sin_t, n_ctx)

    yaf, yar, bonus, gate = rwkv_scan(ua, rwkv_params(p), n_ctx)
    gof = gla_scan(ub, *gla_direction_params(p, 0), n_ctx, False)
    gor = gla_scan(ub, *gla_direction_params(p, 1), n_ctx, True)
    sink = jnp.zeros((1, 128), F32).at[0, :ATTN_HEADS].set(p['attn_sink'])
    yc = window_attention(uatt, sink, n_ctx)
    ysf = s5_scan(us5, s5_direction_params(p, 0), n_ctx, False)
    ysr = s5_scan(us5, s5_direction_params(p, 1), n_ctx, True)

    row = lambda z: z.reshape(1, -1)
    branch_inputs = (yaf, yar, bonus, gate, row(p['rwkv_ln_g']), gof, gor, ub, row(p['gla_ln_g']),
                     yc, us5, ysf, ysr, row(p['s5_d']), p['s5_w_glu'].astype(BF16), row(p['s5_b_glu']))
    xs = merge_branches(xs, g1, mods, w_gate, w_branch, p['w_out'].astype(BF16),
                        branch_inputs, n_ctx)
    return hier_moe_layer(xs, p['norm2_g'].reshape(1, d), mods, p, expert_weights, first_expert, final_g, n_ctx,
                          final_norm)


_LAYER_PARAMS = ('norm1_g', 'norm2_g', 'w_ada', 'b_ada', 'w_in', 'rwkv_mu', 'rwkv_w0', 'rwkv_w1', 'rwkv_w2',
                 'rwkv_a0', 'rwkv_a1', 'rwkv_a2', 'rwkv_kk', 'rwkv_ka', 'rwkv_rk', 'rwkv_g1', 'rwkv_g2',
                 'rwkv_ln_g', 'gla_a2', 'gla_ab', 'gla_ln_g', 'attn_sink', 's5_lam_re', 's5_lam_im', 's5_log_dt',
                 's5_b_re', 's5_b_im', 's5_c_re', 's5_c_im', 's5_d', 's5_w_glu', 's5_b_glu', 'w_branch', 'w_out',
                 'w_router_g', 'b_router_g', 'w_router_e', 'b_router_e', 'w_exp_gate', 'w_exp_up', 'w_exp_down')


def kernel(x, c, ctx, c_ctx, norm1_g, norm2_g, final_norm_g, w_ada, b_ada, w_in, rwkv_mu, rwkv_w0, rwkv_w1, rwkv_w2, rwkv_a0, rwkv_a1, rwkv_a2, rwkv_kk, rwkv_ka, rwkv_rk, rwkv_g1, rwkv_g2, rwkv_ln_g, gla_a2, gla_ab, gla_ln_g, attn_sink, s5_lam_re, s5_lam_im, s5_log_dt, s5_b_re, s5_b_im, s5_c_re, s5_c_im, s5_d, s5_w_glu, s5_b_glu, w_branch, w_out, w_router_g, b_router_g, w_router_e, b_router_e, w_exp_gate, w_exp_up, w_exp_down):
    stacked = dict(zip(_LAYER_PARAMS, (
        norm1_g, norm2_g, w_ada, b_ada, w_in, rwkv_mu, rwkv_w0, rwkv_w1, rwkv_w2, rwkv_a0, rwkv_a1, rwkv_a2,
        rwkv_kk, rwkv_ka, rwkv_rk, rwkv_g1, rwkv_g2, rwkv_ln_g, gla_a2, gla_ab, gla_ln_g, attn_sink,
        s5_lam_re, s5_lam_im, s5_log_dt, s5_b_re, s5_b_im, s5_c_re, s5_c_im, s5_d, s5_w_glu, s5_b_glu,
        w_branch, w_out, w_router_g, b_router_g, w_router_e, b_router_e, w_exp_gate, w_exp_up, w_exp_down)))
    n_batch, n_lat, d = x.shape
    n_ctx = ctx.shape[1]
    depth = w_in.shape[0]
    xs = jnp.concatenate([ctx, x], axis=1)
    cvec = jnp.zeros((16, d), F32).at[:n_batch].set(c).at[n_batch].set(c_ctx)
    cos_t, sin_t = rope_tables(n_ctx, n_lat)
    final_g = final_norm_g.reshape(1, d)
    expert_names = ('w_exp_gate', 'w_exp_up', 'w_exp_down')
    expert_weights = tuple(stacked[k].reshape((-1,) + stacked[k].shape[2:]) for k in expert_names)
    for l in range(depth):
        p = {k: v[l] for k, v in stacked.items() if k not in expert_names}
        xs = _layer(xs, cvec, p, expert_weights, l * N_EXPERTS, cos_t, sin_t, final_g, n_ctx, n_batch,
                    l == depth - 1)
    return xs
```

```python
import functools
import math

import jax
import jax.numpy as jnp
from jax import lax
from jax.experimental import pallas as pl
from jax.experimental.pallas import tpu as pltpu

F32 = jnp.float32
BF16 = jnp.bfloat16
HI = lax.Precision.HIGHEST

EPS = 1e-6
GN_EPS = 64e-5
ROPE_BASE = 10000.0
NEG_INF = -1e30
GRID_W = 64

MIX_W = 256
N_BRANCH = 4
HEAD_DIM = 64
RWKV_HEADS = MIX_W // HEAD_DIM
GLA_HEADS = 4
GLA_DK = 32
GLA_DV = 64
GLA_RANK = 16
GLA_TAU = 16.0
ATTN_HEADS = 4
ATTN_KV_HEADS = 2
WINDOW = 128
ATTN_BLOCK = 128
S5_GROUPS = 16
S5_GROUP_CH = MIX_W // S5_GROUPS
S5_STATE = 64
N_GROUPS = 4
EXPERTS_PER_GROUP = 8
N_EXPERTS = N_GROUPS * EXPERTS_PER_GROUP
TOP_K = 2
MOE_BLOCK = 128
ROW_TILE = 8

RWKV_IN = 4 * MIX_W
GLA_IN = 2 * GLA_HEADS * GLA_DK + 2 * GLA_HEADS * GLA_DV + 2 * GLA_RANK
GLA_IN_PAD = 896
ATTN_IN = (ATTN_HEADS + 2 * ATTN_KV_HEADS) * HEAD_DIM
S5_IN = MIX_W
MIX_IN_PAD = RWKV_IN + GLA_IN_PAD + ATTN_IN + S5_IN

CHUNK = 64
TOKEN_TILE = 256
VMEM_LIMIT = 56 * 1024 * 1024


def _params(sem):
    return pltpu.CompilerParams(dimension_semantics=sem, vmem_limit_bytes=VMEM_LIMIT)


def _sigmoid(z):
    return 1.0 / (1.0 + jnp.exp(-z))


def _dot(a, b, prec=None):
    return jnp.dot(a, b, precision=prec, preferred_element_type=F32)


def _dot_nt(a, b, prec=None):
    return lax.dot_general(a, b, (((1,), (1,)), ((), ())), precision=prec, preferred_element_type=F32)


def _dot_tn(a, b, prec=None):
    return lax.dot_general(a, b, (((0,), (0,)), ((), ())), precision=prec, preferred_element_type=F32)


def _split_bf16(x):
    hi = x.astype(BF16)
    return hi, (x - hi.astype(F32)).astype(BF16)


def _dot_split(x, m):
    hi, lo = _split_bf16(x)
    return _dot(hi, m) + _dot(lo, m)


def _dot_split_rhs(m, x):
    hi, lo = _split_bf16(x)
    return _dot(m, hi) + _dot(m, lo)


def _seg_ones(width, seg):
    r = lax.broadcasted_iota(jnp.int32, (width, width), 0) // seg
    c = lax.broadcasted_iota(jnp.int32, (width, width), 1) // seg
    return (r == c).astype(F32)


def _order_mask(n, reverse, inclusive):
    t = lax.broadcasted_iota(jnp.int32, (n, n), 0)
    i = lax.broadcasted_iota(jnp.int32, (n, n), 1)
    if reverse:
        return (i >= t) if inclusive else (i > t)
    return (i <= t) if inclusive else (i < t)


def _norm_mod(x, g, shift, scale):
    ms = jnp.mean(x * x, axis=-1, keepdims=True)
    return (x * lax.rsqrt(ms + EPS) * g) * (1.0 + scale) + shift


def _ada_kernel(c_ref, w_ref, b_ref, o_ref):
    cv = c_ref[...]
    o_ref[...] = _dot(cv * _sigmoid(cv), w_ref[...], HI) + b_ref[...]


def adaln_mods(cvec, w_ada, b_ada):
    rows, d = cvec.shape
    n = w_ada.shape[1]
    tn = 1024
    return pl.pallas_call(
        _ada_kernel,
        out_shape=jax.ShapeDtypeStruct((rows, n), F32),
        grid=(n // tn,),
        in_specs=[pl.BlockSpec((rows, d), lambda j: (0, 0)),
                  pl.BlockSpec((d, tn), lambda j: (0, j)),
                  pl.BlockSpec((1, tn), lambda j: (0, j))],
        out_specs=pl.BlockSpec((rows, tn), lambda j: (0, j)),
        compiler_params=_params(("parallel",)),
        name="adaln_mods",
    )(cvec, w_ada, b_ada.reshape(1, n))


def _rope(z, cos, sin):
    lane = lax.broadcasted_iota(jnp.int32, z.shape, 1)
    swapped = jnp.where((lane % HEAD_DIM) < HEAD_DIM // 2,
                        pltpu.roll(z, 128 - HEAD_DIM // 2, axis=1),
                        pltpu.roll(z, HEAD_DIM // 2, axis=1))
    return z * cos + swapped * sin


def _in_proj_kernel(x_ref, g_ref, mod_ref, w_ref, cos_ref, sin_ref, oa_ref, ob_ref, oc_ref, od_ref):
    h = _norm_mod(x_ref[0], g_ref[...], mod_ref[0, 0, 0:1, :], mod_ref[0, 0, 1:2, :])
    u = _dot(h.astype(BF16), w_ref[...])
    o1 = RWKV_IN
    o2 = o1 + GLA_IN_PAD
    o3 = o2 + ATTN_IN
    oa_ref[0] = u[:, :o1]
    ob_ref[0] = u[:, o1:o2]
    od_ref[0] = u[:, o3:]
    cos, sin = cos_ref[...], sin_ref[...]
    scale = HEAD_DIM ** -0.5
    qw = ATTN_HEADS * HEAD_DIM
    kw = ATTN_KV_HEADS * HEAD_DIM
    parts = [_rope(u[:, o2 + c:o2 + c + 128], cos, sin) * scale for c in range(0, qw, 128)]
    parts += [_rope(u[:, o2 + qw + c:o2 + qw + c + 128], cos, sin) for c in range(0, kw, 128)]
    parts.append(u[:, o2 + qw + kw:o3])
    oc_ref[0] = jnp.concatenate(parts, axis=1).astype(BF16)


def in_proj(xs, g, mods, w_mix, cos_t, sin_t, n_ctx):
    b, s, d = xs.shape
    tm = TOKEN_TILE
    ctx_tiles = n_ctx // tm
    widths = (RWKV_IN, GLA_IN_PAD, ATTN_IN, S5_IN)
    dtypes = (F32, F32, BF16, F32)
    return pl.pallas_call(
        _in_proj_kernel,
        out_shape=[jax.ShapeDtypeStruct((b, s, w), t) for w, t in zip(widths, dtypes)],
        grid=(b, s // tm),
        in_specs=[pl.BlockSpec((1, tm, d), lambda i, j: (i, j, 0)),
                  pl.BlockSpec((1, d), lambda i, j: (0, 0)),
                  pl.BlockSpec((1, 1, 6, d), lambda i, j: (i, (j >= ctx_tiles).astype(jnp.int32), 0, 0)),
                  pl.BlockSpec((d, MIX_IN_PAD), lambda i, j: (0, 0)),
                  pl.BlockSpec((tm, 128), lambda i, j: (j, 0)),
                  pl.BlockSpec((tm, 128), lambda i, j: (j, 0))],
        out_specs=[pl.BlockSpec((1, tm, w), lambda i, j: (i, j, 0)) for w in widths],
        compiler_params=_params(("parallel", "parallel")),
        name="in_proj",
    )(xs, g.reshape(1, d), mods, w_mix, cos_t, sin_t)


def _attn_kernel(own_ref, prev_ref, next_ref, ctx_ref, sink_ref, band_ref, o_ref, *, ctx_blocks, n_lat):
    blk = ATTN_BLOCK
    qw = ATTN_HEADS * HEAD_DIM
    kw = ATTN_KV_HEADS * HEAD_DIM
    n = pl.program_id(1) - ctx_blocks
    own = own_ref[0]
    k_all = jnp.concatenate([prev_ref[0][:, qw:qw + kw], own[:, qw:qw + kw], next_ref[0][:, qw:qw + kw],
                             ctx_ref[0][:, qw:qw + kw]], axis=0)
    v_all = jnp.concatenate([prev_ref[0][:, qw + kw:], own[:, qw + kw:], next_ref[0][:, qw + kw:],
                             ctx_ref[0][:, qw + kw:]], axis=0)
    n_keys = k_all.shape[0]
    key_blk = lax.broadcasted_iota(jnp.int32, (1, n_keys), 1) // blk
    off = [jnp.where(ok, 0.0, NEG_INF) for ok in (n >= 1, n >= 0, (n >= 0) & ((n + 1) * blk < n_lat))]
    bias = band_ref[...] + jnp.where(key_blk == 0, off[0],
                                     jnp.where(key_blk == 1, off[1], jnp.where(key_blk == 2, off[2], 0.0)))
    low_half = lax.broadcasted_iota(jnp.int32, (blk, 128), 1) < HEAD_DIM
    top_rows = lax.broadcasted_iota(jnp.int32, (2 * blk, 1), 0) < blk
    zero = jnp.zeros((), BF16)
    outs = []
    for c in range(qw // 128):
        qc = own[:, c * 128:(c + 1) * 128]
        lhs = jnp.concatenate([jnp.where(low_half, qc, zero), jnp.where(low_half, zero, qc)], axis=0)
        s = _dot_nt(lhs, k_all) + bias
        sink = jnp.where(top_rows, sink_ref[0:1, c:c + 1], sink_ref[0:1, c + 2:c + 3])
        m = jnp.maximum(jnp.max(s, axis=-1, keepdims=True), sink)
        p = jnp.exp(s - m)
        den = jnp.sum(p, axis=-1, keepdims=True) + jnp.exp(sink - m)
        o = _dot(p.astype(BF16), v_all) / den
        outs.append(jnp.where(low_half, o[:blk], o[blk:]))
    o_ref[0] = jnp.concatenate(outs, axis=1)


def window_attention(u_att, sink, n_ctx):
    b, s, w = u_att.shape
    blk = ATTN_BLOCK
    ctx_blocks = n_ctx // blk
    nblk = s // blk
    kern = functools.partial(_attn_kernel, ctx_blocks=ctx_blocks, n_lat=s - n_ctx)
    q_in = jnp.arange(2 * blk)[:, None] % blk
    k_col = jnp.arange(3 * blk + n_ctx)[None, :]
    band = jnp.where((k_col >= 3 * blk) | (jnp.abs(q_in - (k_col - blk)) <= WINDOW), 0.0, NEG_INF).astype(F32)
    return pl.pallas_call(
        kern,
        out_shape=jax.ShapeDtypeStruct((b, s, ATTN_HEADS * HEAD_DIM), F32),
        grid=(b, nblk),
        in_specs=[pl.BlockSpec((1, blk, w), lambda i, j: (i, j, 0)),
                  pl.BlockSpec((1, blk, w), lambda i, j: (i, jnp.maximum(j - 1, 0), 0)),
                  pl.BlockSpec((1, blk, w), lambda i, j: (i, jnp.minimum(j + 1, nblk - 1), 0)),
                  pl.BlockSpec((1, n_ctx, w), lambda i, j: (i, 0, 0)),
                  pl.BlockSpec((1, 128), lambda i, j: (0, 0)),
                  pl.BlockSpec(band.shape, lambda i, j: (0, 0))],
        out_specs=pl.BlockSpec((1, blk, ATTN_HEADS * HEAD_DIM), lambda i, j: (i, j, 0)),
        compiler_params=_params(("parallel", "parallel")),
        name="window_attention",
    )(u_att, u_att, u_att, u_att, sink, band)


def rope_tables(n_ctx, n_lat):
    rows = n_lat // GRID_W
    row = jnp.repeat(jnp.arange(rows, dtype=F32), GRID_W)
    col = jnp.tile(jnp.arange(GRID_W, dtype=F32), rows)
    n_freq = HEAD_DIM // 4
    inv_freq = ROPE_BASE ** (-jnp.arange(n_freq, dtype=F32) / n_freq)
    ang = jnp.concatenate([row[:, None] * inv_freq, col[:, None] * inv_freq], axis=-1)
    cos, sin = jnp.cos(ang), jnp.sin(ang)
    cos_l = jnp.concatenate([cos, cos, cos, cos], axis=-1)
    sin_l = jnp.concatenate([-sin, sin, -sin, sin], axis=-1)
    cos_t = jnp.concatenate([jnp.ones((n_ctx, 128), F32), cos_l], axis=0)
    sin_t = jnp.concatenate([jnp.zeros((n_ctx, 128), F32), sin_l], axis=0)
    return cos_t, sin_t


def _scan_order(j, ctx_steps, n_steps, reverse):
    if not reverse:
        return j
    return jnp.where(j < ctx_steps, ctx_steps - 1 - j, n_steps - 1 - (j - ctx_steps))


S5_TILE = 128
S5_ROW_PAD = 8


def _s5_kernel(u_ref, bblk_ref, cblk_ref, ar_ref, ai_ref, y_ref, bu_ref, xs_ref, car_ref, cai_ref, *, reverse):
    nb, tm, _ = u_ref.shape
    ns = ar_ref.shape[1]
    stride = tm + S5_ROW_PAD

    @pl.when(pl.program_id(0) == 0)
    def _():
        car_ref[...] = jnp.zeros_like(car_ref)
        cai_ref[...] = jnp.zeros_like(cai_ref)

    nk = ns // 128
    for b in range(nb):
        bu = _dot(u_ref[b].astype(BF16), bblk_ref[...])
        for k in range(2 * nk):
            bu_ref[k, b * stride:b * stride + tm, :] = bu[:, k * 128:(k + 1) * 128]
    ar = [jnp.broadcast_to(ar_ref[:, k * 128:(k + 1) * 128], (nb, 128)) for k in range(nk)]
    ai = [jnp.broadcast_to(ai_ref[:, k * 128:(k + 1) * 128], (nb, 128)) for k in range(nk)]

    def step(i, carry):
        t = (tm - 1 - i) if reverse else i
        rows = pl.ds(t, nb, stride=stride)
        new = []
        for k in range(nk):
            xr, xi = carry[k], carry[nk + k]
            new.append((ar[k] * xr - ai[k] * xi + bu_ref[k, rows, :], ar[k] * xi + ai[k] * xr + bu_ref[nk + k, rows, :]))
        for k in range(nk):
            xs_ref[k, rows, :] = new[k][0]
            xs_ref[nk + k, rows, :] = new[k][1]
        return tuple(z[0] for z in new) + tuple(z[1] for z in new)

    init = tuple(car_ref[:, k * 128:(k + 1) * 128] for k in range(nk)) \
        + tuple(cai_ref[:, k * 128:(k + 1) * 128] for k in range(nk))
    final = lax.fori_loop(0, tm, step, init, unroll=4)
    for k in range(nk):
        car_ref[:, k * 128:(k + 1) * 128] = final[k]
        cai_ref[:, k * 128:(k + 1) * 128] = final[nk + k]
    for b in range(nb):
        xs = jnp.concatenate([xs_ref[k, b * stride:b * stride + tm, :] for k in range(2 * nk)], axis=1)
        y_ref[b] = _dot(xs.astype(BF16), cblk_ref[...])


def s5_scan(u, prm, n_ctx, reverse):
    b, s, w = u.shape
    tm = S5_TILE
    ns = S5_GROUPS * S5_STATE
    ctx_tiles, n_tiles = n_ctx // tm, s // tm
    bblk, cblk, ar, ai = prm

    def tile_map(j):
        return (0, _scan_order(j, ctx_tiles, n_tiles, reverse), 0)

    def const(j):
        return (0, 0)

    rows = b * (tm + S5_ROW_PAD)
    return pl.pallas_call(
        functools.partial(_s5_kernel, reverse=reverse),
        out_shape=jax.ShapeDtypeStruct((b, s, w), F32),
        grid=(n_tiles,),
        in_specs=[pl.BlockSpec((b, tm, w), tile_map),
                  pl.BlockSpec((w, 2 * ns), const), pl.BlockSpec((2 * ns, w), const),
                  pl.BlockSpec((1, ns), const), pl.BlockSpec((1, ns), const)],
        out_specs=pl.BlockSpec((b, tm, w), tile_map),
        scratch_shapes=[pltpu.VMEM((2 * ns // 128, rows, 128), F32), pltpu.VMEM((2 * ns // 128, rows, 128), F32),
                        pltpu.VMEM((b, ns), F32), pltpu.VMEM((b, ns), F32)],
        compiler_params=_params(("arbitrary",)),
        name="s5_scan_rev" if reverse else "s5_scan_fwd",
    )(u, bblk, cblk, ar, ai)


def s5_direction_params(p, d):
    g, ns, ch = S5_GROUPS, S5_STATE, S5_GROUP_CH
    lr, li = p['s5_lam_re'][d], p['s5_lam_im'][d]
    dt = jnp.exp(p['s5_log_dt'][d])[:, None]

    def power(n):
        mag = jnp.exp(n * lr * dt)
        return mag * jnp.cos(n * li * dt), mag * jnp.sin(n * li * dt)

    ar, ai = power(1.0)
    den = lr * lr + li * li
    cr = ((ar - 1.0) * lr + ai * li) / den
    ci = (ai * lr - (ar - 1.0) * li) / den
    bre, bim = p['s5_b_re'], p['s5_b_im']
    bbr = cr[..., None] * bre - ci[..., None] * bim
    bbi = cr[..., None] * bim + ci[..., None] * bre
    eye = jnp.eye(g, dtype=F32)

    def expand_in(z):
        return jnp.einsum('gpc,gh->gchp', z, eye).reshape(g * ch, g * ns)

    def expand_out(z):
        return jnp.einsum('gcp,gh->gphc', z, eye).reshape(g * ns, g * ch)

    bblk = jnp.concatenate([expand_in(bbr), expand_in(bbi)], axis=1).astype(BF16)
    cblk = jnp.concatenate([expand_out(p['s5_c_re'][d]), -expand_out(p['s5_c_im'][d])], axis=0).astype(BF16)
    return bblk, cblk, ar.reshape(1, g * ns), ai.reshape(1, g * ns)


def _log_sigmoid(z):
    return jnp.minimum(z, 0.0) - jnp.log(1.0 + jnp.exp(-jnp.abs(z)))


def _gla_kernel(u_ref, a2_ref, ab_ref, o_ref, st_ref, *, reverse):
    tm = u_ref.shape[1]
    qk = GLA_HEADS * GLA_DK
    vw = GLA_HEADS * GLA_DV

    @pl.when(pl.program_id(1) == 0)
    def _():
        st_ref[...] = jnp.zeros_like(st_ref)

    u = u_ref[0]
    alow = u[:, 2 * qk + 2 * vw:]
    log_a = _log_sigmoid(_dot(alow, a2_ref[...], HI) + ab_ref[...]) * (1.0 / GLA_TAU)
    ti = lax.broadcasted_iota(jnp.int32, (tm, tm), 0)
    tj = lax.broadcasted_iota(jnp.int32, (tm, tm), 1)
    same_chunk = (ti // CHUNK) == (tj // CHUNK)
    before = (tj >= ti) if reverse else (tj <= ti)
    cum = _dot_split_rhs((same_chunk & before).astype(BF16), log_a)
    tot = _dot_split_rhs(same_chunk.astype(BF16), log_a)
    q_dec = u[:, 0:qk] * (GLA_DK ** -0.5) * jnp.exp(cum)
    k = u[:, qk:2 * qk]
    k_inv = (k * jnp.exp(-cum)).astype(BF16)
    k_end = k * jnp.exp(tot - cum)
    decay = jnp.exp(tot)
    v_b = u[:, 2 * qk:2 * qk + vw].astype(BF16)
    head_of_lane = lax.broadcasted_iota(jnp.int32, (tm, qk), 1) // GLA_DK
    q_m = [jnp.where(head_of_lane == h, q_dec, 0.0).astype(BF16) for h in range(GLA_HEADS)]
    ke_m = [jnp.where(head_of_lane == h, k_end, 0.0).astype(BF16) for h in range(GLA_HEADS)]

    incl = _order_mask(CHUNK, reverse, True)
    n_chunks = tm // CHUNK
    units = [(c, h) for c in range(n_chunks) for h in range(GLA_HEADS)]

    def rows_of(c):
        return slice(c * CHUNK, (c + 1) * CHUNK)

    vh = {(c, h): v_b[rows_of(c), h * GLA_DV:(h + 1) * GLA_DV] for c, h in units}
    scores = {(c, h): jnp.where(incl, _dot_nt(q_m[h][rows_of(c)], k_inv[rows_of(c)]), 0.0).astype(BF16)
              for c, h in units}
    intra = {u_: _dot(scores[u_], vh[u_]) for u_ in units}
    vk = {(c, h): _dot_tn(vh[c, h], ke_m[h][rows_of(c)]) for c, h in units}

    heads = range(GLA_HEADS)
    order = range(n_chunks - 1, -1, -1) if reverse else range(n_chunks)
    for c in order:
        st = [st_ref[h] for h in heads]
        o_ref[0, rows_of(c), :] = jnp.concatenate(
            [intra[c, h] + _dot_nt(q_m[h][rows_of(c)], st[h].astype(BF16)) for h in heads], axis=1)
        for h in heads:
            st_ref[h] = st[h] * decay[c * CHUNK:c * CHUNK + 1, :] + vk[c, h]


def gla_scan(u, a2pad, ab, n_ctx, reverse):
    b, s, w = u.shape
    tm = TOKEN_TILE
    ctx_tiles, n_tiles = n_ctx // tm, s // tm
    qk = GLA_HEADS * GLA_DK

    def tile_map(i, j):
        return (i, _scan_order(j, ctx_tiles, n_tiles, reverse), 0)

    return pl.pallas_call(
        functools.partial(_gla_kernel, reverse=reverse),
        out_shape=jax.ShapeDtypeStruct((b, s, GLA_HEADS * GLA_DV), F32),
        grid=(b, n_tiles),
        in_specs=[pl.BlockSpec((1, tm, w), tile_map),
                  pl.BlockSpec((128, qk), lambda i, j: (0, 0)),
                  pl.BlockSpec((1, qk), lambda i, j: (0, 0))],
        out_specs=pl.BlockSpec((1, tm, GLA_HEADS * GLA_DV), tile_map),
        scratch_shapes=[pltpu.VMEM((GLA_HEADS, GLA_DV, GLA_HEADS * GLA_DK), F32)],
        compiler_params=_params(("parallel", "arbitrary")),
        name="gla_scan_rev" if reverse else "gla_scan_fwd",
    )(u, a2pad, ab)


def gla_direction_params(p, d):
    a2pad = jnp.zeros((128, GLA_HEADS * GLA_DK), F32).at[d * GLA_RANK:(d + 1) * GLA_RANK].set(p['gla_a2'][d])
    return a2pad, p['gla_ab'][d].reshape(1, -1)


def _rwkv_tile_terms(u_ref, up_ref, un_ref, pos, shared, wdir, d, reverse, ctx_tiles, n_tiles, aux_refs):
    mu_ref, kkp_ref, ka_ref, rk_ref, g1_ref, g2_ref = shared
    w0_ref, w1_ref, w2_ref, a0_ref, a1_ref, a2_ref = wdir
    tm = u_ref.shape[1]
    w = MIX_W
    first = (pos == 0) | (pos == ctx_tiles)
    last_tile = (pos == ctx_tiles - 1) | (pos == n_tiles - 1)
    prev_row = jnp.where(first, 0.0, up_ref[0, 7:8, :])
    next_row = jnp.where(last_tile, 0.0, un_ref[0, 0:1, :])
    u = u_ref[0]
    row = lax.broadcasted_iota(jnp.int32, u.shape, 0)
    down = jnp.where(row == 0, prev_row, pltpu.roll(u, 1, axis=0))
    up = jnp.where(row == tm - 1, next_row, pltpu.roll(u, tm - 1, axis=0))
    mixed = u + (0.5 * (down + up) - u) * mu_ref[...]
    r, k, v, xa = mixed[:, 0:w], mixed[:, w:2 * w], mixed[:, 2 * w:3 * w], mixed[:, 3 * w:4 * w]
    seg = _seg_ones(w, HEAD_DIM).astype(BF16)
    kk = k * kkp_ref[...]
    kk = kk * lax.rsqrt(_dot_split(kk * kk, seg) + EPS)
    xab = xa.astype(BF16)
    zw = w0_ref[d] + _dot(jnp.tanh(_dot(xab, w1_ref[d])).astype(BF16), w2_ref[d])
    log_w = -math.exp(-0.5) * _sigmoid(zw)
    iclr = _sigmoid(a0_ref[d] + _dot(_dot(xab, a1_ref[d]).astype(BF16), a2_ref[d]))
    k_eff = k * (1.0 + (iclr - 1.0) * ka_ref[...])
    a_vec = -kk
    b_vec = kk * iclr
    if aux_refs is not None:
        bonus_ref, gate_ref = aux_refs
        gate_ref[0] = _dot(_sigmoid(_dot(xab, g1_ref[...])).astype(BF16), g2_ref[...])
        bonus_ref[0] = _dot_split(r * k * rk_ref[...], seg) * v

    ti = lax.broadcasted_iota(jnp.int32, (tm, tm), 0)
    tj = lax.broadcasted_iota(jnp.int32, (tm, tm), 1)
    same_chunk = (ti // CHUNK) == (tj // CHUNK)
    before = (tj >= ti) if reverse else (tj <= ti)
    cum = _dot_split_rhs((same_chunk & before).astype(BF16), log_w)
    tot = _dot_split_rhs(same_chunk.astype(BF16), log_w)
    e_neg = jnp.exp(-cum)
    e_end = jnp.exp(tot - cum)
    lane = lax.broadcasted_iota(jnp.int32, (tm, w), 1)
    even = (lane // HEAD_DIM) % 2 == 0

    def by_parity(z):
        return jnp.where(even, z, 0.0).astype(BF16), jnp.where(even, 0.0, z).astype(BF16)

    return dict(a_par=by_parity(a_vec * jnp.exp(cum - log_w)), r_par=by_parity(r * jnp.exp(cum)),
                be_par=by_parity(b_vec * e_end), ke_par=by_parity(k_eff * e_end),
                b_b=(b_vec * e_neg).astype(BF16), k_b=(k_eff * e_neg).astype(BF16), v_b=v.astype(BF16),
                decay=jnp.exp(tot))


def _rwkv_rows(c):
    return slice(c * CHUNK, (c + 1) * CHUNK)


def _rwkv_lanes(h):
    return slice((h // 2) * 128, (h // 2 + 1) * 128)


def _rwkv_chunk_terms(terms, reverse, n_chunks):
    units = [(c, h) for c in range(n_chunks) for h in range(RWKV_HEADS)]
    t_i = lax.broadcasted_iota(jnp.int32, (CHUNK, CHUNK), 0)
    t_j = lax.broadcasted_iota(jnp.int32, (CHUNK, CHUNK), 1)
    incl_t = (t_i >= t_j) if reverse else (t_i <= t_j)
    strict_t = (t_i > t_j) if reverse else (t_i < t_j)
    eye = (t_i == t_j).astype(F32)
    n_doubling = int(math.log2(CHUNK)) - 1

    def sl(name, c, h):
        return terms[name][h % 2][_rwkv_rows(c), _rwkv_lanes(h)]

    lhs = {(c, h): jnp.concatenate([sl('a_par', c, h), sl('r_par', c, h)], axis=0) for c, h in units}
    rhs = {(c, h): jnp.concatenate([terms['b_b'][_rwkv_rows(c), _rwkv_lanes(h)],
                                    terms['k_b'][_rwkv_rows(c), _rwkv_lanes(h)]], axis=0) for c, h in units}
    m_t = {u_: _dot_nt(rhs[u_], lhs[u_]) for u_ in units}
    a_ab_t = {u_: jnp.where(strict_t, m_t[u_][:CHUNK, :CHUNK], 0.0) for u_ in units}
    a_rb_t = {u_: jnp.where(incl_t, m_t[u_][:CHUNK, CHUNK:], 0.0).astype(BF16) for u_ in units}
    a_ak_t = {u_: jnp.where(strict_t, m_t[u_][CHUNK:, :CHUNK], 0.0).astype(BF16) for u_ in units}
    a_rk_t = {u_: jnp.where(incl_t, m_t[u_][CHUNK:, CHUNK:], 0.0).astype(BF16) for u_ in units}
    inv = {u_: eye + a_ab_t[u_] for u_ in units}
    pw = a_ab_t
    for _ in range(n_doubling):
        pwb = {u_: pw[u_].astype(BF16) for u_ in units}
        pw = {u_: _dot(pwb[u_], pwb[u_]) for u_ in units}
        inv = {u_: inv[u_] + _dot(inv[u_].astype(BF16), pw[u_].astype(BF16)) for u_ in units}
    vh = {(c, h): terms['v_b'][_rwkv_rows(c), h * HEAD_DIM:(h + 1) * HEAD_DIM] for c, h in units}
    return dict(lhs=lhs, inv_t={u_: inv[u_].astype(BF16) for u_ in units}, a_rb_t=a_rb_t,
                akv_t={u_: _dot_tn(vh[u_], a_ak_t[u_]) for u_ in units},
                arkv_t={u_: _dot_tn(vh[u_], a_rk_t[u_]) for u_ in units},
                vk={(c, h): _dot_tn(vh[c, h], sl('ke_par', c, h)) for c, h in units},
                be={(c, h): sl('be_par', c, h) for c, h in units})


def _rwkv_kernel(uf_ref, ufp_ref, ufn_ref, ur_ref, urp_ref, urn_ref, mu_ref, kkp_ref, ka_ref, rk_ref, g1_ref, g2_ref,
                 w0_ref, w1_ref, w2_ref, a0_ref, a1_ref, a2_ref, yf_ref, yr_ref, bonus_ref, gate_ref, st_ref,
                 *, ctx_tiles, n_tiles):
    tm = uf_ref.shape[1]
    n_chunks = tm // CHUNK
    j = pl.program_id(1)

    @pl.when(j == 0)
    def _():
        st_ref[...] = jnp.zeros_like(st_ref)

    shared = (mu_ref, kkp_ref, ka_ref, rk_ref, g1_ref, g2_ref)
    wdir = (w0_ref, w1_ref, w2_ref, a0_ref, a1_ref, a2_ref)
    pos = [_scan_order(j, ctx_tiles, n_tiles, rev) for rev in (False, True)]
    terms = [_rwkv_tile_terms(uf_ref, ufp_ref, ufn_ref, pos[0], shared, wdir, 0, False, ctx_tiles, n_tiles,
                              (bonus_ref, gate_ref)),
             _rwkv_tile_terms(ur_ref, urp_ref, urn_ref, pos[1], shared, wdir, 1, True, ctx_tiles, n_tiles, None)]
    pre = [_rwkv_chunk_terms(terms[d], d == 1, n_chunks) for d in range(2)]
    y_refs = (yf_ref, yr_ref)

    chains = [(d, h) for d in range(2) for h in range(RWKV_HEADS)]
    for step in range(n_chunks):
        chunk = {0: step, 1: n_chunks - 1 - step}
        st = {(d, h): st_ref[d, h] for d, h in chains}
        x0_t = {(d, h): _dot_nt(st[d, h].astype(BF16), pre[d]['lhs'][chunk[d], h]) for d, h in chains}
        u_t = {(d, h): _dot((x0_t[d, h][:, :CHUNK] + pre[d]['akv_t'][chunk[d], h]).astype(BF16),
                            pre[d]['inv_t'][chunk[d], h]).astype(BF16) for d, h in chains}
        for d, h in chains:
            c = chunk[d]
            st_ref[d, h] = (st[d, h] * terms[d]['decay'][c * CHUNK:c * CHUNK + 1, _rwkv_lanes(h)]
                            + _dot(u_t[d, h], pre[d]['be'][c, h]) + pre[d]['vk'][c, h])
        for d in range(2):
            c = chunk[d]
            y_t = [x0_t[d, h][:, CHUNK:] + _dot(u_t[d, h], pre[d]['a_rb_t'][c, h]) + pre[d]['arkv_t'][c, h]
                   for h in range(RWKV_HEADS)]
            y_refs[d][0, _rwkv_rows(c), :] = jnp.concatenate([z.T for z in y_t], axis=1)


def rwkv_scan(u, prm, n_ctx):
    b, s, w4 = u.shape
    tm = TOKEN_TILE
    w = MIX_W
    ctx_tiles, n_tiles = n_ctx // tm, s // tm
    halo_blocks = tm // 8

    def maps(reverse):
        def tile_map(i, j):
            return (i, _scan_order(j, ctx_tiles, n_tiles, reverse), 0)

        def up_map(i, j):
            return (i, jnp.maximum(_scan_order(j, ctx_tiles, n_tiles, reverse) * halo_blocks - 1, 0), 0)

        def un_map(i, j):
            return (i, jnp.minimum((_scan_order(j, ctx_tiles, n_tiles, reverse) + 1) * halo_blocks, s // 8 - 1), 0)

        return tile_map, up_map, un_map

    def full(a):
        return pl.BlockSpec(a.shape, lambda i, j: (0,) * a.ndim)

    fwd, rev = maps(False), maps(True)
    in_specs = []
    for m in (fwd, rev):
        in_specs += [pl.BlockSpec((1, tm, w4), m[0]), pl.BlockSpec((1, 8, w4), m[1]), pl.BlockSpec((1, 8, w4), m[2])]
    return pl.pallas_call(
        functools.partial(_rwkv_kernel, ctx_tiles=ctx_tiles, n_tiles=n_tiles),
        out_shape=[jax.ShapeDtypeStruct((b, s, w), F32)] * 4,
        grid=(b, n_tiles),
        in_specs=in_specs + [full(a) for a in prm],
        out_specs=[pl.BlockSpec((1, tm, w), fwd[0]), pl.BlockSpec((1, tm, w), rev[0]),
                   pl.BlockSpec((1, tm, w), fwd[0]), pl.BlockSpec((1, tm, w), fwd[0])],
        scratch_shapes=[pltpu.VMEM((2, RWKV_HEADS, HEAD_DIM, 2 * HEAD_DIM), F32)],
        compiler_params=_params(("parallel", "arbitrary")),
        name="rwkv_scan",
    )(u, u, u, u, u, u, *prm)


def rwkv_params(p):
    row = lambda z: z.reshape(1, -1)
    rows2 = lambda z: z.reshape(2, 1, -1)
    bf = lambda z: z.astype(BF16)
    return (row(p['rwkv_mu']), row(p['rwkv_kk']), row(p['rwkv_ka']), row(p['rwkv_rk']),
            bf(p['rwkv_g1']), bf(p['rwkv_g2']),
            rows2(p['rwkv_w0']), bf(p['rwkv_w1']), bf(p['rwkv_w2']),
            rows2(p['rwkv_a0']), bf(p['rwkv_a1']), bf(p['rwkv_a2']))


def _gelu_tanh(y):
    return 0.5 * y * (1.0 + jnp.tanh(math.sqrt(2.0 / math.pi) * (y + 0.044715 * (y * y * y))))


def _merge_kernel(x_ref, g_ref, mod_ref, wgate_ref, wbr_ref, wout_ref,
                  yaf_ref, yar_ref, bonus_ref, gate_ref, lna_ref,
                  gof_ref, gor_ref, ub_ref, lnb_ref,
                  yc_ref, us5_ref, ysf_ref, ysr_ref, s5d_ref, wglu_ref, bglu_ref, o_ref):
    d = x_ref.shape[2]
    x = x_ref[0]
    h = _norm_mod(x, g_ref[...], mod_ref[0, 0, 0:1, :], mod_ref[0, 0, 1:2, :]).astype(BF16)
    seg = _seg_ones(MIX_W, HEAD_DIM).astype(BF16)

    def head_mean(z):
        return _dot_split(z, seg) * (1.0 / HEAD_DIM)

    y = yaf_ref[0] + yar_ref[0]
    yc0 = y - head_mean(y)
    ya = (yc0 * lax.rsqrt(head_mean(yc0 * yc0) + GN_EPS) * lna_ref[...] + bonus_ref[0]) * gate_ref[0]

    o = gof_ref[0] + gor_ref[0]
    qk = GLA_HEADS * GLA_DK
    r = ub_ref[0][:, 2 * qk + MIX_W:2 * qk + 2 * MIX_W]
    yb = o * lax.rsqrt(head_mean(o * o) + EPS) * lnb_ref[...] * (r * _sigmoid(r))

    yd = _gelu_tanh(s5d_ref[...] * us5_ref[0] + ysf_ref[0] + ysr_ref[0])
    yd = yd * _sigmoid(_dot(yd.astype(BF16), wglu_ref[...]) + bglu_ref[...])

    merged = None
    for i, yi in enumerate((ya, yb, yc_ref[0], yd)):
        gate_i = _sigmoid(_dot(h, wgate_ref[:, i * d:(i + 1) * d]))
        term = gate_i * _dot(yi.astype(BF16), wbr_ref[i])
        merged = term if merged is None else merged + term
    o_ref[0] = x + mod_ref[0, 0, 2:3, :] * _dot(merged.astype(BF16), wout_ref[...])


def merge_branches(xs, g, mods, w_gate, w_branch, w_out, branch_inputs, n_ctx):
    b, s, d = xs.shape
    tm = TOKEN_TILE
    ctx_tiles = n_ctx // tm
    (yaf, yar, bonus, gate, lna, gof, gor, ub, lnb, yc, us5, ysf, ysr, s5d, wglu, bglu) = branch_inputs

    def tok(a):
        return pl.BlockSpec((1, tm, a.shape[2]), lambda i, j: (i, j, 0))

    def full(a):
        return pl.BlockSpec(a.shape, lambda i, j: (0,) * a.ndim)

    mod_spec = pl.BlockSpec((1, 1, 6, d), lambda i, j: (i, (j >= ctx_tiles).astype(jnp.int32), 0, 0))
    return pl.pallas_call(
        _merge_kernel,
        out_shape=jax.ShapeDtypeStruct((b, s, d), F32),
        grid=(b, s // tm),
        in_specs=[tok(xs), full(g), mod_spec, full(w_gate), full(w_branch), full(w_out),
                  tok(yaf), tok(yar), tok(bonus), tok(gate), full(lna),
                  tok(gof), tok(gor), tok(ub), full(lnb),
                  tok(yc), tok(us5), tok(ysf), tok(ysr), full(s5d), full(wglu), full(bglu)],
        out_specs=tok(xs),
        compiler_params=_params(("parallel", "parallel")),
        name="merge_branches",
    )(xs, g, mods, w_gate, w_branch, w_out, yaf, yar, bonus, gate, lna, gof, gor, ub, lnb,
      yc, us5, ysf, ysr, s5d, wglu, bglu)


def _router_kernel(x_ref, g_ref, mod_ref, wr_ref, br_ref, h_ref, idx_ref, wt_ref):
    h = _norm_mod(x_ref[0], g_ref[...], mod_ref[0, 0, 3:4, :], mod_ref[0, 0, 4:5, :])
    for k in range(h.shape[1] // 128):
        h_ref[pl.ds(k, h.shape[0], stride=ROW_TILE), :] = h[:, k * 128:(k + 1) * 128]
    h_hi, h_lo = _split_bf16(h)
    logits = _dot(h_hi, wr_ref[0]) + (_dot(h_lo, wr_ref[0]) + _dot(h_hi, wr_ref[1])) + br_ref[...]
    lane = lax.broadcasted_iota(jnp.int32, logits.shape, 1)
    big = jnp.int32(1 << 20)

    def first_max(z):
        m = jnp.max(z, axis=-1, keepdims=True)
        return m, jnp.min(jnp.where(z == m, lane, big), axis=-1, keepdims=True)

    gl = jnp.where(lane < N_GROUPS, logits, -jnp.inf)
    gmax, gsel = first_max(gl)
    g_w = 1.0 / jnp.sum(jnp.exp(gl - gmax), axis=-1, keepdims=True)
    lo = N_GROUPS + EXPERTS_PER_GROUP * gsel
    el = jnp.where((lane >= lo) & (lane < lo + EXPERTS_PER_GROUP), logits, -jnp.inf)
    v1, i1 = first_max(el)
    v2, i2 = first_max(jnp.where(lane == i1, -jnp.inf, el))
    e2 = jnp.exp(v2 - v1)
    w1 = g_w / (1.0 + e2)
    w2 = g_w * e2 / (1.0 + e2)
    idx_ref[...] = jnp.where(lane == 0, i1 - N_GROUPS, jnp.where(lane == 1, i2 - N_GROUPS, 0))
    wt_ref[...] = jnp.where(lane == 0, w1, jnp.where(lane == 1, w2, 0.0))


def moe_router(xs, g, mods, w_router, b_router, n_ctx):
    b, s, d = xs.shape
    tm = TOKEN_TILE
    ctx_tiles, n_tiles = n_ctx // tm, s // tm
    n = b * s
    mod_spec = pl.BlockSpec((1, 1, 6, d), lambda i, j: (i, (j >= ctx_tiles).astype(jnp.int32), 0, 0))
    flat = lambda i, j: (i * n_tiles + j, 0)
    return pl.pallas_call(
        _router_kernel,
        out_shape=[jax.ShapeDtypeStruct((n * ROW_TILE, d // ROW_TILE), F32),
                   jax.ShapeDtypeStruct((n, 128), jnp.int32), jax.ShapeDtypeStruct((n, 128), F32)],
        grid=(b, n_tiles),
        in_specs=[pl.BlockSpec((1, tm, d), lambda i, j: (i, j, 0)),
                  pl.BlockSpec((1, d), lambda i, j: (0, 0)), mod_spec,
                  pl.BlockSpec((2, d, 128), lambda i, j: (0, 0, 0)),
                  pl.BlockSpec((1, 128), lambda i, j: (0, 0))],
        out_specs=[pl.BlockSpec((tm * ROW_TILE, d // ROW_TILE), flat), pl.BlockSpec((tm, 128), flat),
                   pl.BlockSpec((tm, 128), flat)],
        compiler_params=_params(("parallel", "parallel")),
        name="moe_router",
    )(xs, g, mods, w_router, b_router)


def _row_copy(src, src_row, dst, dst_row, sem):
    return pltpu.make_async_copy(src.at[pl.ds(pl.multiple_of(src_row, ROW_TILE), ROW_TILE)],
                                 dst.at[pl.ds(pl.multiple_of(dst_row, ROW_TILE), ROW_TILE)], sem)


def _load_rows(ref, n_rows):
    return jnp.concatenate([ref[pl.ds(k, n_rows, stride=ROW_TILE), :] for k in range(ROW_TILE)], axis=1)


def _expert_kernel(blk_e_ref, tok_ref, tok_next_ref, h_hbm, wg_ref, wu_ref, wd_ref, y_ref, xbuf, wgb, wub, wdb, gsem):
    i = pl.program_id(0)
    slot = i % 2

    @pl.when((i == 0) | (blk_e_ref[i] != blk_e_ref[jnp.maximum(i - 1, 0)]))
    def _():
        wgb[...] = wg_ref[0].astype(BF16)
        wub[...] = wu_ref[0].astype(BF16)
        wdb[...] = wd_ref[0].astype(BF16)

    def gather(ids_ref, s, start, rows=range(MOE_BLOCK)):
        for r in rows:
            cp = _row_copy(h_hbm, ids_ref[0, 0, r], xbuf.at[s], r * ROW_TILE, gsem.at[s])
            if start:
                cp.start(priority=r % 2)
            else:
                cp.wait()

    @pl.when(i == 0)
    def _():
        gather(tok_ref, 0, True)

    gather(tok_ref, slot, False)
    xb = _load_rows(xbuf.at[slot], MOE_BLOCK).astype(BF16)
    n_slices = 4
    hw = wgb.shape[1] // n_slices
    per_slice = MOE_BLOCK // n_slices
    acc = None
    for k in range(n_slices):
        gather(tok_next_ref, 1 - slot, True, range(k * per_slice, (k + 1) * per_slice))
        gate = _dot(xb, wgb[:, k * hw:(k + 1) * hw])
        hid = gate * _sigmoid(gate) * _dot(xb, wub[:, k * hw:(k + 1) * hw])
        part = _dot(hid.astype(BF16), wdb[k * hw:(k + 1) * hw, :])
        acc = part if acc is None else acc + part
    for k in range(ROW_TILE):
        y_ref[pl.ds(k, MOE_BLOCK, stride=ROW_TILE), :] = acc[:, k * 128:(k + 1) * 128]

    @pl.when(i == pl.num_programs(0) - 1)
    def _():
        gather(tok_next_ref, 1 - slot, False)


def moe_experts(h2, blk_e, buf_tok, w_gate, w_up, w_down):
    d = w_gate.shape[1]
    n_blk = blk_e.shape[0]
    hid = w_gate.shape[2]
    tile_rows = MOE_BLOCK * ROW_TILE

    def ids(shift):
        return pl.BlockSpec((1, 1, MOE_BLOCK), lambda i, be: (jnp.clip(i + shift, 0, n_blk - 1), 0, 0),
                            memory_space=pltpu.SMEM)

    grid_spec = pltpu.PrefetchScalarGridSpec(
        num_scalar_prefetch=1,
        grid=(n_blk,),
        in_specs=[ids(0), ids(1),
                  pl.BlockSpec(memory_space=pl.ANY),
                  pl.BlockSpec((1, d, hid), lambda i, be: (be[i], 0, 0)),
                  pl.BlockSpec((1, d, hid), lambda i, be: (be[i], 0, 0)),
                  pl.BlockSpec((1, hid, d), lambda i, be: (be[i], 0, 0))],
        out_specs=pl.BlockSpec((tile_rows, d // ROW_TILE), lambda i, be: (i, 0)),
        scratch_shapes=[pltpu.VMEM((2, tile_rows, d // ROW_TILE), F32),
                        pltpu.VMEM((d, hid), BF16), pltpu.VMEM((d, hid), BF16), pltpu.VMEM((hid, d), BF16),
                        pltpu.SemaphoreType.DMA((2,))],
    )
    tok = buf_tok.reshape(n_blk, 1, MOE_BLOCK)
    return pl.pallas_call(
        _expert_kernel,
        out_shape=jax.ShapeDtypeStruct((n_blk * tile_rows, d // ROW_TILE), F32),
        grid_spec=grid_spec,
        compiler_params=_params(("arbitrary",)),
        name="moe_experts",
    )(blk_e, tok, tok, h2, w_gate, w_up, w_down)


def _combine_kernel(pos_ref, pos_next_ref, x_ref, mod_ref, wt_ref, fg_ref, y_hbm, o_ref, rows, sem, *, final_norm):
    tm = x_ref.shape[1]
    i = pl.program_id(0)
    slot = i % 2

    def gather(ids_ref, s, start):
        for r in range(tm):
            for k in range(TOP_K):
                cp = _row_copy(y_hbm, ids_ref[0, 0, TOP_K * r + k], rows.at[s, k], r * ROW_TILE, sem.at[s])
                if start:
                    cp.start(priority=r % 2)
                else:
                    cp.wait()

    @pl.when(i == 0)
    def _():
        gather(pos_ref, 0, True)

    gather(pos_ref, slot, False)
    gather(pos_next_ref, 1 - slot, True)
    wt = wt_ref[...]
    y = wt[:, 0:1] * _load_rows(rows.at[slot, 0], tm) + wt[:, 1:2] * _load_rows(rows.at[slot, 1], tm)
    out = x_ref[0] + mod_ref[0, 0, 5:6, :] * y
    if final_norm:
        out = out * lax.rsqrt(jnp.mean(out * out, axis=-1, keepdims=True) + EPS) * fg_ref[...]
    o_ref[0] = out

    @pl.when(i == pl.num_programs(0) - 1)
    def _():
        gather(pos_next_ref, 1 - slot, False)


def moe_combine(xs, mods, wts, pos, y_buf, final_g, n_ctx, final_norm):
    b, s, d = xs.shape
    tm = MOE_BLOCK
    ctx_tiles, n_tiles = n_ctx // tm, s // tm
    first = ctx_tiles if final_norm else 0
    per_batch = n_tiles - first
    n_steps = b * per_batch

    def tile_of(i):
        return i // per_batch, i % per_batch + first

    def flat(i):
        bi, j = tile_of(i)
        return bi * n_tiles + j

    def x_map(i):
        bi, j = tile_of(i)
        return (bi, j, 0)

    def mod_map(i):
        bi, j = tile_of(i)
        return (bi, (j >= ctx_tiles).astype(jnp.int32), 0, 0)

    ids = pos.reshape(b * n_tiles, 1, TOP_K * tm)
    return pl.pallas_call(
        functools.partial(_combine_kernel, final_norm=final_norm),
        out_shape=jax.ShapeDtypeStruct((b, per_batch * tm, d), F32),
        grid=(n_steps,),
        in_specs=[pl.BlockSpec((1, 1, TOP_K * tm), lambda i: (flat(i), 0, 0), memory_space=pltpu.SMEM),
                  pl.BlockSpec((1, 1, TOP_K * tm), lambda i: (flat(jnp.minimum(i + 1, n_steps - 1)), 0, 0),
                               memory_space=pltpu.SMEM),
                  pl.BlockSpec((1, tm, d), x_map),
                  pl.BlockSpec((1, 1, 6, d), mod_map),
                  pl.BlockSpec((tm, 128), lambda i: (flat(i), 0)),
                  pl.BlockSpec((1, d), lambda i: (0, 0)),
                  pl.BlockSpec(memory_space=pl.ANY)],
        out_specs=pl.BlockSpec((1, tm, d), lambda i: (i // per_batch, i % per_batch, 0)),
        scratch_shapes=[pltpu.VMEM((2, TOP_K, tm * ROW_TILE, d // ROW_TILE), F32), pltpu.SemaphoreType.DMA((2,))],
        compiler_params=_params(("arbitrary",)),
        name="moe_combine",
    )(ids, ids, xs, mods, wts, final_g, y_buf)


def moe_dispatch_plan(expert_ids):
    n = expert_ids.shape[0]
    flat_e = expert_ids.reshape(-1)
    n_assign = n * TOP_K
    iota = jnp.arange(n_assign, dtype=jnp.int32)
    _, order = lax.sort((flat_e, iota), num_keys=1, is_stable=True)
    _, rank = lax.sort((order, iota), num_keys=1, is_stable=True)
    one_hot = flat_e[:, None] == jnp.arange(N_EXPERTS, dtype=jnp.int32)[None, :]
    counts = jnp.sum(one_hot.astype(jnp.int32), axis=0)
    start = jnp.cumsum(counts) - counts
    padded = (counts + MOE_BLOCK - 1) // MOE_BLOCK * MOE_BLOCK
    pend = jnp.cumsum(padded)
    pstart = pend - padded
    pos = rank + jnp.sum(jnp.where(one_hot, (pstart - start)[None, :], 0), axis=1)
    cap = -(-n_assign // MOE_BLOCK) * MOE_BLOCK + N_EXPERTS * MOE_BLOCK
    n_blk = cap // MOE_BLOCK
    blk_first = jnp.arange(n_blk, dtype=jnp.int32) * MOE_BLOCK
    blk_e = jnp.minimum(jnp.sum((pend[None, :] <= blk_first[:, None]).astype(jnp.int32), axis=1), N_EXPERTS - 1)
    lane = jnp.arange(MOE_BLOCK, dtype=jnp.int32)[None, :]
    in_expert = (blk_first - pstart[blk_e])[:, None] + lane
    is_real = in_expert < counts[blk_e][:, None]
    assign = order[jnp.clip(start[blk_e][:, None] + in_expert, 0, n_assign - 1)]
    buf_tok = jnp.where(is_real, assign // TOP_K, 0)
    return blk_e, buf_tok.reshape(-1) * ROW_TILE, pos * ROW_TILE


def hier_moe_layer(xs, g2, mods, p, expert_weights, first_expert, final_g, n_ctx, final_norm):
    d = xs.shape[2]
    w_router = jnp.zeros((d, 128), F32).at[:, :N_GROUPS].set(p['w_router_g']) \
        .at[:, N_GROUPS:N_GROUPS + N_EXPERTS].set(p['w_router_e'])
    b_router = jnp.zeros((1, 128), F32).at[0, :N_GROUPS].set(p['b_router_g']) \
        .at[0, N_GROUPS:N_GROUPS + N_EXPERTS].set(p['b_router_e'])
    h2, idx, wts = moe_router(xs, g2, mods, jnp.stack(_split_bf16(w_router)), b_router, n_ctx)
    blk_e, buf_tok, pos = moe_dispatch_plan(idx[:, :TOP_K])
    y_buf = moe_experts(h2, blk_e + first_expert, buf_tok, *expert_weights)
    return moe_combine(xs, mods, wts, pos, y_buf, final_g, n_ctx, final_norm)


def _layer(xs, cvec, p, expert_weights, first_expert, cos_t, sin_t, final_g, n_ctx, n_batch, final_norm):
    b, s, d = xs.shape
    m = adaln_mods(cvec, p['w_ada'], p['b_ada']).reshape(cvec.shape[0], 6, d)
    mods = jnp.stack([jnp.broadcast_to(m[n_batch], (n_batch, 6, d)), m[:n_batch]], axis=1)
    w_in = p['w_in']
    o1 = RWKV_IN
    o2 = o1 + GLA_IN
    o3 = o2 + ATTN_IN
    o4 = o3 + S5_IN
    hd = HEAD_DIM
    head_order = jnp.concatenate([jnp.arange(h * hd, (h + 1) * hd) for h in (0, 2, 1, 3)])
    w_att = jnp.concatenate([w_in[:, o2:o2 + ATTN_HEADS * hd][:, head_order], w_in[:, o2 + ATTN_HEADS * hd:o3]], axis=1)
    w_mix = jnp.concatenate([w_in[:, :o2], jnp.zeros((d, GLA_IN_PAD - GLA_IN), F32), w_att, w_in[:, o3:o4]],
                            axis=1).astype(BF16)
    w_gate = w_in[:, o4:].astype(BF16)
    w_branch = p['w_branch'].at[2].set(p['w_branch'][2][head_order]).astype(BF16)
    g1 = p['norm1_g'].reshape(1, d)
    ua, ub, uatt, us5 = in_proj(xs, g1, mods, w_mix, cos_t, sin_t, n_ctx)

    yaf, yar, bonus, gate = rwkv_scan(ua, rwkv_params(p), n_ctx)
    gof = gla_scan(ub, *gla_direction_params(p, 0), n_ctx, False)
    gor = gla_scan(ub, *gla_direction_params(p, 1), n_ctx, True)
    sink = jnp.zeros((1, 128), F32).at[0, :ATTN_HEADS].set(p['attn_sink'])
    yc = window_attention(uatt, sink, n_ctx)
    ysf = s5_scan(us5, s5_direction_params(p, 0), n_ctx, False)
    ysr = s5_scan(us5, s5_direction_params(p, 1), n_ctx, True)

    row = lambda z: z.reshape(1, -1)
    branch_inputs = (yaf, yar, bonus, gate, row(p['rwkv_ln_g']), gof, gor, ub, row(p['gla_ln_g']),
                     yc, us5, ysf, ysr, row(p['s5_d']), p['s5_w_glu'].astype(BF16), row(p['s5_b_glu']))
    xs = merge_branches(xs, g1, mods, w_gate, w_branch, p['w_out'].astype(BF16),
                        branch_inputs, n_ctx)
    return hier_moe_layer(xs, p['norm2_g'].reshape(1, d), mods, p, expert_weights, first_expert, final_g, n_ctx,
                          final_norm)


_LAYER_PARAMS = ('norm1_g', 'norm2_g', 'w_ada', 'b_ada', 'w_in', 'rwkv_mu', 'rwkv_w0', 'rwkv_w1', 'rwkv_w2',
                 'rwkv_a0', 'rwkv_a1', 'rwkv_a2', 'rwkv_kk', 'rwkv_ka', 'rwkv_rk', 'rwkv_g1', 'rwkv_g2',
                 'rwkv_ln_g', 'gla_a2', 'gla_ab', 'gla_ln_g', 'attn_sink', 's5_lam_re', 's5_lam_im', 's5_log_dt',
                 's5_b_re', 's5_b_im', 's5_c_re', 's5_c_im', 's5_d', 's5_w_glu', 's5_b_glu', 'w_branch', 'w_out',
                 'w_router_g', 'b_router_g', 'w_router_e', 'b_router_e', 'w_exp_gate', 'w_exp_up', 'w_exp_down')


def kernel(x, c, ctx, c_ctx, norm1_g, norm2_g, final_norm_g, w_ada, b_ada, w_in, rwkv_mu, rwkv_w0, rwkv_w1, rwkv_w2, rwkv_a0, rwkv_a1, rwkv_a2, rwkv_kk, rwkv_ka, rwkv_rk, rwkv_g1, rwkv_g2, rwkv_ln_g, gla_a2, gla_ab, gla_ln_g, attn_sink, s5_lam_re, s5_lam_im, s5_log_dt, s5_b_re, s5_b_im, s5_c_re, s5_c_im, s5_d, s5_w_glu, s5_b_glu, w_branch, w_out, w_router_g, b_router_g, w_router_e, b_router_e, w_exp_gate, w_exp_up, w_exp_down):
    stacked = dict(zip(_LAYER_PARAMS, (
        norm1_g, norm2_g, w_ada, b_ada, w_in, rwkv_mu, rwkv_w0, rwkv_w1, rwkv_w2, rwkv_a0, rwkv_a1, rwkv_a2,
        rwkv_kk, rwkv_ka, rwkv_rk, rwkv_g1, rwkv_g2, rwkv_ln_g, gla_a2, gla_ab, gla_ln_g, attn_sink,
        s5_lam_re, s5_lam_im, s5_log_dt, s5_b_re, s5_b_im, s5_c_re, s5_c_im, s5_d, s5_w_glu, s5_b_glu,
        w_branch, w_out, w_router_g, b_router_g, w_router_e, b_router_e, w_exp_gate, w_exp_up, w_exp_down)))
    n_batch, n_lat, d = x.shape
    n_ctx = ctx.shape[1]
    depth = w_in.shape[0]
    xs = jnp.concatenate([ctx, x], axis=1)
    cvec = jnp.zeros((16, d), F32).at[:n_batch].set(c).at[n_batch].set(c_ctx)
    cos_t, sin_t = rope_tables(n_ctx, n_lat)
    final_g = final_norm_g.reshape(1, d)
    expert_names = ('w_exp_gate', 'w_exp_up', 'w_exp_down')
    expert_weights = tuple(stacked[k].reshape((-1,) + stacked[k].shape[2:]) for k in expert_names)
    for l in range(depth):
        p = {k: v[l] for k, v in stacked.items() if k not in expert_names}
        xs = _layer(xs, cvec, p, expert_weights, l * N_EXPERTS, cos_t, sin_t, final_g, n_ctx, n_batch,
                    l == depth - 1)
    return xs
```

```python
import functools
import math

import jax
import jax.numpy as jnp
from jax import lax
from jax.experimental import pallas as pl
from jax.experimental.pallas import tpu as pltpu

F32 = jnp.float32
BF16 = jnp.bfloat16
HI = lax.Precision.HIGHEST

EPS = 1e-6
GN_EPS = 64e-5
ROPE_BASE = 10000.0
NEG_INF = -1e30
GRID_W = 64

MIX_W = 256
N_BRANCH = 4
HEAD_DIM = 64
RWKV_HEADS = MIX_W // HEAD_DIM
GLA_HEADS = 4
GLA_DK = 32
GLA_DV = 64
GLA_RANK = 16
GLA_TAU = 16.0
ATTN_HEADS = 4
ATTN_KV_HEADS = 2
WINDOW = 128
ATTN_BLOCK = 128
S5_GROUPS = 16
S5_GROUP_CH = MIX_W // S5_GROUPS
S5_STATE = 64
N_GROUPS = 4
EXPERTS_PER_GROUP = 8
N_EXPERTS = N_GROUPS * EXPERTS_PER_GROUP
TOP_K = 2
MOE_BLOCK = 128
ROW_TILE = 8

RWKV_IN = 4 * MIX_W
GLA_IN = 2 * GLA_HEADS * GLA_DK + 2 * GLA_HEADS * GLA_DV + 2 * GLA_RANK
GLA_IN_PAD = 896
ATTN_IN = (ATTN_HEADS + 2 * ATTN_KV_HEADS) * HEAD_DIM
S5_IN = MIX_W
MIX_IN_PAD = RWKV_IN + GLA_IN_PAD + ATTN_IN + S5_IN

CHUNK = 64
TOKEN_TILE = 256
VMEM_LIMIT = 56 * 1024 * 1024


def _params(sem):
    return pltpu.CompilerParams(dimension_semantics=sem, vmem_limit_bytes=VMEM_LIMIT)


def _sigmoid(z):
    return 1.0 / (1.0 + jnp.exp(-z))


def _dot(a, b, prec=None):
    return jnp.dot(a, b, precision=prec, preferred_element_type=F32)


def _dot_nt(a, b, prec=None):
    return lax.dot_general(a, b, (((1,), (1,)), ((), ())), precision=prec, preferred_element_type=F32)


def _dot_tn(a, b, prec=None):
    return lax.dot_general(a, b, (((0,), (0,)), ((), ())), precision=prec, preferred_element_type=F32)


def _split_bf16(x):
    hi = x.astype(BF16)
    return hi, (x - hi.astype(F32)).astype(BF16)


def _dot_split(x, m):
    hi, lo = _split_bf16(x)
    return _dot(hi, m) + _dot(lo, m)


def _dot_split_rhs(m, x):
    hi, lo = _split_bf16(x)
    return _dot(m, hi) + _dot(m, lo)


def _seg_ones(width, seg):
    r = lax.broadcasted_iota(jnp.int32, (width, width), 0) // seg
    c = lax.broadcasted_iota(jnp.int32, (width, width), 1) // seg
    return (r == c).astype(F32)


def _order_mask(n, reverse, inclusive):
    t = lax.broadcasted_iota(jnp.int32, (n, n), 0)
    i = lax.broadcasted_iota(jnp.int32, (n, n), 1)
    if reverse:
        return (i >= t) if inclusive else (i > t)
    return (i <= t) if inclusive else (i < t)


def _norm_mod(x, g, shift, scale):
    ms = jnp.mean(x * x, axis=-1, keepdims=True)
    return (x * lax.rsqrt(ms + EPS) * g) * (1.0 + scale) + shift


def _ada_kernel(c_ref, w_ref, b_ref, o_ref):
    cv = c_ref[...]
    o_ref[...] = _dot(cv * _sigmoid(cv), w_ref[...], HI) + b_ref[...]


def adaln_mods(cvec, w_ada, b_ada):
    rows, d = cvec.shape
    n = w_ada.shape[1]
    tn = 1024
    return pl.pallas_call(
        _ada_kernel,
        out_shape=jax.ShapeDtypeStruct((rows, n), F32),
        grid=(n // tn,),
        in_specs=[pl.BlockSpec((rows, d), lambda j: (0, 0)),
                  pl.BlockSpec((d, tn), lambda j: (0, j)),
                  pl.BlockSpec((1, tn), lambda j: (0, j))],
        out_specs=pl.BlockSpec((rows, tn), lambda j: (0, j)),
        compiler_params=_params(("parallel",)),
        name="adaln_mods",
    )(cvec, w_ada, b_ada.reshape(1, n))


def _rope(z, cos, sin):
    lane = lax.broadcasted_iota(jnp.int32, z.shape, 1)
    swapped = jnp.where((lane % HEAD_DIM) < HEAD_DIM // 2,
                        pltpu.roll(z, 128 - HEAD_DIM // 2, axis=1),
                        pltpu.roll(z, HEAD_DIM // 2, axis=1))
    return z * cos + swapped * sin


def _in_proj_kernel(x_ref, g_ref, mod_ref, w_ref, cos_ref, sin_ref, oa_ref, ob_ref, oc_ref, od_ref):
    h = _norm_mod(x_ref[0], g_ref[...], mod_ref[0, 0, 0:1, :], mod_ref[0, 0, 1:2, :])
    u = _dot(h.astype(BF16), w_ref[...])
    o1 = RWKV_IN
    o2 = o1 + GLA_IN_PAD
    o3 = o2 + ATTN_IN
    oa_ref[0] = u[:, :o1]
    ob_ref[0] = u[:, o1:o2]
    od_ref[0] = u[:, o3:]
    cos, sin = cos_ref[...], sin_ref[...]
    scale = HEAD_DIM ** -0.5
    qw = ATTN_HEADS * HEAD_DIM
    kw = ATTN_KV_HEADS * HEAD_DIM
    parts = [_rope(u[:, o2 + c:o2 + c + 128], cos, sin) * scale for c in range(0, qw, 128)]
    parts += [_rope(u[:, o2 + qw + c:o2 + qw + c + 128], cos, sin) for c in range(0, kw, 128)]
    parts.append(u[:, o2 + qw + kw:o3])
    oc_ref[0] = jnp.concatenate(parts, axis=1).astype(BF16)


def in_proj(xs, g, mods, w_mix, cos_t, sin_t, n_ctx):
    b, s, d = xs.shape
    tm = TOKEN_TILE
    ctx_tiles = n_ctx // tm
    widths = (RWKV_IN, GLA_IN_PAD, ATTN_IN, S5_IN)
    dtypes = (F32, F32, BF16, F32)
    return pl.pallas_call(
        _in_proj_kernel,
        out_shape=[jax.ShapeDtypeStruct((b, s, w), t) for w, t in zip(widths, dtypes)],
        grid=(b, s // tm),
        in_specs=[pl.BlockSpec((1, tm, d), lambda i, j: (i, j, 0)),
                  pl.BlockSpec((1, d), lambda i, j: (0, 0)),
                  pl.BlockSpec((1, 1, 6, d), lambda i, j: (i, (j >= ctx_tiles).astype(jnp.int32), 0, 0)),
                  pl.BlockSpec((d, MIX_IN_PAD), lambda i, j: (0, 0)),
                  pl.BlockSpec((tm, 128), lambda i, j: (j, 0)),
                  pl.BlockSpec((tm, 128), lambda i, j: (j, 0))],
        out_specs=[pl.BlockSpec((1, tm, w), lambda i, j: (i, j, 0)) for w in widths],
        compiler_params=_params(("parallel", "parallel")),
        name="in_proj",
    )(xs, g.reshape(1, d), mods, w_mix, cos_t, sin_t)


def _attn_kernel(own_ref, prev_ref, next_ref, ctx_ref, sink_ref, band_ref, o_ref, *, ctx_blocks, n_lat):
    blk = ATTN_BLOCK
    qw = ATTN_HEADS * HEAD_DIM
    kw = ATTN_KV_HEADS * HEAD_DIM
    n = pl.program_id(1) - ctx_blocks
    own = own_ref[0]
    k_all = jnp.concatenate([prev_ref[0][:, qw:qw + kw], own[:, qw:qw + kw], next_ref[0][:, qw:qw + kw],
                             ctx_ref[0][:, qw:qw + kw]], axis=0)
    v_all = jnp.concatenate([prev_ref[0][:, qw + kw:], own[:, qw + kw:], next_ref[0][:, qw + kw:],
                             ctx_ref[0][:, qw + kw:]], axis=0)
    n_keys = k_all.shape[0]
    key_blk = lax.broadcasted_iota(jnp.int32, (1, n_keys), 1) // blk
    off = [jnp.where(ok, 0.0, NEG_INF) for ok in (n >= 1, n >= 0, (n >= 0) & ((n + 1) * blk < n_lat))]
    bias = band_ref[...] + jnp.where(key_blk == 0, off[0],
                                     jnp.where(key_blk == 1, off[1], jnp.where(key_blk == 2, off[2], 0.0)))
    low_half = lax.broadcasted_iota(jnp.int32, (blk, 128), 1) < HEAD_DIM
    top_rows = lax.broadcasted_iota(jnp.int32, (2 * blk, 1), 0) < blk
    zero = jnp.zeros((), BF16)
    outs = []
    for c in range(qw // 128):
        qc = own[:, c * 128:(c + 1) * 128]
        lhs = jnp.concatenate([jnp.where(low_half, qc, zero), jnp.where(low_half, zero, qc)], axis=0)
        s = _dot_nt(lhs, k_all) + bias
        sink = jnp.where(top_rows, sink_ref[0:1, c:c + 1], sink_ref[0:1, c + 2:c + 3])
        m = jnp.maximum(jnp.max(s, axis=-1, keepdims=True), sink)
        p = jnp.exp(s - m)
        den = jnp.sum(p, axis=-1, keepdims=True) + jnp.exp(sink - m)
        o = _dot(p.astype(BF16), v_all) / den
        outs.append(jnp.where(low_half, o[:blk], o[blk:]))
    o_ref[0] = jnp.concatenate(outs, axis=1)


def window_attention(u_att, sink, n_ctx):
    b, s, w = u_att.shape
    blk = ATTN_BLOCK
    ctx_blocks = n_ctx // blk
    nblk = s // blk
    kern = functools.partial(_attn_kernel, ctx_blocks=ctx_blocks, n_lat=s - n_ctx)
    q_in = jnp.arange(2 * blk)[:, None] % blk
    k_col = jnp.arange(3 * blk + n_ctx)[None, :]
    band = jnp.where((k_col >= 3 * blk) | (jnp.abs(q_in - (k_col - blk)) <= WINDOW), 0.0, NEG_INF).astype(F32)
    return pl.pallas_call(
        kern,
        out_shape=jax.ShapeDtypeStruct((b, s, ATTN_HEADS * HEAD_DIM), F32),
        grid=(b, nblk),
        in_specs=[pl.BlockSpec((1, blk, w), lambda i, j: (i, j, 0)),
                  pl.BlockSpec((1, blk, w), lambda i, j: (i, jnp.maximum(j - 1, 0), 0)),
                  pl.BlockSpec((1, blk, w), lambda i, j: (i, jnp.minimum(j + 1, nblk - 1), 0)),
                  pl.BlockSpec((1, n_ctx, w), lambda i, j: (i, 0, 0)),
                  pl.BlockSpec((1, 128), lambda i, j: (0, 0)),
                  pl.BlockSpec(band.shape, lambda i, j: (0, 0))],
        out_specs=pl.BlockSpec((1, blk, ATTN_HEADS * HEAD_DIM), lambda i, j: (i, j, 0)),
        compiler_params=_params(("parallel", "parallel")),
        name="window_attention",
    )(u_att, u_att, u_att, u_att, sink, band)


def rope_tables(n_ctx, n_lat):
    rows = n_lat // GRID_W
    row = jnp.repeat(jnp.arange(rows, dtype=F32), GRID_W)
    col = jnp.tile(jnp.arange(GRID_W, dtype=F32), rows)
    n_freq = HEAD_DIM // 4
    inv_freq = ROPE_BASE ** (-jnp.arange(n_freq, dtype=F32) / n_freq)
    ang = jnp.concatenate([row[:, None] * inv_freq, col[:, None] * inv_freq], axis=-1)
    cos, sin = jnp.cos(ang), jnp.sin(ang)
    cos_l = jnp.concatenate([cos, cos, cos, cos], axis=-1)
    sin_l = jnp.concatenate([-sin, sin, -sin, sin], axis=-1)
    cos_t = jnp.concatenate([jnp.ones((n_ctx, 128), F32), cos_l], axis=0)
    sin_t = jnp.concatenate([jnp.zeros((n_ctx, 128), F32), sin_l], axis=0)
    return cos_t, sin_t


def _scan_order(j, ctx_steps, n_steps, reverse):
    if not reverse:
        return j
    return jnp.where(j < ctx_steps, ctx_steps - 1 - j, n_steps - 1 - (j - ctx_steps))


S5_TILE = 128
S5_ROW_PAD = 8


def _s5_kernel(u_ref, bblk_ref, cblk_ref, ar_ref, ai_ref, y_ref, bu_ref, xs_ref, car_ref, cai_ref, *, reverse):
    nb, tm, _ = u_ref.shape
    ns = ar_ref.shape[1]
    stride = tm + S5_ROW_PAD

    @pl.when(pl.program_id(0) == 0)
    def _():
        car_ref[...] = jnp.zeros_like(car_ref)
        cai_ref[...] = jnp.zeros_like(cai_ref)

    nk = ns // 128
    for b in range(nb):
        bu = _dot(u_ref[b].astype(BF16), bblk_ref[...])
        for k in range(2 * nk):
            bu_ref[k, b * stride:b * stride + tm, :] = bu[:, k * 128:(k + 1) * 128]
    ar = [jnp.broadcast_to(ar_ref[:, k * 128:(k + 1) * 128], (nb, 128)) for k in range(nk)]
    ai = [jnp.broadcast_to(ai_ref[:, k * 128:(k + 1) * 128], (nb, 128)) for k in range(nk)]

    def step(i, carry):
        t = (tm - 1 - i) if reverse else i
        rows = pl.ds(t, nb, stride=stride)
        new = []
        for k in range(nk):
            xr, xi = carry[k], carry[nk + k]
            new.append((ar[k] * xr - ai[k] * xi + bu_ref[k, rows, :], ar[k] * xi + ai[k] * xr + bu_ref[nk + k, rows, :]))
        for k in range(nk):
            xs_ref[k, rows, :] = new[k][0]
            xs_ref[nk + k, rows, :] = new[k][1]
        return tuple(z[0] for z in new) + tuple(z[1] for z in new)

    init = tuple(car_ref[:, k * 128:(k + 1) * 128] for k in range(nk)) \
        + tuple(cai_ref[:, k * 128:(k + 1) * 128] for k in range(nk))
    final = lax.fori_loop(0, tm, step, init, unroll=4)
    for k in range(nk):
        car_ref[:, k * 128:(k + 1) * 128] = final[k]
        cai_ref[:, k * 128:(k + 1) * 128] = final[nk + k]
    for b in range(nb):
        xs = jnp.concatenate([xs_ref[k, b * stride:b * stride + tm, :] for k in range(2 * nk)], axis=1)
        y_ref[b] = _dot(xs.astype(BF16), cblk_ref[...])


def s5_scan(u, prm, n_ctx, reverse):
    b, s, w = u.shape
    tm = S5_TILE
    ns = S5_GROUPS * S5_STATE
    ctx_tiles, n_tiles = n_ctx // tm, s // tm
    bblk, cblk, ar, ai = prm

    def tile_map(j):
        return (0, _scan_order(j, ctx_tiles, n_tiles, reverse), 0)

    def const(j):
        return (0, 0)

    rows = b * (tm + S5_ROW_PAD)
    return pl.pallas_call(
        functools.partial(_s5_kernel, reverse=reverse),
        out_shape=jax.ShapeDtypeStruct((b, s, w), F32),
        grid=(n_tiles,),
        in_specs=[pl.BlockSpec((b, tm, w), tile_map),
                  pl.BlockSpec((w, 2 * ns), const), pl.BlockSpec((2 * ns, w), const),
                  pl.BlockSpec((1, ns), const), pl.BlockSpec((1, ns), const)],
        out_specs=pl.BlockSpec((b, tm, w), tile_map),
        scratch_shapes=[pltpu.VMEM((2 * ns // 128, rows, 128), F32), pltpu.VMEM((2 * ns // 128, rows, 128), F32),
                        pltpu.VMEM((b, ns), F32), pltpu.VMEM((b, ns), F32)],
        compiler_params=_params(("arbitrary",)),
        name="s5_scan_rev" if reverse else "s5_scan_fwd",
    )(u, bblk, cblk, ar, ai)


def s5_direction_params(p, d):
    g, ns, ch = S5_GROUPS, S5_STATE, S5_GROUP_CH
    lr, li = p['s5_lam_re'][d], p['s5_lam_im'][d]
    dt = jnp.exp(p['s5_log_dt'][d])[:, None]

    def power(n):
        mag = jnp.exp(n * lr * dt)
        return mag * jnp.cos(n * li * dt), mag * jnp.sin(n * li * dt)

    ar, ai = power(1.0)
    den = lr * lr + li * li
    cr = ((ar - 1.0) * lr + ai * li) / den
    ci = (ai * lr - (ar - 1.0) * li) / den
    bre, bim = p['s5_b_re'], p['s5_b_im']
    bbr = cr[..., None] * bre - ci[..., None] * bim
    bbi = cr[..., None] * bim + ci[..., None] * bre
    eye = jnp.eye(g, dtype=F32)

    def expand_in(z):
        return jnp.einsum('gpc,gh->gchp', z, eye).reshape(g * ch, g * ns)

    def expand_out(z):
        return jnp.einsum('gcp,gh->gphc', z, eye).reshape(g * ns, g * ch)

    bblk = jnp.concatenate([expand_in(bbr), expand_in(bbi)], axis=1).astype(BF16)
    cblk = jnp.concatenate([expand_out(p['s5_c_re'][d]), -expand_out(p['s5_c_im'][d])], axis=0).astype(BF16)
    return bblk, cblk, ar.reshape(1, g * ns), ai.reshape(1, g * ns)


def _log_sigmoid(z):
    return jnp.minimum(z, 0.0) - jnp.log(1.0 + jnp.exp(-jnp.abs(z)))


def _gla_kernel(u_ref, a2_ref, ab_ref, o_ref, st_ref, *, reverse):
    tm = u_ref.shape[1]
    qk = GLA_HEADS * GLA_DK
    vw = GLA_HEADS * GLA_DV

    @pl.when(pl.program_id(1) == 0)
    def _():
        st_ref[...] = jnp.zeros_like(st_ref)

    u = u_ref[0]
    alow = u[:, 2 * qk + 2 * vw:]
    log_a = _log_sigmoid(_dot(alow, a2_ref[...], HI) + ab_ref[...]) * (1.0 / GLA_TAU)
    ti = lax.broadcasted_iota(jnp.int32, (tm, tm), 0)
    tj = lax.broadcasted_iota(jnp.int32, (tm, tm), 1)
    same_chunk = (ti // CHUNK) == (tj // CHUNK)
    before = (tj >= ti) if reverse else (tj <= ti)
    cum = _dot_split_rhs((same_chunk & before).astype(BF16), log_a)
    tot = _dot_split_rhs(same_chunk.astype(BF16), log_a)
    q_dec = u[:, 0:qk] * (GLA_DK ** -0.5) * jnp.exp(cum)
    k = u[:, qk:2 * qk]
    k_inv = (k * jnp.exp(-cum)).astype(BF16)
    k_end = k * jnp.exp(tot - cum)
    decay = jnp.exp(tot)
    v_b = u[:, 2 * qk:2 * qk + vw].astype(BF16)
    head_of_lane = lax.broadcasted_iota(jnp.int32, (tm, qk), 1) // GLA_DK
    q_m = [jnp.where(head_of_lane == h, q_dec, 0.0).astype(BF16) for h in range(GLA_HEADS)]
    ke_m = [jnp.where(head_of_lane == h, k_end, 0.0).astype(BF16) for h in range(GLA_HEADS)]

    incl = _order_mask(CHUNK, reverse, True)
    n_chunks = tm // CHUNK
    units = [(c, h) for c in range(n_chunks) for h in range(GLA_HEADS)]

    def rows_of(c):
        return slice(c * CHUNK, (c + 1) * CHUNK)

    vh = {(c, h): v_b[rows_of(c), h * GLA_DV:(h + 1) * GLA_DV] for c, h in units}
    scores = {(c, h): jnp.where(incl, _dot_nt(q_m[h][rows_of(c)], k_inv[rows_of(c)]), 0.0).astype(BF16)
              for c, h in units}
    intra = {u_: _dot(scores[u_], vh[u_]) for u_ in units}
    vk = {(c, h): _dot_tn(vh[c, h], ke_m[h][rows_of(c)]) for c, h in units}

    heads = range(GLA_HEADS)
    order = range(n_chunks - 1, -1, -1) if reverse else range(n_chunks)
    for c in order:
        st = [st_ref[h] for h in heads]
        o_ref[0, rows_of(c), :] = jnp.concatenate(
            [intra[c, h] + _dot_nt(q_m[h][rows_of(c)], st[h].astype(BF16)) for h in heads], axis=1)
        for h in heads:
            st_ref[h] = st[h] * decay[c * CHUNK:c * CHUNK + 1, :] + vk[c, h]


def gla_scan(u, a2pad, ab, n_ctx, reverse):
    b, s, w = u.shape
    tm = TOKEN_TILE
    ctx_tiles, n_tiles = n_ctx // tm, s // tm
    qk = GLA_HEADS * GLA_DK

    def tile_map(i, j):
        return (i, _scan_order(j, ctx_tiles, n_tiles, reverse), 0)

    return pl.pallas_call(
        functools.partial(_gla_kernel, reverse=reverse),
        out_shape=jax.ShapeDtypeStruct((b, s, GLA_HEADS * GLA_DV), F32),
        grid=(b, n_tiles),
        in_specs=[pl.BlockSpec((1, tm, w), tile_map),
                  pl.BlockSpec((128, qk), lambda i, j: (0, 0)),
                  pl.BlockSpec((1, qk), lambda i, j: (0, 0))],
        out_specs=pl.BlockSpec((1, tm, GLA_HEADS * GLA_DV), tile_map),
        scratch_shapes=[pltpu.VMEM((GLA_HEADS, GLA_DV, GLA_HEADS * GLA_DK), F32)],
        compiler_params=_params(("parallel", "arbitrary")),
        name="gla_scan_rev" if reverse else "gla_scan_fwd",
    )(u, a2pad, ab)


def gla_direction_params(p, d):
    a2pad = jnp.zeros((128, GLA_HEADS * GLA_DK), F32).at[d * GLA_RANK:(d + 1) * GLA_RANK].set(p['gla_a2'][d])
    return a2pad, p['gla_ab'][d].reshape(1, -1)


def _rwkv_tile_terms(u_ref, up_ref, un_ref, pos, shared, wdir, d, reverse, ctx_tiles, n_tiles, aux_refs):
    mu_ref, kkp_ref, ka_ref, rk_ref, g1_ref, g2_ref = shared
    w0_ref, w1_ref, w2_ref, a0_ref, a1_ref, a2_ref = wdir
    tm = u_ref.shape[1]
    w = MIX_W
    first = (pos == 0) | (pos == ctx_tiles)
    last_tile = (pos == ctx_tiles - 1) | (pos == n_tiles - 1)
    prev_row = jnp.where(first, 0.0, up_ref[0, 7:8, :])
    next_row = jnp.where(last_tile, 0.0, un_ref[0, 0:1, :])
    u = u_ref[0]
    row = lax.broadcasted_iota(jnp.int32, u.shape, 0)
    down = jnp.where(row == 0, prev_row, pltpu.roll(u, 1, axis=0))
    up = jnp.where(row == tm - 1, next_row, pltpu.roll(u, tm - 1, axis=0))
    mixed = u + (0.5 * (down + up) - u) * mu_ref[...]
    r, k, v, xa = mixed[:, 0:w], mixed[:, w:2 * w], mixed[:, 2 * w:3 * w], mixed[:, 3 * w:4 * w]
    seg = _seg_ones(w, HEAD_DIM).astype(BF16)
    kk = k * kkp_ref[...]
    kk = kk * lax.rsqrt(_dot_split(kk * kk, seg) + EPS)
    xab = xa.astype(BF16)
    zw = w0_ref[d] + _dot(jnp.tanh(_dot(xab, w1_ref[d])).astype(BF16), w2_ref[d])
    log_w = -math.exp(-0.5) * _sigmoid(zw)
    iclr = _sigmoid(a0_ref[d] + _dot(_dot(xab, a1_ref[d]).astype(BF16), a2_ref[d]))
    k_eff = k * (1.0 + (iclr - 1.0) * ka_ref[...])
    a_vec = -kk
    b_vec = kk * iclr
    if aux_refs is not None:
        bonus_ref, gate_ref = aux_refs
        gate_ref[0] = _dot(_sigmoid(_dot(xab, g1_ref[...])).astype(BF16), g2_ref[...])
        bonus_ref[0] = _dot_split(r * k * rk_ref[...], seg) * v

    ti = lax.broadcasted_iota(jnp.int32, (tm, tm), 0)
    tj = lax.broadcasted_iota(jnp.int32, (tm, tm), 1)
    same_chunk = (ti // CHUNK) == (tj // CHUNK)
    before = (tj >= ti) if reverse else (tj <= ti)
    cum = _dot_split_rhs((same_chunk & before).astype(BF16), log_w)
    tot = _dot_split_rhs(same_chunk.astype(BF16), log_w)
    e_neg = jnp.exp(-cum)
    e_end = jnp.exp(tot - cum)
    lane = lax.broadcasted_iota(jnp.int32, (tm, w), 1)
    even = (lane // HEAD_DIM) % 2 == 0

    def by_parity(z):
        return jnp.where(even, z, 0.0).astype(BF16), jnp.where(even, 0.0, z).astype(BF16)

    return dict(a_par=by_parity(a_vec * jnp.exp(cum - log_w)), r_par=by_parity(r * jnp.exp(cum)),
                be_par=by_parity(b_vec * e_end), ke_par=by_parity(k_eff * e_end),
                b_b=(b_vec * e_neg).astype(BF16), k_b=(k_eff * e_neg).astype(BF16), v_b=v.astype(BF16),
                decay=jnp.exp(tot))


def _rwkv_rows(c):
    return slice(c * CHUNK, (c + 1) * CHUNK)


def _rwkv_lanes(h):
    return slice((h // 2) * 128, (h // 2 + 1) * 128)


def _rwkv_chunk_terms(terms, reverse, n_chunks):
    units = [(c, h) for c in range(n_chunks) for h in range(RWKV_HEADS)]
    t_i = lax.broadcasted_iota(jnp.int32, (CHUNK, CHUNK), 0)
    t_j = lax.broadcasted_iota(jnp.int32, (CHUNK, CHUNK), 1)
    incl_t = (t_i >= t_j) if reverse else (t_i <= t_j)
    strict_t = (t_i > t_j) if reverse else (t_i < t_j)
    eye = (t_i == t_j).astype(F32)
    n_doubling = int(math.log2(CHUNK)) - 1

    def sl(name, c, h):
        return terms[name][h % 2][_rwkv_rows(c), _rwkv_lanes(h)]

    lhs = {(c, h): jnp.concatenate([sl('a_par', c, h), sl('r_par', c, h)], axis=0) for c, h in units}
    rhs = {(c, h): jnp.concatenate([terms['b_b'][_rwkv_rows(c), _rwkv_lanes(h)],
                                    terms['k_b'][_rwkv_rows(c), _rwkv_lanes(h)]], axis=0) for c, h in units}
    m_t = {u_: _dot_nt(rhs[u_], lhs[u_]) for u_ in units}
    a_ab_t = {u_: jnp.where(strict_t, m_t[u_][:CHUNK, :CHUNK], 0.0) for u_ in units}
    a_rb_t = {u_: jnp.where(incl_t, m_t[u_][:CHUNK, CHUNK:], 0.0).astype(BF16) for u_ in units}
    a_ak_t = {u_: jnp.where(strict_t, m_t[u_][CHUNK:, :CHUNK], 0.0).astype(BF16) for u_ in units}
    a_rk_t = {u_: jnp.where(incl_t, m_t[u_][CHUNK:, CHUNK:], 0.0).astype(BF16) for u_ in units}
    inv = {u_: eye + a_ab_t[u_] for u_ in units}
    pw = a_ab_t
    for _ in range(n_doubling):
        pwb = {u_: pw[u_].astype(BF16) for u_ in units}
        pw = {u_: _dot(pwb[u_], pwb[u_]) for u_ in units}
        inv = {u_: inv[u_] + _dot(inv[u_].astype(BF16), pw[u_].astype(BF16)) for u_ in units}
    vh = {(c, h): terms['v_b'][_rwkv_rows(c), h * HEAD_DIM:(h + 1) * HEAD_DIM] for c, h in units}
    return dict(lhs=lhs, inv_t={u_: inv[u_].astype(BF16) for u_ in units}, a_rb_t=a_rb_t,
                akv_t={u_: _dot_tn(vh[u_], a_ak_t[u_]) for u_ in units},
                arkv_t={u_: _dot_tn(vh[u_], a_rk_t[u_]) for u_ in units},
                vk={(c, h): _dot_tn(vh[c, h], sl('ke_par', c, h)) for c, h in units},
                be={(c, h): sl('be_par', c, h) for c, h in units})


def _rwkv_kernel(uf_ref, ufp_ref, ufn_ref, ur_ref, urp_ref, urn_ref, mu_ref, kkp_ref, ka_ref, rk_ref, g1_ref, g2_ref,
                 w0_ref, w1_ref, w2_ref, a0_ref, a1_ref, a2_ref, yf_ref, yr_ref, bonus_ref, gate_ref, st_ref,
                 *, ctx_tiles, n_tiles):
    tm = uf_ref.shape[1]
    n_chunks = tm // CHUNK
    j = pl.program_id(1)

    @pl.when(j == 0)
    def _():
        st_ref[...] = jnp.zeros_like(st_ref)

    shared = (mu_ref, kkp_ref, ka_ref, rk_ref, g1_ref, g2_ref)
    wdir = (w0_ref, w1_ref, w2_ref, a0_ref, a1_ref, a2_ref)
    pos = [_scan_order(j, ctx_tiles, n_tiles, rev) for rev in (False, True)]
    terms = [_rwkv_tile_terms(uf_ref, ufp_ref, ufn_ref, pos[0], shared, wdir, 0, False, ctx_tiles, n_tiles,
                              (bonus_ref, gate_ref)),
             _rwkv_tile_terms(ur_ref, urp_ref, urn_ref, pos[1], shared, wdir, 1, True, ctx_tiles, n_tiles, None)]
    pre = [_rwkv_chunk_terms(terms[d], d == 1, n_chunks) for d in range(2)]
    y_refs = (yf_ref, yr_ref)

    chains = [(d, h) for d in range(2) for h in range(RWKV_HEADS)]
    for step in range(n_chunks):
        chunk = {0: step, 1: n_chunks - 1 - step}
        st = {(d, h): st_ref[d, h] for d, h in chains}
        x0_t = {(d, h): _dot_nt(st[d, h].astype(BF16), pre[d]['lhs'][chunk[d], h]) for d, h in chains}
        u_t = {(d, h): _dot((x0_t[d, h][:, :CHUNK] + pre[d]['akv_t'][chunk[d], h]).astype(BF16),
                            pre[d]['inv_t'][chunk[d], h]).astype(BF16) for d, h in chains}
        for d, h in chains:
            c = chunk[d]
            st_ref[d, h] = (st[d, h] * terms[d]['decay'][c * CHUNK:c * CHUNK + 1, _rwkv_lanes(h)]
                            + _dot(u_t[d, h], pre[d]['be'][c, h]) + pre[d]['vk'][c, h])
        for d in range(2):
            c = chunk[d]
            y_t = [x0_t[d, h][:, CHUNK:] + _dot(u_t[d, h], pre[d]['a_rb_t'][c, h]) + pre[d]['arkv_t'][c, h]
                   for h in range(RWKV_HEADS)]
            y_refs[d][0, _rwkv_rows(c), :] = jnp.concatenate([z.T for z in y_t], axis=1)


def rwkv_scan(u, prm, n_ctx):
    b, s, w4 = u.shape
    tm = TOKEN_TILE
    w = MIX_W
    ctx_tiles, n_tiles = n_ctx // tm, s // tm
    halo_blocks = tm // 8

    def maps(reverse):
        def tile_map(i, j):
            return (i, _scan_order(j, ctx_tiles, n_tiles, reverse), 0)

        def up_map(i, j):
            return (i, jnp.maximum(_scan_order(j, ctx_tiles, n_tiles, reverse) * halo_blocks - 1, 0), 0)

        def un_map(i, j):
            return (i, jnp.minimum((_scan_order(j, ctx_tiles, n_tiles, reverse) + 1) * halo_blocks, s // 8 - 1), 0)

        return tile_map, up_map, un_map

    def full(a):
        return pl.BlockSpec(a.shape, lambda i, j: (0,) * a.ndim)

    fwd, rev = maps(False), maps(True)
    in_specs = []
    for m in (fwd, rev):
        in_specs += [pl.BlockSpec((1, tm, w4), m[0]), pl.BlockSpec((1, 8, w4), m[1]), pl.BlockSpec((1, 8, w4), m[2])]
    return pl.pallas_call(
        functools.partial(_rwkv_kernel, ctx_tiles=ctx_tiles, n_tiles=n_tiles),
        out_shape=[jax.ShapeDtypeStruct((b, s, w), F32)] * 4,
        grid=(b, n_tiles),
        in_specs=in_specs + [full(a) for a in prm],
        out_specs=[pl.BlockSpec((1, tm, w), fwd[0]), pl.BlockSpec((1, tm, w), rev[0]),
                   pl.BlockSpec((1, tm, w), fwd[0]), pl.BlockSpec((1, tm, w), fwd[0])],
        scratch_shapes=[pltpu.VMEM((2, RWKV_HEADS, HEAD_DIM, 2 * HEAD_DIM), F32)],
        compiler_params=_params(("parallel", "arbitrary")),
        name="rwkv_scan",
    )(u, u, u, u, u, u, *prm)


def rwkv_params(p):
    row = lambda z: z.reshape(1, -1)
    rows2 = lambda z: z.reshape(2, 1, -1)
    bf = lambda z: z.astype(BF16)
    return (row(p['rwkv_mu']), row(p['rwkv_kk']), row(p['rwkv_ka']), row(p['rwkv_rk']),
            bf(p['rwkv_g1']), bf(p['rwkv_g2']),
            rows2(p['rwkv_w0']), bf(p['rwkv_w1']), bf(p['rwkv_w2']),
            rows2(p['rwkv_a0']), bf(p['rwkv_a1']), bf(p['rwkv_a2']))


def _gelu_tanh(y):
    return 0.5 * y * (1.0 + jnp.tanh(math.sqrt(2.0 / math.pi) * (y + 0.044715 * (y * y * y))))


def _merge_kernel(x_ref, g_ref, mod_ref, wgate_ref, wbr_ref, wout_ref,
                  yaf_ref, yar_ref, bonus_ref, gate_ref, lna_ref,
                  gof_ref, gor_ref, ub_ref, lnb_ref,
                  yc_ref, us5_ref, ysf_ref, ysr_ref, s5d_ref, wglu_ref, bglu_ref, o_ref):
    d = x_ref.shape[2]
    x = x_ref[0]
    h = _norm_mod(x, g_ref[...], mod_ref[0, 0, 0:1, :], mod_ref[0, 0, 1:2, :]).astype(BF16)
    seg = _seg_ones(MIX_W, HEAD_DIM).astype(BF16)

    def head_mean(z):
        return _dot_split(z, seg) * (1.0 / HEAD_DIM)

    y = yaf_ref[0] + yar_ref[0]
    yc0 = y - head_mean(y)
    ya = (yc0 * lax.rsqrt(head_mean(yc0 * yc0) + GN_EPS) * lna_ref[...] + bonus_ref[0]) * gate_ref[0]

    o = gof_ref[0] + gor_ref[0]
    qk = GLA_HEADS * GLA_DK
    r = ub_ref[0][:, 2 * qk + MIX_W:2 * qk + 2 * MIX_W]
    yb = o * lax.rsqrt(head_mean(o * o) + EPS) * lnb_ref[...] * (r * _sigmoid(r))

    yd = _gelu_tanh(s5d_ref[...] * us5_ref[0] + ysf_ref[0] + ysr_ref[0])
    yd = yd * _sigmoid(_dot(yd.astype(BF16), wglu_ref[...]) + bglu_ref[...])

    merged = None
    for i, yi in enumerate((ya, yb, yc_ref[0], yd)):
        gate_i = _sigmoid(_dot(h, wgate_ref[:, i * d:(i + 1) * d]))
        term = gate_i * _dot(yi.astype(BF16), wbr_ref[i])
        merged = term if merged is None else merged + term
    o_ref[0] = x + mod_ref[0, 0, 2:3, :] * _dot(merged.astype(BF16), wout_ref[...])


def merge_branches(xs, g, mods, w_gate, w_branch, w_out, branch_inputs, n_ctx):
    b, s, d = xs.shape
    tm = TOKEN_TILE
    ctx_tiles = n_ctx // tm
    (yaf, yar, bonus, gate, lna, gof, gor, ub, lnb, yc, us5, ysf, ysr, s5d, wglu, bglu) = branch_inputs

    def tok(a):
        return pl.BlockSpec((1, tm, a.shape[2]), lambda i, j: (i, j, 0))

    def full(a):
        return pl.BlockSpec(a.shape, lambda i, j: (0,) * a.ndim)

    mod_spec = pl.BlockSpec((1, 1, 6, d), lambda i, j: (i, (j >= ctx_tiles).astype(jnp.int32), 0, 0))
    return pl.pallas_call(
        _merge_kernel,
        out_shape=jax.ShapeDtypeStruct((b, s, d), F32),
        grid=(b, s // tm),
        in_specs=[tok(xs), full(g), mod_spec, full(w_gate), full(w_branch), full(w_out),
                  tok(yaf), tok(yar), tok(bonus), tok(gate), full(lna),
                  tok(gof), tok(gor), tok(ub), full(lnb),
                  tok(yc), tok(us5), tok(ysf), tok(ysr), full(s5d), full(wglu), full(bglu)],
        out_specs=tok(xs),
        compiler_params=_params(("parallel", "parallel")),
        name="merge_branches",
    )(xs, g, mods, w_gate, w_branch, w_out, yaf, yar, bonus, gate, lna, gof, gor, ub, lnb,
      yc, us5, ysf, ysr, s5d, wglu, bglu)


def _router_kernel(x_ref, g_ref, mod_ref, wr_ref, br_ref, h_ref, idx_ref, wt_ref):
    h = _norm_mod(x_ref[0], g_ref[...], mod_ref[0, 0, 3:4, :], mod_ref[0, 0, 4:5, :])
    for k in range(h.shape[1] // 128):
        h_ref[pl.ds(k, h.shape[0], stride=ROW_TILE), :] = h[:, k * 128:(k + 1) * 128]
    h_hi, h_lo = _split_bf16(h)
    logits = _dot(h_hi, wr_ref[0]) + (_dot(h_lo, wr_ref[0]) + _dot(h_hi, wr_ref[1])) + br_ref[...]
    lane = lax.broadcasted_iota(jnp.int32, logits.shape, 1)
    big = jnp.int32(1 << 20)

    def first_max(z):
        m = jnp.max(z, axis=-1, keepdims=True)
        return m, jnp.min(jnp.where(z == m, lane, big), axis=-1, keepdims=True)

    gl = jnp.where(lane < N_GROUPS, logits, -jnp.inf)
    gmax, gsel = first_max(gl)
    g_w = 1.0 / jnp.sum(jnp.exp(gl - gmax), axis=-1, keepdims=True)
    lo = N_GROUPS + EXPERTS_PER_GROUP * gsel
    el = jnp.where((lane >= lo) & (lane < lo + EXPERTS_PER_GROUP), logits, -jnp.inf)
    v1, i1 = first_max(el)
    v2, i2 = first_max(jnp.where(lane == i1, -jnp.inf, el))
    e2 = jnp.exp(v2 - v1)
    w1 = g_w / (1.0 + e2)
    w2 = g_w * e2 / (1.0 + e2)
    idx_ref[...] = jnp.where(lane == 0, i1 - N_GROUPS, jnp.where(lane == 1, i2 - N_GROUPS, 0))
    wt_ref[...] = jnp.where(lane == 0, w1, jnp.where(lane == 1, w2, 0.0))


def moe_router(xs, g, mods, w_router, b_router, n_ctx):
    b, s, d = xs.shape
    tm = TOKEN_TILE
    ctx_tiles, n_tiles = n_ctx // tm, s // tm
    n = b * s
    mod_spec = pl.BlockSpec((1, 1, 6, d), lambda i, j: (i, (j >= ctx_tiles).astype(jnp.int32), 0, 0))
    flat = lambda i, j: (i * n_tiles + j, 0)
    return pl.pallas_call(
        _router_kernel,
        out_shape=[jax.ShapeDtypeStruct((n * ROW_TILE, d // ROW_TILE), F32),
                   jax.ShapeDtypeStruct((n, 128), jnp.int32), jax.ShapeDtypeStruct((n, 128), F32)],
        grid=(b, n_tiles),
        in_specs=[pl.BlockSpec((1, tm, d), lambda i, j: (i, j, 0)),
                  pl.BlockSpec((1, d), lambda i, j: (0, 0)), mod_spec,
                  pl.BlockSpec((2, d, 128), lambda i, j: (0, 0, 0)),
                  pl.BlockSpec((1, 128), lambda i, j: (0, 0))],
        out_specs=[pl.BlockSpec((tm * ROW_TILE, d // ROW_TILE), flat), pl.BlockSpec((tm, 128), flat),
                   pl.BlockSpec((tm, 128), flat)],
        compiler_params=_params(("parallel", "parallel")),
        name="moe_router",
    )(xs, g, mods, w_router, b_router)


def _row_copy(src, src_row, dst, dst_row, sem):
    return pltpu.make_async_copy(src.at[pl.ds(pl.multiple_of(src_row, ROW_TILE), ROW_TILE)],
                                 dst.at[pl.ds(pl.multiple_of(dst_row, ROW_TILE), ROW_TILE)], sem)


def _load_rows(ref, n_rows):
    return jnp.concatenate([ref[pl.ds(k, n_rows, stride=ROW_TILE), :] for k in range(ROW_TILE)], axis=1)


def _expert_kernel(blk_e_ref, tok_ref, tok_next_ref, tok_next2_ref, h_hbm, wg_ref, wu_ref, wd_ref, y_ref,
                   xbuf, wgb, wub, wdb, gsem):
    i = pl.program_id(0)
    slot = i % 3
    ahead = (i + 2) % 3

    @pl.when((i == 0) | (blk_e_ref[i] != blk_e_ref[jnp.maximum(i - 1, 0)]))
    def _():
        wgb[...] = wg_ref[0].astype(BF16)
        wub[...] = wu_ref[0].astype(BF16)
        wdb[...] = wd_ref[0].astype(BF16)

    def gather(ids_ref, s, start, rows=range(MOE_BLOCK)):
        for r in rows:
            cp = _row_copy(h_hbm, ids_ref[0, 0, r], xbuf.at[s], r * ROW_TILE, gsem.at[s])
            if start:
                cp.start(priority=r % 2)
            else:
                cp.wait()

    @pl.when(i == 0)
    def _():
        gather(tok_ref, 0, True)
        gather(tok_next_ref, 1, True)

    gather(tok_ref, slot, False)
    xb = _load_rows(xbuf.at[slot], MOE_BLOCK).astype(BF16)
    n_slices = 4
    hw = wgb.shape[1] // n_slices
    per_slice = MOE_BLOCK // n_slices
    acc = None
    for k in range(n_slices):
        gather(tok_next2_ref, ahead, True, range(k * per_slice, (k + 1) * per_slice))
        gate = _dot(xb, wgb[:, k * hw:(k + 1) * hw])
        hid = gate * _sigmoid(gate) * _dot(xb, wub[:, k * hw:(k + 1) * hw])
        part = _dot(hid.astype(BF16), wdb[k * hw:(k + 1) * hw, :])
        acc = part if acc is None else acc + part
    for k in range(ROW_TILE):
        y_ref[pl.ds(k, MOE_BLOCK, stride=ROW_TILE), :] = acc[:, k * 128:(k + 1) * 128]

    @pl.when(i == pl.num_programs(0) - 1)
    def _():
        gather(tok_next_ref, (i + 1) % 3, False)
        gather(tok_next2_ref, ahead, False)


def moe_experts(h2, blk_e, buf_tok, w_gate, w_up, w_down):
    d = w_gate.shape[1]
    n_blk = blk_e.shape[0]
    hid = w_gate.shape[2]
    tile_rows = MOE_BLOCK * ROW_TILE

    def ids(shift):
        return pl.BlockSpec((1, 1, MOE_BLOCK), lambda i, be: (jnp.clip(i + shift, 0, n_blk - 1), 0, 0),
                            memory_space=pltpu.SMEM)

    grid_spec = pltpu.PrefetchScalarGridSpec(
        num_scalar_prefetch=1,
        grid=(n_blk,),
        in_specs=[ids(0), ids(1), ids(2),
                  pl.BlockSpec(memory_space=pl.ANY),
                  pl.BlockSpec((1, d, hid), lambda i, be: (be[i], 0, 0)),
                  pl.BlockSpec((1, d, hid), lambda i, be: (be[i], 0, 0)),
                  pl.BlockSpec((1, hid, d), lambda i, be: (be[i], 0, 0))],
        out_specs=pl.BlockSpec((tile_rows, d // ROW_TILE), lambda i, be: (i, 0)),
        scratch_shapes=[pltpu.VMEM((3, tile_rows, d // ROW_TILE), F32),
                        pltpu.VMEM((d, hid), BF16), pltpu.VMEM((d, hid), BF16), pltpu.VMEM((hid, d), BF16),
                        pltpu.SemaphoreType.DMA((3,))],
    )
    tok = buf_tok.reshape(n_blk, 1, MOE_BLOCK)
    return pl.pallas_call(
        _expert_kernel,
        out_shape=jax.ShapeDtypeStruct((n_blk * tile_rows, d // ROW_TILE), F32),
        grid_spec=grid_spec,
        compiler_params=_params(("arbitrary",)),
        name="moe_experts",
    )(blk_e, tok, tok, tok, h2, w_gate, w_up, w_down)


def _combine_kernel(pos_ref, pos_next_ref, x_ref, mod_ref, wt_ref, fg_ref, y_hbm, o_ref, rows, sem, *, final_norm):
    tm = x_ref.shape[1]
    i = pl.program_id(0)
    slot = i % 2

    def gather(ids_ref, s, start):
        for r in range(tm):
            for k in range(TOP_K):
                cp = _row_copy(y_hbm, ids_ref[0, 0, TOP_K * r + k], rows.at[s, k], r * ROW_TILE, sem.at[s])
                if start:
                    cp.start(priority=r % 2)
                else:
                    cp.wait()

    @pl.when(i == 0)
    def _():
        gather(pos_ref, 0, True)

    gather(pos_ref, slot, False)
    gather(pos_next_ref, 1 - slot, True)
    wt = wt_ref[...]
    y = wt[:, 0:1] * _load_rows(rows.at[slot, 0], tm) + wt[:, 1:2] * _load_rows(rows.at[slot, 1], tm)
    out = x_ref[0] + mod_ref[0, 0, 5:6, :] * y
    if final_norm:
        out = out * lax.rsqrt(jnp.mean(out * out, axis=-1, keepdims=True) + EPS) * fg_ref[...]
    o_ref[0] = out

    @pl.when(i == pl.num_programs(0) - 1)
    def _():
        gather(pos_next_ref, 1 - slot, False)


def moe_combine(xs, mods, wts, pos, y_buf, final_g, n_ctx, final_norm):
    b, s, d = xs.shape
    tm = MOE_BLOCK
    ctx_tiles, n_tiles = n_ctx // tm, s // tm
    first = ctx_tiles if final_norm else 0
    per_batch = n_tiles - first
    n_steps = b * per_batch

    def tile_of(i):
        return i // per_batch, i % per_batch + first

    def flat(i):
        bi, j = tile_of(i)
        return bi * n_tiles + j

    def x_map(i):
        bi, j = tile_of(i)
        return (bi, j, 0)

    def mod_map(i):
        bi, j = tile_of(i)
        return (bi, (j >= ctx_tiles).astype(jnp.int32), 0, 0)

    ids = pos.reshape(b * n_tiles, 1, TOP_K * tm)
    return pl.pallas_call(
        functools.partial(_combine_kernel, final_norm=final_norm),
        out_shape=jax.ShapeDtypeStruct((b, per_batch * tm, d), F32),
        grid=(n_steps,),
        in_specs=[pl.BlockSpec((1, 1, TOP_K * tm), lambda i: (flat(i), 0, 0), memory_space=pltpu.SMEM),
                  pl.BlockSpec((1, 1, TOP_K * tm), lambda i: (flat(jnp.minimum(i + 1, n_steps - 1)), 0, 0),
                               memory_space=pltpu.SMEM),
                  pl.BlockSpec((1, tm, d), x_map),
                  pl.BlockSpec((1, 1, 6, d), mod_map),
                  pl.BlockSpec((tm, 128), lambda i: (flat(i), 0)),
                  pl.BlockSpec((1, d), lambda i: (0, 0)),
                  pl.BlockSpec(memory_space=pl.ANY)],
        out_specs=pl.BlockSpec((1, tm, d), lambda i: (i // per_batch, i % per_batch, 0)),
        scratch_shapes=[pltpu.VMEM((2, TOP_K, tm * ROW_TILE, d // ROW_TILE), F32), pltpu.SemaphoreType.DMA((2,))],
        compiler_params=_params(("arbitrary",)),
        name="moe_combine",
    )(ids, ids, xs, mods, wts, final_g, y_buf)


def moe_dispatch_plan(expert_ids):
    n = expert_ids.shape[0]
    flat_e = expert_ids.reshape(-1)
    n_assign = n * TOP_K
    iota = jnp.arange(n_assign, dtype=jnp.int32)
    _, order = lax.sort((flat_e, iota), num_keys=1, is_stable=True)
    _, rank = lax.sort((order, iota), num_keys=1, is_stable=True)
    one_hot = flat_e[:, None] == jnp.arange(N_EXPERTS, dtype=jnp.int32)[None, :]
    counts = jnp.sum(one_hot.astype(jnp.int32), axis=0)
    start = jnp.cumsum(counts) - counts
    padded = (counts + MOE_BLOCK - 1) // MOE_BLOCK * MOE_BLOCK
    pend = jnp.cumsum(padded)
    pstart = pend - padded
    pos = rank + jnp.sum(jnp.where(one_hot, (pstart - start)[None, :], 0), axis=1)
    cap = -(-n_assign // MOE_BLOCK) * MOE_BLOCK + N_EXPERTS * MOE_BLOCK
    n_blk = cap // MOE_BLOCK
    blk_first = jnp.arange(n_blk, dtype=jnp.int32) * MOE_BLOCK
    blk_e = jnp.minimum(jnp.sum((pend[None, :] <= blk_first[:, None]).astype(jnp.int32), axis=1), N_EXPERTS - 1)
    lane = jnp.arange(MOE_BLOCK, dtype=jnp.int32)[None, :]
    in_expert = (blk_first - pstart[blk_e])[:, None] + lane
    is_real = in_expert < counts[blk_e][:, None]
    assign = order[jnp.clip(start[blk_e][:, None] + in_expert, 0, n_assign - 1)]
    buf_tok = jnp.where(is_real, assign // TOP_K, 0)
    return blk_e, buf_tok.reshape(-1) * ROW_TILE, pos * ROW_TILE


def hier_moe_layer(xs, g2, mods, p, expert_weights, first_expert, final_g, n_ctx, final_norm):
    d = xs.shape[2]
    w_router = jnp.zeros((d, 128), F32).at[:, :N_GROUPS].set(p['w_router_g']) \
        .at[:, N_GROUPS:N_GROUPS + N_EXPERTS].set(p['w_router_e'])
    b_router = jnp.zeros((1, 128), F32).at[0, :N_GROUPS].set(p['b_router_g']) \
        .at[0, N_GROUPS:N_GROUPS + N_EXPERTS].set(p['b_router_e'])
    h2, idx, wts = moe_router(xs, g2, mods, jnp.stack(_split_bf16(w_router)), b_router, n_ctx)
    blk_e, buf_tok, pos = moe_dispatch_plan(idx[:, :TOP_K])
    y_buf = moe_experts(h2, blk_e + first_expert, buf_tok, *expert_weights)
    return moe_combine(xs, mods, wts, pos, y_buf, final_g, n_ctx, final_norm)


def _layer(xs, cvec, p, expert_weights, first_expert, cos_t, sin_t, final_g, n_ctx, n_batch, final_norm):
    b, s, d = xs.shape
    m = adaln_mods(cvec, p['w_ada'], p['b_ada']).reshape(cvec.shape[0], 6, d)
    mods = jnp.stack([jnp.broadcast_to(m[n_batch], (n_batch, 6, d)), m[:n_batch]], axis=1)
    w_in = p['w_in']
    o1 = RWKV_IN
    o2 = o1 + GLA_IN
    o3 = o2 + ATTN_IN
    o4 = o3 + S5_IN
    hd = HEAD_DIM
    head_order = jnp.concatenate([jnp.arange(h * hd, (h + 1) * hd) for h in (0, 2, 1, 3)])
    w_att = jnp.concatenate([w_in[:, o2:o2 + ATTN_HEADS * hd][:, head_order], w_in[:, o2 + ATTN_HEADS * hd:o3]], axis=1)
    w_mix = jnp.concatenate([w_in[:, :o2], jnp.zeros((d, GLA_IN_PAD - GLA_IN), F32), w_att, w_in[:, o3:o4]],
                            axis=1).astype(BF16)
    w_gate = w_in[:, o4:].astype(BF16)
    w_branch = p['w_branch'].at[2].set(p['w_branch'][2][head_order]).astype(BF16)
    g1 = p['norm1_g'].reshape(1, d)
    ua, ub, uatt, us5 = in_proj(xs, g1, mods, w_mix, cos_t, sin_t, n_ctx)

    yaf, yar, bonus, gate = rwkv_scan(ua, rwkv_params(p), n_ctx)
    gof = gla_scan(ub, *gla_direction_params(p, 0), n_ctx, False)
    gor = gla_scan(ub, *gla_direction_params(p, 1), n_ctx, True)
    sink = jnp.zeros((1, 128), F32).at[0, :ATTN_HEADS].set(p['attn_sink'])
    yc = window_attention(uatt, sink, n_ctx)
    ysf = s5_scan(us5, s5_direction_params(p, 0), n_ctx, False)
    ysr = s5_scan(us5, s5_direction_params(p, 1), n_ctx, True)

    row = lambda z: z.reshape(1, -1)
    branch_inputs = (yaf, yar, bonus, gate, row(p['rwkv_ln_g']), gof, gor, ub, row(p['gla_ln_g']),
                     yc, us5, ysf, ysr, row(p['s5_d']), p['s5_w_glu'].astype(BF16), row(p['s5_b_glu']))
    xs = merge_branches(xs, g1, mods, w_gate, w_branch, p['w_out'].astype(BF16),
                        branch_inputs, n_ctx)
    return hier_moe_layer(xs, p['norm2_g'].reshape(1, d), mods, p, expert_weights, first_expert, final_g, n_ctx,
                          final_norm)


_LAYER_PARAMS = ('norm1_g', 'norm2_g', 'w_ada', 'b_ada', 'w_in', 'rwkv_mu', 'rwkv_w0', 'rwkv_w1', 'rwkv_w2',
                 'rwkv_a0', 'rwkv_a1', 'rwkv_a2', 'rwkv_kk', 'rwkv_ka', 'rwkv_rk', 'rwkv_g1', 'rwkv_g2',
                 'rwkv_ln_g', 'gla_a2', 'gla_ab', 'gla_ln_g', 'attn_sink', 's5_lam_re', 's5_lam_im', 's5_log_dt',
                 's5_b_re', 's5_b_im', 's5_c_re', 's5_c_im', 's5_d', 's5_w_glu', 's5_b_glu', 'w_branch', 'w_out',
                 'w_router_g', 'b_router_g', 'w_router_e', 'b_router_e', 'w_exp_gate', 'w_exp_up', 'w_exp_down')


def kernel(x, c, ctx, c_ctx, norm1_g, norm2_g, final_norm_g, w_ada, b_ada, w_in, rwkv_mu, rwkv_w0, rwkv_w1, rwkv_w2, rwkv_a0, rwkv_a1, rwkv_a2, rwkv_kk, rwkv_ka, rwkv_rk, rwkv_g1, rwkv_g2, rwkv_ln_g, gla_a2, gla_ab, gla_ln_g, attn_sink, s5_lam_re, s5_lam_im, s5_log_dt, s5_b_re, s5_b_im, s5_c_re, s5_c_im, s5_d, s5_w_glu, s5_b_glu, w_branch, w_out, w_router_g, b_router_g, w_router_e, b_router_e, w_exp_gate, w_exp_up, w_exp_down):
    stacked = dict(zip(_LAYER_PARAMS, (
        norm1_g, norm2_g, w_ada, b_ada, w_in, rwkv_mu, rwkv_w0, rwkv_w1, rwkv_w2, rwkv_a0, rwkv_a1, rwkv_a2,
        rwkv_kk, rwkv_ka, rwkv_rk, rwkv_g1, rwkv_g2, rwkv_ln_g, gla_a2, gla_ab, gla_ln_g, attn_sink,
        s5_lam_re, s5_lam_im, s5_log_dt, s5_b_re, s5_b_im, s5_c_re, s5_c_im, s5_d, s5_w_glu, s5_b_glu,
        w_branch, w_out, w_router_g, b_router_g, w_router_e, b_router_e, w_exp_gate, w_exp_up, w_exp_down)))
    n_batch, n_lat, d = x.shape
    n_ctx = ctx.shape[1]
    depth = w_in.shape[0]
    xs = jnp.concatenate([ctx, x], axis=1)
    cvec = jnp.zeros((16, d), F32).at[:n_batch].set(c).at[n_batch].set(c_ctx)
    cos_t, sin_t = rope_tables(n_ctx, n_lat)
    final_g = final_norm_g.reshape(1, d)
    expert_names = ('w_exp_gate', 'w_exp_up', 'w_exp_down')
    expert_weights = tuple(stacked[k].reshape((-1,) + stacked[k].shape[2:]) for k in expert_names)
    for l in range(depth):
        p = {k: v[l] for k, v in stacked.items() if k not in expert_names}
        xs = _layer(xs, cvec, p, expert_weights, l * N_EXPERTS, cos_t, sin_t, final_g, n_ctx, n_batch,
                    l == depth - 1)
    return xs
```
